```python
import math
import jax
import jax.numpy as jnp
from jax import lax
import numpy as np

D_MODEL = 1024
BATCH = 8
SEQ = 4096
DEPTH = 2

GRID_W = 64
CTX_LEN = 256
EPS = 1e-6

GDN_HEADS = 6
GDN_DK = 64
GDN_DV = 64
GDN_CHUNK = 64

MLA_HEADS = 6
MLA_Q_RANK = 256
MLA_KV_RANK = 128
MLA_NOPE = 64
MLA_ROPE = 32
MLA_V = 64
MLA_QK = MLA_NOPE + MLA_ROPE
ATTN_BLOCK = 128
ROPE_BASE = 10000.0
ROPE_AXIS = MLA_ROPE // 2

HY_CH = 256
HY_BANDS = 16
HY_EMB = 1 + 2 * HY_BANDS
HY_HIDDEN = 64
HY_TARGET = 1e-2
HY_FAST_DECAY_PCT = 0.3
HY_SLOW_DECAY_PCT = 1.5
HY_MAX_DECAY = math.log(HY_TARGET) / HY_FAST_DECAY_PCT
HY_MIN_DECAY = math.log(HY_TARGET) / HY_SLOW_DECAY_PCT

D_FF = 2816

GDN_WIDTH = GDN_HEADS * GDN_DV
MLA_WIDTH = MLA_HEADS * MLA_V
HY_WIDTH = HY_CH
MIX_WIDTH = GDN_WIDTH + MLA_WIDTH + HY_WIDTH

GDN_QKV = GDN_HEADS * (2 * GDN_DK + GDN_DV)
GDN_IN = GDN_QKV + GDN_WIDTH + 4 * GDN_HEADS
MLA_IN = MLA_Q_RANK + MLA_KV_RANK + MLA_ROPE
HY_IN = 3 * HY_CH
P_IN = GDN_IN + MLA_IN + HY_IN

kernel_name = 'hybrid_gdn_mla_hyena_prefix_dit'


def rms_norm(x, g):
    xf = x.astype(jnp.float32)
    y = xf * lax.rsqrt(jnp.mean(xf * xf, axis=-1, keepdims=True) + EPS)
    return (y * g.astype(jnp.float32)).astype(x.dtype)


def modulate(h, shift, scale):
    return h * (1.0 + scale) + shift


def dwconv3(x, w, b=None):
    xp = jnp.pad(x, ((0, 0), (1, 1), (0, 0)))
    y = xp[:, :-2] * w[0] + xp[:, 1:-1] * w[1] + xp[:, 2:] * w[2]
    return y if b is None else y + b


def axial_rope(l):
    rows = l // GRID_W
    row = jnp.repeat(jnp.arange(rows, dtype=jnp.float32), GRID_W)
    col = jnp.tile(jnp.arange(GRID_W, dtype=jnp.float32), rows)
    inv = ROPE_BASE ** (-jnp.arange(0, ROPE_AXIS, 2, dtype=jnp.float32) / ROPE_AXIS)
    ang = jnp.concatenate([row[:, None] * inv, col[:, None] * inv], axis=-1)
    return jnp.cos(ang), jnp.sin(ang)


def apply_rope(x, cos, sin):
    half = x.shape[-1] // 2
    x1, x2 = x[..., :half], x[..., half:]
    cos, sin = cos[None, :, None, :], sin[None, :, None, :]
    return jnp.concatenate([x1 * cos - x2 * sin, x1 * sin + x2 * cos], axis=-1)


def gdn_chunked(q, k, v, g, beta, s0):
    b, l, h, dk = q.shape
    dv = v.shape[-1]
    n = l // GDN_CHUNK

    def blocks(t):
        t = t.reshape((b, n, GDN_CHUNK, h) + t.shape[3:])
        return jnp.moveaxis(t, 3, 1)

    q, k, v, g, beta = (blocks(t) for t in (q, k, v, g, beta))
    gc = jnp.cumsum(g, axis=-1)
    idx = jnp.arange(GDN_CHUNK)
    incl = idx[:, None] >= idx[None, :]
    strict = idx[:, None] > idx[None, :]
    diff = gc[..., :, None] - gc[..., None, :]
    dec_incl = jnp.where(incl, jnp.exp(jnp.where(incl, diff, 0.0)), 0.0)
    dec_strict = jnp.where(strict, dec_incl, 0.0)
    a_mat = beta[..., :, None] * jnp.einsum('bhnik,bhnjk->bhnij', k, k) * dec_strict
    eg = jnp.exp(gc)[..., None]
    rhs = beta[..., None] * jnp.concatenate([v, k * eg], axis=-1)
    sol = lax.linalg.triangular_solve(a_mat, rhs, left_side=True, lower=True, unit_diagonal=True)
    u, w = sol[..., :dv], sol[..., dv:]
    qk = jnp.einsum('bhnik,bhnjk->bhnij', q, k) * dec_incl
    q_dec = q * eg
    k_dec = k * jnp.exp(gc[..., -1:] - gc)[..., None]
    c_dec = jnp.exp(gc[..., -1])
    xs = tuple(jnp.moveaxis(t, 2, 0) for t in (u, w, q_dec, qk, k_dec, c_dec))

    def step(s, inp):
        u_c, w_c, q_c, qk_c, k_c, d_c = inp
        delta = u_c - jnp.einsum('bhck,bhkv->bhcv', w_c, s)
        o_c = jnp.einsum('bhck,bhkv->bhcv', q_c, s) + jnp.einsum('bhij,bhjv->bhiv', qk_c, delta)
        s = d_c[..., None, None] * s + jnp.einsum('bhck,bhcv->bhkv', k_c, delta)
        return s, o_c

    s_fin, o = lax.scan(step, s0, xs)
    o = jnp.moveaxis(o, 0, 2).reshape(b, h, l, dv)
    return jnp.moveaxis(o, 1, 2), s_fin


def gdn_prep(p, conv_w, a_log, dt_bias):
    b, l, _ = p.shape
    p = p.astype(jnp.float32)
    qkv = jax.nn.silu(dwconv3(p[..., :GDN_QKV], conv_w.astype(jnp.float32)))
    hk = GDN_HEADS * GDN_DK
    q = qkv[..., :hk].reshape(b, l, GDN_HEADS, GDN_DK)
    k = qkv[..., hk:2 * hk].reshape(b, l, GDN_HEADS, GDN_DK)
    v = qkv[..., 2 * hk:].reshape(b, l, GDN_HEADS, GDN_DV)
    q = q * lax.rsqrt(jnp.sum(q * q, axis=-1, keepdims=True) + EPS) * (GDN_DK ** -0.5)
    k = k * lax.rsqrt(jnp.sum(k * k, axis=-1, keepdims=True) + EPS)
    z = p[..., GDN_QKV:GDN_QKV + GDN_WIDTH].reshape(b, l, GDN_HEADS, GDN_DV)
    ab = p[..., GDN_QKV + GDN_WIDTH:].reshape(b, l, 2, 2, GDN_HEADS)
    g = -jnp.exp(a_log.astype(jnp.float32)) * jax.nn.softplus(ab[:, :, 0] + dt_bias.astype(jnp.float32))
    beta = jax.nn.sigmoid(ab[:, :, 1])
    return q, k, v, z, g, beta


def gdn_mixer(p_ctx, p_lat, conv_w, a_log, dt_bias, norm_g):
    qc, kc, vc, zc, gc, bc = gdn_prep(p_ctx, conv_w, a_log, dt_bias)
    ql, kl, vl, zl, gl, bl = gdn_prep(p_lat, conv_w, a_log, dt_bias)
    s0 = jnp.zeros((ql.shape[0], GDN_HEADS, GDN_DK, GDN_DV), jnp.float32)
    same = lambda t: t
    rev = lambda t: t[:, ::-1]
    outs_c, outs_l = [], []
    for d, fl in ((0, same), (1, rev)):
        oc, s_ctx = gdn_chunked(fl(qc), fl(kc), fl(vc), fl(gc[:, :, d]), fl(bc[:, :, d]), s0)
        ol, _ = gdn_chunked(fl(ql), fl(kl), fl(vl), fl(gl[:, :, d]), fl(bl[:, :, d]), s_ctx)
        outs_c.append(fl(oc))
        outs_l.append(fl(ol))

    def gated_out(o, z):
        y = rms_norm(o, norm_g) * jax.nn.silu(z)
        return y.reshape(o.shape[0], o.shape[1], GDN_WIDTH)

    return gated_out(outs_c[0] + outs_c[1], zc), gated_out(outs_l[0] + outs_l[1], zl)


def mla_heads(p, q_norm_g, w_uq, kv_norm_g, w_ukv, q_head_g, k_head_g, rope):
    b, l, _ = p.shape
    c_q = rms_norm(p[..., :MLA_Q_RANK], q_norm_g)
    c_kv = rms_norm(p[..., MLA_Q_RANK:MLA_Q_RANK + MLA_KV_RANK], kv_norm_g)
    k_rope = p[..., MLA_Q_RANK + MLA_KV_RANK:]
    q = (c_q @ w_uq).reshape(b, l, MLA_HEADS, MLA_QK)
    kv = (c_kv @ w_ukv).reshape(b, l, MLA_HEADS, MLA_NOPE + MLA_V)
    k = jnp.concatenate([kv[..., :MLA_NOPE],
                         jnp.broadcast_to(k_rope[:, :, None, :], (b, l, MLA_HEADS, MLA_ROPE))], axis=-1)
    v = kv[..., MLA_NOPE:]
    q = rms_norm(q, q_head_g)
    k = rms_norm(k, k_head_g)
    if rope is not None:
        cos, sin = rope
        q = jnp.concatenate([q[..., :MLA_NOPE], apply_rope(q[..., MLA_NOPE:], cos, sin)], axis=-1)
        k = jnp.concatenate([k[..., :MLA_NOPE], apply_rope(k[..., MLA_NOPE:], cos, sin)], axis=-1)
    return q, k, v


def softmax_attend(q, k, v, scale):
    s = jnp.einsum('bqhd,bkhd->bhqk', q, k).astype(jnp.float32) * scale
    p = jax.nn.softmax(s, axis=-1).astype(v.dtype)
    return jnp.einsum('bhqk,bkhd->bqhd', p, v)


def mla_mixer(p_ctx, p_lat, q_norm_g, w_uq, kv_norm_g, w_ukv, q_head_g, k_head_g, rope, with_ctx):
    qc, kc, vc = mla_heads(p_ctx, q_norm_g, w_uq, kv_norm_g, w_ukv, q_head_g, k_head_g, None)
    ql, kl, vl = mla_heads(p_lat, q_norm_g, w_uq, kv_norm_g, w_ukv, q_head_g, k_head_g, rope)
    scale = MLA_QK ** -0.5
    k_all = jnp.concatenate([kl, kc], axis=1)
    v_all = jnp.concatenate([vl, vc], axis=1)
    b, l = ql.shape[0], ql.shape[1]
    nb = l // ATTN_BLOCK
    q_blocks = jnp.moveaxis(ql.reshape(b, nb, ATTN_BLOCK, MLA_HEADS, MLA_QK), 1, 0)
    o_lat = lax.map(lambda qb: softmax_attend(qb, k_all, v_all, scale), q_blocks)
    o_lat = jnp.moveaxis(o_lat, 0, 1).reshape(b, l, MLA_WIDTH)
    if not with_ctx:
        return None, o_lat
    o_ctx = softmax_attend(qc, kc, vc, scale).reshape(b, qc.shape[1], MLA_WIDTH)
    return o_ctx, o_lat


def hyena_filter(l, w1, b1, w2, b2, w3, b3):
    t = jnp.arange(l, dtype=jnp.float32)
    t_norm = t / max(l - 1, 1)
    bands = jnp.linspace(1e-4, HY_BANDS - 1, HY_BANDS, dtype=jnp.float32)
    ang = 2.0 * math.pi * t[:, None] * bands[None, :] / l
    z = jnp.concatenate([t_norm[:, None], jnp.cos(ang), jnp.sin(ang)], axis=-1)
    h = jnp.sin(z @ w1 + b1)
    h = jnp.sin(h @ w2 + b2)
    h = (h @ w3 + b3).reshape(l, 2, HY_CH).astype(jnp.float32)
    deltas = jnp.abs(jnp.linspace(HY_MIN_DECAY, HY_MAX_DECAY, HY_CH, dtype=jnp.float32))
    h = h * jnp.exp(-t_norm[:, None, None] * deltas)
    buf = jnp.concatenate([h[:, 0], jnp.zeros((1, HY_CH), jnp.float32), h[:0:-1, 1]], axis=0)
    return buf / jnp.sum(jnp.abs(buf), axis=0, keepdims=True)


def hyena_mixer(p, conv_w, conv_b, w1, b1, w2, b2, w3, b3, d_skip):
    b, l, _ = p.shape
    u = dwconv3(p, conv_w, conv_b)
    x0, x1, v = u[..., :HY_CH], u[..., HY_CH:2 * HY_CH], u[..., 2 * HY_CH:]
    z = (v * x1).astype(jnp.float32)
    buf = hyena_filter(l, w1, b1, w2, b2, w3, b3)
    zf = jnp.fft.rfft(z, n=2 * l, axis=1)
    hf = jnp.fft.rfft(buf, n=2 * l, axis=0)
    y = jnp.fft.irfft(zf * hf[None], n=2 * l, axis=1)[:, :l] + z * d_skip
    return x0 * y.astype(x0.dtype)


def conv_ffn(h, w_up, conv_w, conv_b, w_down):
    up = h @ w_up
    gate = dwconv3(up[..., :D_FF], conv_w, conv_b)
    return (jax.nn.silu(gate) * up[..., D_FF:]) @ w_down


def setup_inputs(seed: int = 0) -> dict:
    key = jax.random.key(seed)
    ks = jax.random.split(key, 33)

    def nrm(i, shape, scale):
        return jax.random.normal(ks[i], shape, jnp.float32) * scale

    def gain(i, n):
        return 1.0 + nrm(i, (DEPTH, n), 0.05)

    a_val = jax.random.uniform(ks[9], (DEPTH, 2, GDN_HEADS), jnp.float32, 1.0, 16.0)
    dt = jnp.exp(jax.random.uniform(ks[10], (DEPTH, 2, GDN_HEADS), jnp.float32,
                                    math.log(1e-3), math.log(1e-1)))
    return {
        'x': nrm(0, (BATCH, SEQ, D_MODEL), 1.0),
        'c': nrm(1, (BATCH, D_MODEL), 1.0),
        'ctx': nrm(2, (BATCH, CTX_LEN, D_MODEL), 1.0),
        'c_ctx': nrm(3, (D_MODEL,), 1.0),
        'ada_w': nrm(4, (DEPTH, D_MODEL, 6 * D_MODEL), 0.5 * D_MODEL ** -0.5),
        'ada_b': nrm(5, (DEPTH, 6 * D_MODEL), 0.02),
        'mix_norm_g': gain(6, D_MODEL),
        'w_in': nrm(7, (DEPTH, D_MODEL, P_IN), D_MODEL ** -0.5),
        'gdn_conv_w': nrm(8, (DEPTH, 3, GDN_QKV), 3 ** -0.5),
        'gdn_a_log': jnp.log(a_val),
        'gdn_dt_bias': dt + jnp.log(-jnp.expm1(-dt)),
        'gdn_norm_g': gain(11, GDN_DV),
        'mla_q_norm_g': gain(12, MLA_Q_RANK),
        'mla_w_uq': nrm(13, (DEPTH, MLA_Q_RANK, MLA_HEADS * MLA_QK), MLA_Q_RANK ** -0.5),
        'mla_kv_norm_g': gain(14, MLA_KV_RANK),
        'mla_w_ukv': nrm(15, (DEPTH, MLA_KV_RANK, MLA_HEADS * (MLA_NOPE + MLA_V)), MLA_KV_RANK ** -0.5),
        'mla_q_head_g': gain(16, MLA_QK),
        'mla_k_head_g': gain(17, MLA_QK),
        'hy_conv_w': nrm(18, (DEPTH, 3, HY_IN), 3 ** -0.5),
        'hy_conv_b': nrm(19, (DEPTH, HY_IN), 0.02),
        'hy_w1': nrm(20, (DEPTH, HY_EMB, HY_HIDDEN), HY_EMB ** -0.5),
        'hy_b1': nrm(21, (DEPTH, HY_HIDDEN), 0.02),
        'hy_w2': nrm(22, (DEPTH, HY_HIDDEN, HY_HIDDEN), HY_HIDDEN ** -0.5),
        'hy_b2': nrm(23, (DEPTH, HY_HIDDEN), 0.02),
        'hy_w3': nrm(24, (DEPTH, HY_HIDDEN, 2 * HY_CH), HY_HIDDEN ** -0.5),
        'hy_b3': nrm(25, (DEPTH, 2 * HY_CH), 0.02),
        'hy_d': nrm(26, (DEPTH, HY_CH), 1.0),
        'w_out': nrm(27, (DEPTH, MIX_WIDTH, D_MODEL), MIX_WIDTH ** -0.5),
        'ffn_norm_g': gain(28, D_MODEL),
        'ffn_w_up': nrm(29, (DEPTH, D_MODEL, 2 * D_FF), D_MODEL ** -0.5),
        'ffn_conv_w': nrm(30, (DEPTH, 3, D_FF), 3 ** -0.5),
        'ffn_conv_b': nrm(31, (DEPTH, D_FF), 0.02),
        'ffn_w_down': nrm(32, (DEPTH, D_FF, D_MODEL), D_FF ** -0.5),
    }


def reference(x, c, ctx, c_ctx, ada_w, ada_b, mix_norm_g, w_in, gdn_conv_w, gdn_a_log, gdn_dt_bias,
              gdn_norm_g, mla_q_norm_g, mla_w_uq, mla_kv_norm_g, mla_w_ukv, mla_q_head_g, mla_k_head_g,
              hy_conv_w, hy_conv_b, hy_w1, hy_b1, hy_w2, hy_b2, hy_w3, hy_b3, hy_d, w_out,
              ffn_norm_g, ffn_w_up, ffn_conv_w, ffn_conv_b, ffn_w_down):
    rope = axial_rope(x.shape[1])
    s1, s2 = GDN_IN, GDN_IN + MLA_IN
    for i in range(DEPTH):
        last = i == DEPTH - 1
        mod_lat = (jax.nn.silu(c) @ ada_w[i] + ada_b[i])[:, None, :]
        mod_ctx = (jax.nn.silu(c_ctx) @ ada_w[i] + ada_b[i])[None, None, :]
        sa_l, ca_l, ga_l, sf_l, cf_l, gf_l = jnp.split(mod_lat, 6, axis=-1)
        sa_c, ca_c, ga_c, sf_c, cf_c, gf_c = jnp.split(mod_ctx, 6, axis=-1)

        p_lat = modulate(rms_norm(x, mix_norm_g[i]), sa_l, ca_l) @ w_in[i]
        p_ctx = modulate(rms_norm(ctx, mix_norm_g[i]), sa_c, ca_c) @ w_in[i]

        gdn_c, gdn_l = gdn_mixer(p_ctx[..., :s1], p_lat[..., :s1], gdn_conv_w[i], gdn_a_log[i],
                                 gdn_dt_bias[i], gdn_norm_g[i])
        mla_c, mla_l = mla_mixer(p_ctx[..., s1:s2], p_lat[..., s1:s2], mla_q_norm_g[i], mla_w_uq[i],
                                 mla_kv_norm_g[i], mla_w_ukv[i], mla_q_head_g[i], mla_k_head_g[i],
                                 rope, not last)
        hy_l = hyena_mixer(p_lat[..., s2:], hy_conv_w[i], hy_conv_b[i], hy_w1[i], hy_b1[i],
                           hy_w2[i], hy_b2[i], hy_w3[i], hy_b3[i], hy_d[i])
        mix_l = jnp.concatenate([gdn_l, mla_l, hy_l], axis=-1).astype(x.dtype)
        x = x + ga_l * (mix_l @ w_out[i])
        h_l = modulate(rms_norm(x, ffn_norm_g[i]), sf_l, cf_l)
        x = x + gf_l * conv_ffn(h_l, ffn_w_up[i], ffn_conv_w[i], ffn_conv_b[i], ffn_w_down[i])

        if not last:
            hy_c = hyena_mixer(p_ctx[..., s2:], hy_conv_w[i], hy_conv_b[i], hy_w1[i], hy_b1[i],
                               hy_w2[i], hy_b2[i], hy_w3[i], hy_b3[i], hy_d[i])
            mix_c = jnp.concatenate([gdn_c, mla_c, hy_c], axis=-1).astype(ctx.dtype)
            ctx = ctx + ga_c * (mix_c @ w_out[i])
            h_c = modulate(rms_norm(ctx, ffn_norm_g[i]), sf_c, cf_c)
            ctx = ctx + gf_c * conv_ffn(h_c, ffn_w_up[i], ffn_conv_w[i], ffn_conv_b[i], ffn_w_down[i])
    return x
```

```python
import functools
import math

import jax
import jax.numpy as jnp
import numpy as np
from jax import lax
from jax.experimental import pallas as pl
from jax.experimental.pallas import tpu as pltpu

D_MODEL = 1024
DEPTH = 2
GRID_W = 64
EPS = 1e-6

GDN_HEADS = 6
GDN_DK = 64
GDN_DV = 64
GDN_CHUNK = 64

MLA_HEADS = 6
MLA_Q_RANK = 256
MLA_KV_RANK = 128
MLA_NOPE = 64
MLA_ROPE = 32
MLA_V = 64
MLA_QK = MLA_NOPE + MLA_ROPE
ATTN_BLOCK = 128
ROPE_BASE = 10000.0
ROPE_AXIS = MLA_ROPE // 2

HY_CH = 256
HY_BANDS = 16
HY_EMB = 1 + 2 * HY_BANDS
HY_HIDDEN = 64
HY_TARGET = 1e-2
HY_FAST_DECAY_PCT = 0.3
HY_SLOW_DECAY_PCT = 1.5
HY_MAX_DECAY = math.log(HY_TARGET) / HY_FAST_DECAY_PCT
HY_MIN_DECAY = math.log(HY_TARGET) / HY_SLOW_DECAY_PCT

D_FF = 2816

GDN_WIDTH = GDN_HEADS * GDN_DV
MLA_WIDTH = MLA_HEADS * MLA_V
HY_WIDTH = HY_CH
GDN_QKV = GDN_HEADS * (2 * GDN_DK + GDN_DV)
GDN_IN = GDN_QKV + GDN_WIDTH + 4 * GDN_HEADS
MLA_IN = MLA_Q_RANK + MLA_KV_RANK + MLA_ROPE
HY_IN = 3 * HY_CH

LANES = 128
SUBLANES_BF16 = 16
VMEM_LIMIT_BYTES = 56 * 1024 * 1024

AB_PAD = LANES
MLA_PAD = 512
IN_GROUPS = (GDN_QKV, GDN_WIDTH, AB_PAD, MLA_PAD, HY_IN)
IN_TOTAL = sum(IN_GROUPS)

FFN_CHUNK = 256
HALO = SUBLANES_BF16

_BF16 = jnp.bfloat16
_F32 = jnp.float32


def _norm_mod(x, g, shift, scale):
    ms = jnp.mean(x * x, axis=-1, keepdims=True)
    y = x * lax.rsqrt(ms + EPS) * g
    return y * (1.0 + scale) + shift


def _in_proj_kernel(x_ref, g_ref, shift_ref, scale_ref, w_ref, *out_refs):
    h = _norm_mod(x_ref[0], g_ref[...], shift_ref[0], scale_ref[0])
    p = jnp.dot(h.astype(_BF16), w_ref[...], preferred_element_type=_F32)
    off = 0
    for o_ref, n in zip(out_refs, IN_GROUPS):
        o_ref[0] = p[:, off:off + n]
        off += n


def _in_proj(x, g, shift, scale, w_pad, tm):
    b, l, d = x.shape
    grid = (b, l // tm)
    vec = pl.BlockSpec((1, 1, d), lambda i, j: (i, 0, 0))
    return pl.pallas_call(
        _in_proj_kernel,
        grid=grid,
        in_specs=[
            pl.BlockSpec((1, tm, d), lambda i, j: (i, j, 0)),
            pl.BlockSpec((1, d), lambda i, j: (0, 0)),
            vec, vec,
            pl.BlockSpec((d, IN_TOTAL), lambda i, j: (0, 0)),
        ],
        out_specs=[pl.BlockSpec((1, tm, n), lambda i, j: (i, j, 0)) for n in IN_GROUPS],
        out_shape=[jax.ShapeDtypeStruct((b, l, n), _F32) for n in IN_GROUPS],
        compiler_params=pltpu.CompilerParams(
            dimension_semantics=("parallel", "parallel"), vmem_limit_bytes=VMEM_LIMIT_BYTES),
        name="in_proj",
    )(x, g, shift, scale, w_pad)


def _pad_w_in(w_in):
    s1 = GDN_QKV + GDN_WIDTH
    s2 = GDN_IN
    s3 = GDN_IN + MLA_IN
    d = w_in.shape[0]
    z = lambda n: jnp.zeros((d, n), w_in.dtype)
    parts = [w_in[:, :s1], w_in[:, s1:s2], z(AB_PAD - 4 * GDN_HEADS),
             w_in[:, s2:s3], z(MLA_PAD - MLA_IN), w_in[:, s3:]]
    return jnp.concatenate(parts, axis=1).astype(_BF16)


def _out_proj_kernel(x_ref, a_ref, b_ref, c_ref, ga_ref, w_ref, o_ref):
    mix = jnp.concatenate([a_ref[0], b_ref[0], c_ref[0]], axis=-1).astype(_BF16)
    y = jnp.dot(mix, w_ref[...], preferred_element_type=_F32)
    o_ref[0] = x_ref[0] + ga_ref[0] * y


def _out_proj(x, gdn, mla, hy, ga, w_out, tm):
    b, l, d = x.shape
    row = lambda n: pl.BlockSpec((1, tm, n), lambda i, j: (i, j, 0))
    return pl.pallas_call(
        _out_proj_kernel,
        grid=(b, l // tm),
        in_specs=[row(d), row(GDN_WIDTH), row(MLA_WIDTH), row(HY_WIDTH),
                  pl.BlockSpec((1, 1, d), lambda i, j: (i, 0, 0)),
                  pl.BlockSpec((d, d), lambda i, j: (0, 0))],
        out_specs=row(d),
        out_shape=jax.ShapeDtypeStruct((b, l, d), _F32),
        compiler_params=pltpu.CompilerParams(
            dimension_semantics=("parallel", "parallel"), vmem_limit_bytes=VMEM_LIMIT_BYTES),
        name="out_proj",
    )(x, gdn, mla, hy, ga, w_out)


def _ffn_kernel(xp_ref, x_ref, xn_ref, g_ref, sf_ref, cf_ref, gf_ref, wup_ref, cw_ref, cb_ref, wdn_ref,
                o_ref, h_ref, up_ref, act_ref, *, tl):
    i = pl.program_id(1)
    nt = pl.num_programs(1)
    g, sf, cf = g_ref[...], sf_ref[0], cf_ref[0]
    pv = (i > 0).astype(_F32)
    nv = (i < nt - 1).astype(_F32)
    h_ref[0:HALO] = (_norm_mod(xp_ref[0], g, sf, cf) * pv).astype(_BF16)
    h_ref[HALO:HALO + tl] = _norm_mod(x_ref[0], g, sf, cf).astype(_BF16)
    h_ref[HALO + tl:] = (_norm_mod(xn_ref[0], g, sf, cf) * nv).astype(_BF16)

    def chunk(c, carry):
        lo = pl.multiple_of(c * FFN_CHUNK, FFN_CHUNK)
        wg = wup_ref[:, pl.ds(lo, FFN_CHUNK)]
        wv = wup_ref[:, pl.ds(D_FF + lo, FFN_CHUNK)]
        ug = jnp.dot(h_ref[...], wg, preferred_element_type=_F32)
        up_ref[...] = ug
        uv = jnp.dot(h_ref[HALO:HALO + tl], wv, preferred_element_type=_F32)
        cw = cw_ref[:, pl.ds(lo, FFN_CHUNK)]
        cb = cb_ref[:, pl.ds(lo, FFN_CHUNK)]
        gate = (up_ref[HALO - 1:HALO - 1 + tl] * cw[0:1] + ug[HALO:HALO + tl] * cw[1:2]
                + up_ref[HALO + 1:HALO + 1 + tl] * cw[2:3] + cb)
        act = gate * jax.nn.sigmoid(gate) * uv
        act_ref[:, pl.ds(lo, FFN_CHUNK)] = act.astype(_BF16)
        return carry

    lax.fori_loop(0, D_FF // FFN_CHUNK, chunk, 0)
    y = jnp.dot(act_ref[...], wdn_ref[...], preferred_element_type=_F32)
    o_ref[0] = x_ref[0] + gf_ref[0] * y


def _ffn(x, g, sf, cf, gf, w_up, conv_w, conv_b, w_down, tl):
    b, l, d = x.shape
    nh = tl // HALO
    last = l // HALO - 1
    vec = pl.BlockSpec((1, 1, d), lambda i, j: (i, 0, 0))
    whole = lambda a: pl.BlockSpec(a.shape, lambda i, j: (0,) * a.ndim)
    return pl.pallas_call(
        functools.partial(_ffn_kernel, tl=tl),
        grid=(b, l // tl),
        in_specs=[
            pl.BlockSpec((1, HALO, d), lambda i, j: (i, jnp.maximum(j * nh - 1, 0), 0)),
            pl.BlockSpec((1, tl, d), lambda i, j: (i, j, 0)),
            pl.BlockSpec((1, HALO, d), lambda i, j: (i, jnp.minimum((j + 1) * nh, last), 0)),
            pl.BlockSpec((1, d), lambda i, j: (0, 0)),
            vec, vec, vec,
            whole(w_up), whole(conv_w), whole(conv_b), whole(w_down),
        ],
        out_specs=pl.BlockSpec((1, tl, d), lambda i, j: (i, j, 0)),
        out_shape=jax.ShapeDtypeStruct((b, l, d), _F32),
        scratch_shapes=[
            pltpu.VMEM((tl + 2 * HALO, d), _BF16),
            pltpu.VMEM((tl + 2 * HALO, FFN_CHUNK), _F32),
            pltpu.VMEM((tl, D_FF), _BF16),
        ],
        compiler_params=pltpu.CompilerParams(
            dimension_semantics=("parallel", "parallel"), vmem_limit_bytes=VMEM_LIMIT_BYTES),
        name="conv_ffn",
    )(x, x, x, g, sf, cf, gf, w_up, conv_w, conv_b, w_down)


def rms_norm(x, g):
    xf = x.astype(jnp.float32)
    y = xf * lax.rsqrt(jnp.mean(xf * xf, axis=-1, keepdims=True) + EPS)
    return (y * g.astype(jnp.float32)).astype(x.dtype)


def dwconv3(x, w, b=None):
    xp = jnp.pad(x, ((0, 0), (1, 1), (0, 0)))
    y = xp[:, :-2] * w[0] + xp[:, 1:-1] * w[1] + xp[:, 2:] * w[2]
    return y if b is None else y + b


def axial_rope(l):
    rows = l // GRID_W
    row = jnp.repeat(jnp.arange(rows, dtype=jnp.float32), GRID_W)
    col = jnp.tile(jnp.arange(GRID_W, dtype=jnp.float32), rows)
    inv = ROPE_BASE ** (-jnp.arange(0, ROPE_AXIS, 2, dtype=jnp.float32) / ROPE_AXIS)
    ang = jnp.concatenate([row[:, None] * inv, col[:, None] * inv], axis=-1)
    return jnp.cos(ang), jnp.sin(ang)


def apply_rope(x, cos, sin):
    half = x.shape[-1] // 2
    x1, x2 = x[..., :half], x[..., half:]
    cos, sin = cos[None, :, None, :], sin[None, :, None, :]
    return jnp.concatenate([x1 * cos - x2 * sin, x1 * sin + x2 * cos], axis=-1)


def gdn_chunked(q, k, v, g, beta, s0):
    b, l, h, dk = q.shape
    dv = v.shape[-1]
    n = l // GDN_CHUNK

    def blocks(t):
        t = t.reshape((b, n, GDN_CHUNK, h) + t.shape[3:])
        return jnp.moveaxis(t, 3, 1)

    q, k, v, g, beta = (blocks(t) for t in (q, k, v, g, beta))
    gc = jnp.cumsum(g, axis=-1)
    idx = jnp.arange(GDN_CHUNK)
    incl = idx[:, None] >= idx[None, :]
    strict = idx[:, None] > idx[None, :]
    diff = gc[..., :, None] - gc[..., None, :]
    dec_incl = jnp.where(incl, jnp.exp(jnp.where(incl, diff, 0.0)), 0.0)
    dec_strict = jnp.where(strict, dec_incl, 0.0)
    a_mat = beta[..., :, None] * jnp.einsum('bhnik,bhnjk->bhnij', k, k) * dec_strict
    eg = jnp.exp(gc)[..., None]
    rhs = beta[..., None] * jnp.concatenate([v, k * eg], axis=-1)
    sol = lax.linalg.triangular_solve(a_mat, rhs, left_side=True, lower=True, unit_diagonal=True)
    u, w = sol[..., :dv], sol[..., dv:]
    qk = jnp.einsum('bhnik,bhnjk->bhnij', q, k) * dec_incl
    q_dec = q * eg
    k_dec = k * jnp.exp(gc[..., -1:] - gc)[..., None]
    c_dec = jnp.exp(gc[..., -1])
    xs = tuple(jnp.moveaxis(t, 2, 0) for t in (u, w, q_dec, qk, k_dec, c_dec))

    def step(s, inp):
        u_c, w_c, q_c, qk_c, k_c, d_c = inp
        delta = u_c - jnp.einsum('bhck,bhkv->bhcv', w_c, s)
        o_c = jnp.einsum('bhck,bhkv->bhcv', q_c, s) + jnp.einsum('bhij,bhjv->bhiv', qk_c, delta)
        s = d_c[..., None, None] * s + jnp.einsum('bhck,bhcv->bhkv', k_c, delta)
        return s, o_c

    s_fin, o = lax.scan(step, s0, xs)
    o = jnp.moveaxis(o, 0, 2).reshape(b, h, l, dv)
    return jnp.moveaxis(o, 1, 2), s_fin


def gdn_prep(qkv_in, z_in, ab_in, conv_w, a_log, dt_bias):
    b, l, _ = qkv_in.shape
    qkv = jax.nn.silu(dwconv3(qkv_in, conv_w.astype(jnp.float32)))
    hk = GDN_HEADS * GDN_DK
    q = qkv[..., :hk].reshape(b, l, GDN_HEADS, GDN_DK)
    k = qkv[..., hk:2 * hk].reshape(b, l, GDN_HEADS, GDN_DK)
    v = qkv[..., 2 * hk:].reshape(b, l, GDN_HEADS, GDN_DV)
    q = q * lax.rsqrt(jnp.sum(q * q, axis=-1, keepdims=True) + EPS) * (GDN_DK ** -0.5)
    k = k * lax.rsqrt(jnp.sum(k * k, axis=-1, keepdims=True) + EPS)
    z = z_in.reshape(b, l, GDN_HEADS, GDN_DV)
    ab = ab_in[..., :4 * GDN_HEADS].reshape(b, l, 2, 2, GDN_HEADS)
    g = -jnp.exp(a_log.astype(jnp.float32)) * jax.nn.softplus(ab[:, :, 0] + dt_bias.astype(jnp.float32))
    beta = jax.nn.sigmoid(ab[:, :, 1])
    return q, k, v, z, g, beta


def gdn_mixer(pc, pl_, conv_w, a_log, dt_bias, norm_g):
    qc, kc, vc, zc, gc, bc = gdn_prep(*pc, conv_w, a_log, dt_bias)
    ql, kl, vl, zl, gl, bl = gdn_prep(*pl_, conv_w, a_log, dt_bias)
    s0 = jnp.zeros((ql.shape[0], GDN_HEADS, GDN_DK, GDN_DV), jnp.float32)
    same = lambda t: t
    rev = lambda t: t[:, ::-1]
    outs_c, outs_l = [], []
    for d, fl in ((0, same), (1, rev)):
        oc, s_ctx = gdn_chunked(fl(qc), fl(kc), fl(vc), fl(gc[:, :, d]), fl(bc[:, :, d]), s0)
        ol, _ = gdn_chunked(fl(ql), fl(kl), fl(vl), fl(gl[:, :, d]), fl(bl[:, :, d]), s_ctx)
        outs_c.append(fl(oc))
        outs_l.append(fl(ol))

    def gated_out(o, z):
        y = rms_norm(o, norm_g) * jax.nn.silu(z)
        return y.reshape(o.shape[0], o.shape[1], GDN_WIDTH)

    return gated_out(outs_c[0] + outs_c[1], zc), gated_out(outs_l[0] + outs_l[1], zl)


def mla_heads(p, q_norm_g, w_uq, kv_norm_g, w_ukv, q_head_g, k_head_g, rope):
    b, l, _ = p.shape
    c_q = rms_norm(p[..., :MLA_Q_RANK], q_norm_g)
    c_kv = rms_norm(p[..., MLA_Q_RANK:MLA_Q_RANK + MLA_KV_RANK], kv_norm_g)
    k_rope = p[..., MLA_Q_RANK + MLA_KV_RANK:MLA_IN]
    q = (c_q @ w_uq).reshape(b, l, MLA_HEADS, MLA_QK)
    kv = (c_kv @ w_ukv).reshape(b, l, MLA_HEADS, MLA_NOPE + MLA_V)
    k = jnp.concatenate([kv[..., :MLA_NOPE],
                         jnp.broadcast_to(k_rope[:, :, None, :], (b, l, MLA_HEADS, MLA_ROPE))], axis=-1)
    v = kv[..., MLA_NOPE:]
    q = rms_norm(q, q_head_g)
    k = rms_norm(k, k_head_g)
    if rope is not None:
        cos, sin = rope
        q = jnp.concatenate([q[..., :MLA_NOPE], apply_rope(q[..., MLA_NOPE:], cos, sin)], axis=-1)
        k = jnp.concatenate([k[..., :MLA_NOPE], apply_rope(k[..., MLA_NOPE:], cos, sin)], axis=-1)
    return q, k, v


def softmax_attend(q, k, v, scale):
    s = jnp.einsum('bqhd,bkhd->bhqk', q, k).astype(jnp.float32) * scale
    p = jax.nn.softmax(s, axis=-1).astype(v.dtype)
    return jnp.einsum('bhqk,bkhd->bqhd', p, v)


def mla_mixer(p_ctx, p_lat, q_norm_g, w_uq, kv_norm_g, w_ukv, q_head_g, k_head_g, rope, with_ctx):
    qc, kc, vc = mla_heads(p_ctx, q_norm_g, w_uq, kv_norm_g, w_ukv, q_head_g, k_head_g, None)
    ql, kl, vl = mla_heads(p_lat, q_norm_g, w_uq, kv_norm_g, w_ukv, q_head_g, k_head_g, rope)
    scale = MLA_QK ** -0.5
    k_all = jnp.concatenate([kl, kc], axis=1)
    v_all = jnp.concatenate([vl, vc], axis=1)
    b, l = ql.shape[0], ql.shape[1]
    nb = l // ATTN_BLOCK
    q_blocks = jnp.moveaxis(ql.reshape(b, nb, ATTN_BLOCK, MLA_HEADS, MLA_QK), 1, 0)
    o_lat = lax.map(lambda qb: softmax_attend(qb, k_all, v_all, scale), q_blocks)
    o_lat = jnp.moveaxis(o_lat, 0, 1).reshape(b, l, MLA_WIDTH)
    if not with_ctx:
        return None, o_lat
    o_ctx = softmax_attend(qc, kc, vc, scale).reshape(b, qc.shape[1], MLA_WIDTH)
    return o_ctx, o_lat


def hyena_filter(l, w1, b1, w2, b2, w3, b3):
    t = jnp.arange(l, dtype=jnp.float32)
    t_norm = t / max(l - 1, 1)
    bands = jnp.linspace(1e-4, HY_BANDS - 1, HY_BANDS, dtype=jnp.float32)
    ang = 2.0 * math.pi * t[:, None] * bands[None, :] / l
    z = jnp.concatenate([t_norm[:, None], jnp.cos(ang), jnp.sin(ang)], axis=-1)
    h = jnp.sin(z @ w1 + b1)
    h = jnp.sin(h @ w2 + b2)
    h = (h @ w3 + b3).reshape(l, 2, HY_CH).astype(jnp.float32)
    deltas = jnp.abs(jnp.linspace(HY_MIN_DECAY, HY_MAX_DECAY, HY_CH, dtype=jnp.float32))
    h = h * jnp.exp(-t_norm[:, None, None] * deltas)
    buf = jnp.concatenate([h[:, 0], jnp.zeros((1, HY_CH), jnp.float32), h[:0:-1, 1]], axis=0)
    return buf / jnp.sum(jnp.abs(buf), axis=0, keepdims=True)


def hyena_mixer(p, conv_w, conv_b, w1, b1, w2, b2, w3, b3, d_skip):
    b, l, _ = p.shape
    u = dwconv3(p, conv_w, conv_b)
    x0, x1, v = u[..., :HY_CH], u[..., HY_CH:2 * HY_CH], u[..., 2 * HY_CH:]
    z = (v * x1).astype(jnp.float32)
    buf = hyena_filter(l, w1, b1, w2, b2, w3, b3)
    zf = jnp.fft.rfft(z, n=2 * l, axis=1)
    hf = jnp.fft.rfft(buf, n=2 * l, axis=0)
    y = jnp.fft.irfft(zf * hf[None], n=2 * l, axis=1)[:, :l] + z * d_skip
    return x0 * y.astype(x0.dtype)


def kernel(x, c, ctx, c_ctx, ada_w, ada_b, mix_norm_g, w_in, gdn_conv_w, gdn_a_log, gdn_dt_bias, gdn_norm_g, mla_q_norm_g, mla_w_uq, mla_kv_norm_g, mla_w_ukv, mla_q_head_g, mla_k_head_g, hy_conv_w, hy_conv_b, hy_w1, hy_b1, hy_w2, hy_b2, hy_w3, hy_b3, hy_d, w_out, ffn_norm_g, ffn_w_up, ffn_conv_w, ffn_conv_b, ffn_w_down):
    bsz, seq, d = x.shape
    n_ctx = ctx.shape[1]
    rope = axial_rope(seq)
    tm_lat, tm_ctx = 512, n_ctx
    for i in range(DEPTH):
        last = i == DEPTH - 1
        mod_lat = (jax.nn.silu(c) @ ada_w[i] + ada_b[i])[:, None, :]
        mod_ctx = jnp.broadcast_to((jax.nn.silu(c_ctx) @ ada_w[i] + ada_b[i])[None, None, :], (bsz, 1, 6 * d))
        sa_l, ca_l, ga_l, sf_l, cf_l, gf_l = jnp.split(mod_lat, 6, axis=-1)
        sa_c, ca_c, ga_c, sf_c, cf_c, gf_c = jnp.split(mod_ctx, 6, axis=-1)

        w_in_p = _pad_w_in(w_in[i])
        g_mix = mix_norm_g[i][None, :]
        qkv_l, z_l, ab_l, mla_l_in, hy_l_in = _in_proj(x, g_mix, sa_l, ca_l, w_in_p, tm_lat)
        qkv_c, z_c, ab_c, mla_c_in, hy_c_in = _in_proj(ctx, g_mix, sa_c, ca_c, w_in_p, tm_ctx)

        gdn_c, gdn_l = gdn_mixer((qkv_c, z_c, ab_c), (qkv_l, z_l, ab_l), gdn_conv_w[i], gdn_a_log[i],
                                 gdn_dt_bias[i], gdn_norm_g[i])
        mla_c, mla_l = mla_mixer(mla_c_in, mla_l_in, mla_q_norm_g[i], mla_w_uq[i],
                                 mla_kv_norm_g[i], mla_w_ukv[i], mla_q_head_g[i], mla_k_head_g[i],
                                 rope, not last)
        hy_args = (hy_conv_w[i], hy_conv_b[i], hy_w1[i], hy_b1[i], hy_w2[i], hy_b2[i], hy_w3[i], hy_b3[i], hy_d[i])
        hy_l = hyena_mixer(hy_l_in, *hy_args)

        w_out_b = w_out[i].astype(_BF16)
        w_up_b = ffn_w_up[i].astype(_BF16)
        w_dn_b = ffn_w_down[i].astype(_BF16)
        g_ffn = ffn_norm_g[i][None, :]
        cb = ffn_conv_b[i][None, :]

        x = _out_proj(x, gdn_l, mla_l, hy_l, ga_l, w_out_b, tm_lat)
        x = _ffn(x, g_ffn, sf_l, cf_l, gf_l, w_up_b, ffn_conv_w[i], cb, w_dn_b, tm_lat)

        if not last:
            hy_c = hyena_mixer(hy_c_in, *hy_args)
            ctx = _out_proj(ctx, gdn_c, mla_c, hy_c, ga_c, w_out_b, tm_ctx)
            ctx = _ffn(ctx, g_ffn, sf_c, cf_c, gf_c, w_up_b, ffn_conv_w[i], cb, w_dn_b, tm_ctx)
    return x
```

```python
import functools
import math

import jax
import jax.numpy as jnp
import numpy as np
from jax import lax
from jax.experimental import pallas as pl
from jax.experimental.pallas import tpu as pltpu

D_MODEL = 1024
DEPTH = 2
GRID_W = 64
EPS = 1e-6

GDN_HEADS = 6
GDN_DK = 64
GDN_DV = 64
GDN_CHUNK = 64

MLA_HEADS = 6
MLA_Q_RANK = 256
MLA_KV_RANK = 128
MLA_NOPE = 64
MLA_ROPE = 32
MLA_V = 64
MLA_QK = MLA_NOPE + MLA_ROPE
ATTN_BLOCK = 128
ROPE_BASE = 10000.0
ROPE_AXIS = MLA_ROPE // 2

HY_CH = 256
HY_BANDS = 16
HY_EMB = 1 + 2 * HY_BANDS
HY_HIDDEN = 64
HY_TARGET = 1e-2
HY_FAST_DECAY_PCT = 0.3
HY_SLOW_DECAY_PCT = 1.5
HY_MAX_DECAY = math.log(HY_TARGET) / HY_FAST_DECAY_PCT
HY_MIN_DECAY = math.log(HY_TARGET) / HY_SLOW_DECAY_PCT

D_FF = 2816

GDN_WIDTH = GDN_HEADS * GDN_DV
MLA_WIDTH = MLA_HEADS * MLA_V
HY_WIDTH = HY_CH
GDN_QKV = GDN_HEADS * (2 * GDN_DK + GDN_DV)
GDN_IN = GDN_QKV + GDN_WIDTH + 4 * GDN_HEADS
MLA_IN = MLA_Q_RANK + MLA_KV_RANK + MLA_ROPE
HY_IN = 3 * HY_CH

LANES = 128
SUBLANES = 8
SUBLANES_BF16 = 16
VMEM_LIMIT_BYTES = 56 * 1024 * 1024

TOK = 256
GDN_PAIRS = GDN_HEADS // 2
N_GATE = 4 * GDN_HEADS

AB_PAD = LANES
MLA_PAD = 512
IN_GROUPS = (GDN_QKV, GDN_WIDTH, AB_PAD, MLA_PAD, HY_IN)
IN_TOTAL = sum(IN_GROUPS)

FFN_CHUNK = 256
HALO = SUBLANES_BF16

_BF16 = jnp.bfloat16
_F32 = jnp.float32
_HI = lax.Precision.HIGHEST
_NT = (((1,), (1,)), ((), ()))
_TN = (((0,), (0,)), ((), ()))


def _cparams(n_axes, sem=None):
    return pltpu.CompilerParams(
        dimension_semantics=sem or ("parallel",) * n_axes, vmem_limit_bytes=VMEM_LIMIT_BYTES)


def _norm_mod(x, g, shift, scale):
    ms = jnp.mean(x * x, axis=-1, keepdims=True)
    y = x * lax.rsqrt(ms + EPS) * g
    return y * (1.0 + scale) + shift


def _silu(x):
    return x * jax.nn.sigmoid(x)


def _in_proj_kernel(x_ref, c_ref, g_ref, sl_ref, cl_ref, sc_ref, cc_ref, w_ref, *out_refs):
    is_ctx = pl.program_id(1) == 0
    x = jnp.where(is_ctx, c_ref[0], x_ref[0])
    shift = jnp.where(is_ctx, sc_ref[0], sl_ref[0])
    scale = jnp.where(is_ctx, cc_ref[0], cl_ref[0])
    h = _norm_mod(x, g_ref[...], shift, scale)
    p = jnp.dot(h.astype(_BF16), w_ref[...], preferred_element_type=_F32)
    off = 0
    for o_ref, n in zip(out_refs, IN_GROUPS):
        o_ref[0] = p[:, off:off + n]
        off += n


def _in_proj(x, ctx, g, shift_l, scale_l, shift_c, scale_c, w_pad):
    b, l, d = x.shape
    nt = 1 + l // TOK
    vec_l = pl.BlockSpec((1, 1, d), lambda i, j: (i, 0, 0))
    vec_c = pl.BlockSpec((1, 1, d), lambda i, j: (0, 0, 0))
    return pl.pallas_call(
        _in_proj_kernel,
        grid=(b, nt),
        in_specs=[
            pl.BlockSpec((1, TOK, d), lambda i, j: (i, jnp.maximum(j - 1, 0), 0)),
            pl.BlockSpec((1, TOK, d), lambda i, j: (i, 0, 0)),
            pl.BlockSpec((1, d), lambda i, j: (0, 0)),
            vec_l, vec_l, vec_c, vec_c,
            pl.BlockSpec((d, IN_TOTAL), lambda i, j: (0, 0)),
        ],
        out_specs=[pl.BlockSpec((1, TOK, n), lambda i, j: (i, j, 0)) for n in IN_GROUPS],
        out_shape=[jax.ShapeDtypeStruct((b, nt * TOK, n), _F32) for n in IN_GROUPS],
        compiler_params=_cparams(2),
        name="in_proj",
    )(x, ctx, g, shift_l, scale_l, shift_c, scale_c, w_pad)


def _pad_w_in(w_in):
    s1 = GDN_QKV + GDN_WIDTH
    s2 = GDN_IN
    s3 = GDN_IN + MLA_IN
    d = w_in.shape[0]
    z = lambda n: jnp.zeros((d, n), w_in.dtype)
    parts = [w_in[:, :s1], w_in[:, s1:s2], z(AB_PAD - N_GATE),
             w_in[:, s2:s3], z(MLA_PAD - MLA_IN), w_in[:, s3:]]
    return jnp.concatenate(parts, axis=1).astype(_BF16)


def _gdn_consts():
    r = np.arange(TOK)
    same = (r[:, None] // GDN_CHUNK) == (r[None, :] // GDN_CHUNK)
    tril = (same & (r[None, :] <= r[:, None])).astype(np.float32)
    triu = (same & (r[None, :] >= r[:, None])).astype(np.float32)
    c = np.arange(GDN_WIDTH)
    head_ones = (c[:, None] // GDN_DK == c[None, :] // GDN_DK).astype(np.float32)
    expand = np.zeros((LANES, 4 * GDN_WIDTH), np.float32)
    for k in range(4):
        for h in range(GDN_HEADS):
            expand[k * GDN_HEADS + h, k * GDN_WIDTH + h * GDN_DK:k * GDN_WIDTH + (h + 1) * GDN_DK] = 1.0
    return tril, triu, head_ones, expand


def _gdn_prep_kernel(xp_ref, x_ref, xn_ref, ab_ref, cw_ref, alog_ref, dt_ref, tril_ref, triu_ref, hones_ref,
                     exp_ref, q_ref, k_ref, v_ref, gf_ref, gb_ref, bf_ref, bb_ref, xe_ref):
    j = pl.program_id(1)
    nt = pl.num_programs(1)
    pv = (j >= 2).astype(_F32)
    nv = jnp.logical_and(j >= 1, j < nt - 1).astype(_F32)
    xe_ref[0:SUBLANES] = xp_ref[0] * pv
    xe_ref[SUBLANES:SUBLANES + TOK] = x_ref[0]
    xe_ref[SUBLANES + TOK:] = xn_ref[0] * nv
    cw = cw_ref[...]
    y = (xe_ref[SUBLANES - 1:SUBLANES - 1 + TOK] * cw[0:1] + x_ref[0] * cw[1:2]
         + xe_ref[SUBLANES + 1:SUBLANES + 1 + TOK] * cw[2:3])
    y = _silu(y)
    hk = GDN_HEADS * GDN_DK
    q, k, v = y[:, :hk], y[:, hk:2 * hk], y[:, 2 * hk:]
    hones = hones_ref[...]
    qs = jnp.dot(q * q, hones, precision=_HI, preferred_element_type=_F32)
    ks = jnp.dot(k * k, hones, precision=_HI, preferred_element_type=_F32)
    q_ref[0] = q * lax.rsqrt(qs + EPS) * (GDN_DK ** -0.5)
    k_ref[0] = k * lax.rsqrt(ks + EPS)
    v_ref[0] = v

    ab = ab_ref[0]
    lane = lax.broadcasted_iota(jnp.int32, ab.shape, 1)
    a_in = ab + dt_ref[...]
    softplus = jnp.maximum(a_in, 0.0) + jnp.log(1.0 + jnp.exp(-jnp.abs(a_in)))
    g = jnp.where(lane < 2 * GDN_HEADS, -jnp.exp(alog_ref[...]) * softplus, 0.0)
    gc_f = jnp.dot(tril_ref[...], g, precision=_HI, preferred_element_type=_F32)
    gc_b = jnp.dot(triu_ref[...], g, precision=_HI, preferred_element_type=_F32)
    cols = jnp.where(lane < GDN_HEADS, gc_f, jnp.where(lane < 2 * GDN_HEADS, gc_b, jax.nn.sigmoid(ab)))
    wide = jnp.dot(cols, exp_ref[...], precision=_HI, preferred_element_type=_F32)
    gf_ref[0] = wide[:, 0:GDN_WIDTH]
    gb_ref[0] = wide[:, GDN_WIDTH:2 * GDN_WIDTH]
    bf_ref[0] = wide[:, 2 * GDN_WIDTH:3 * GDN_WIDTH]
    bb_ref[0] = wide[:, 3 * GDN_WIDTH:]


def _gdn_prep(qkv, ab, conv_w, a_log_row, dt_row):
    b, lt, _ = qkv.shape
    nt = lt // TOK
    nh = TOK // SUBLANES
    last = lt // SUBLANES - 1
    consts = [jnp.asarray(a) for a in _gdn_consts()]
    whole = lambda a: pl.BlockSpec(a.shape, lambda i, j: (0,) * a.ndim)
    row = lambda n: pl.BlockSpec((1, TOK, n), lambda i, j: (i, j, 0))
    return pl.pallas_call(
        _gdn_prep_kernel,
        grid=(b, nt),
        in_specs=[
            pl.BlockSpec((1, SUBLANES, GDN_QKV), lambda i, j: (i, jnp.maximum(j * nh - 1, 0), 0)),
            row(GDN_QKV),
            pl.BlockSpec((1, SUBLANES, GDN_QKV), lambda i, j: (i, jnp.minimum((j + 1) * nh, last), 0)),
            row(AB_PAD), whole(conv_w), whole(a_log_row), whole(dt_row),
        ] + [whole(a) for a in consts],
        out_specs=[row(GDN_WIDTH)] * 7,
        out_shape=[jax.ShapeDtypeStruct((b, lt, GDN_WIDTH), _F32)] * 7,
        scratch_shapes=[pltpu.VMEM((TOK + 2 * SUBLANES, GDN_QKV), _F32)],
        compiler_params=_cparams(2),
        name="gdn_prep",
    )(qkv, qkv, qkv, ab, conv_w, a_log_row, dt_row, *consts)


def _block_diag(z, left):
    return jnp.concatenate([jnp.where(left, z, 0.0), jnp.where(left, 0.0, z)], axis=0).astype(_BF16)


def _mm(a, b):
    return jnp.dot(a.astype(_BF16), b, preferred_element_type=_F32)


def _gdn_chunk(q, k, v, gx, bx, s, backward, masks):
    left, eye2, incl, strict, ones64, diag_blocks = masks
    c = GDN_CHUNK
    yk = _block_diag(k, left)
    qk_kk = lax.dot_general(jnp.concatenate([q, k], axis=0).astype(_BF16), yk, _NT, preferred_element_type=_F32)
    qk, kk = qk_kk[:c], qk_kk[c:]
    d0 = jnp.where(eye2, gx, 0.0)
    t_hi = d0.astype(_BF16)
    r1 = d0 - t_hi.astype(_F32)
    t_mid = r1.astype(_BF16)
    t_lo = (r1 - t_mid.astype(_F32)).astype(_BF16)
    r3 = jnp.dot(ones64, jnp.concatenate([t_hi, t_mid, t_lo], axis=1), preferred_element_type=_F32)
    r = r3[:, :LANES] + r3[:, LANES:2 * LANES] + r3[:, 2 * LANES:]
    dec = jnp.where(incl, jnp.exp(jnp.where(incl, gx - r, 0.0)), 0.0)
    a = bx * kk * jnp.where(strict, dec, 0.0)
    qk = qk * dec
    eg = jnp.exp(gx)
    p = jnp.where(eye2, 1.0, 0.0) - a
    pw = a
    pw_bd = _block_diag(pw, left)
    for _ in range(5):
        pw = _mm(pw, pw_bd)
        pw_bd = _block_diag(pw, left)
        p = p + _mm(p, pw_bd)
    rhs = jnp.concatenate([_block_diag(bx * v, left), _block_diag(bx * k * eg, left)], axis=1)
    uw = _mm(p, rhs)
    u, w = uw[:, :LANES], uw[:, LANES:]
    qd = q * eg
    tot = gx[0:1] if backward else gx[c - 1:c]
    kd = k * jnp.exp(tot - gx)
    sb = s.astype(_BF16)
    ws_qs = _mm(jnp.concatenate([w, qd], axis=0), sb)
    delta = u - ws_qs[:c]
    o = ws_qs[c:] + _mm(qk, _block_diag(delta, left))
    upd = lax.dot_general(kd.astype(_BF16), delta.astype(_BF16), _TN, preferred_element_type=_F32)
    s = s * jnp.exp(tot) + jnp.where(diag_blocks, upd, 0.0)
    return o, s


def _gdn_scan_kernel(cq, ck, cv, cgf, cbf, cgb, cbb, fq, fk, fv, fg, fb, rq, rk, rv, rg, rb,
                     ocf_ref, ocb_ref, of_ref, ob_ref, sf_ref, sb_ref):
    step = pl.program_id(1)
    is_ctx = step == 0

    @pl.when(is_ctx)
    def _():
        sf_ref[...] = jnp.zeros_like(sf_ref)
        sb_ref[...] = jnp.zeros_like(sb_ref)

    c = GDN_CHUNK
    li = lax.broadcasted_iota(jnp.int32, (c, LANES), 1)
    ri = lax.broadcasted_iota(jnp.int32, (c, LANES), 0)
    lj = li & (c - 1)
    left = li < c
    eye2 = lj == ri
    ones64 = jnp.ones((c, c), _BF16)
    r2 = lax.broadcasted_iota(jnp.int32, (LANES, LANES), 0)
    c2 = lax.broadcasted_iota(jnp.int32, (LANES, LANES), 1)
    diag_blocks = (r2 // c) == (c2 // c)
    masks_f = (left, eye2, ri >= lj, ri > lj, ones64, diag_blocks)
    masks_b = (left, eye2, ri <= lj, ri < lj, ones64, diag_blocks)

    def run(refs, ctx_refs, s_ref, o_ref, oc_ref, backward, masks):
        vals = [jnp.where(is_ctx, cr[0], r[0]) for cr, r in zip(ctx_refs, refs)]
        order = range(TOK // c - 1, -1, -1) if backward else range(TOK // c)
        outs = {}
        for p in range(GDN_PAIRS):
            s = s_ref[p]
            for n in order:
                blk = [a[n * c:(n + 1) * c, p * LANES:(p + 1) * LANES] for a in vals]
                o, s = _gdn_chunk(*blk, s, backward, masks)
                outs[(n, p)] = o
            s_ref[p] = s
        o_all = jnp.concatenate(
            [jnp.concatenate([outs[(n, p)] for p in range(GDN_PAIRS)], axis=1) for n in range(TOK // c)], axis=0)

        @pl.when(is_ctx)
        def _():
            oc_ref[0] = o_all

        @pl.when(jnp.logical_not(is_ctx))
        def _():
            o_ref[0] = o_all

    run((fq, fk, fv, fg, fb), (cq, ck, cv, cgf, cbf), sf_ref, of_ref, ocf_ref, False, masks_f)
    run((rq, rk, rv, rg, rb), (cq, ck, cv, cgb, cbb), sb_ref, ob_ref, ocb_ref, True, masks_b)


def _gdn_intra(probs, masks):
    left, eye2, ones64, tri, same_blk = masks
    c = GDN_CHUNK
    qs, ks, vs, gxs, bxs, bws = zip(*probs)
    qk_kk = [lax.dot_general(jnp.concatenate([q, k], axis=0).astype(_BF16), _block_diag(k, left), _NT,
                             preferred_element_type=_F32) for q, k in zip(qs, ks)]
    rs = []
    for gx in gxs:
        d0 = jnp.where(eye2, gx, 0.0)
        t_hi = d0.astype(_BF16)
        r1 = d0 - t_hi.astype(_F32)
        t_mid = r1.astype(_BF16)
        t_lo = (r1 - t_mid.astype(_F32)).astype(_BF16)
        r3 = jnp.dot(ones64, jnp.concatenate([t_hi, t_mid, t_lo], axis=1), preferred_element_type=_F32)
        rs.append(r3[:, :LANES] + r3[:, LANES:2 * LANES] + r3[:, 2 * LANES:])
    a_s, qkm, egs = [], [], []
    for x, gx, bx, r, bw in zip(qk_kk, gxs, bxs, rs, bws):
        incl, strict = tri[bw]
        dec = jnp.where(incl, jnp.exp(jnp.where(incl, gx - r, 0.0)), 0.0)
        a_s.append(bx * x[c:] * jnp.where(strict, dec, 0.0))
        qkm.append(x[:c] * dec)
        egs.append(jnp.exp(gx))
    eye_f = jnp.where(eye2, 1.0, 0.0)
    base = same_blk[8]
    d1 = [jnp.where(base, a, 0.0) for a in a_s]
    ps = [eye_f - d for d in d1]
    d2 = [_mm(d, _block_diag(d, left)) for d in d1]
    d2_bd = [_block_diag(d, left) for d in d2]
    ps = [p + _mm(p, bd) for p, bd in zip(ps, d2_bd)]
    d4 = [_mm(d, bd) for d, bd in zip(d2, d2_bd)]
    ps = [p + _mm(p, _block_diag(d, left)) for p, d in zip(ps, d4)]
    for blk in (8, 16, 32):
        off = jnp.logical_and(same_blk[2 * blk], jnp.logical_not(same_blk[blk]))
        t1 = [_mm(p, _block_diag(jnp.where(off, a, 0.0), left)) for p, a in zip(ps, a_s)]
        ps = [p - _mm(t, _block_diag(p, left)) for p, t in zip(ps, t1)]
    out = []
    for p, q, k, v, gx, bx, eg, qk, bw in zip(ps, qs, ks, vs, gxs, bxs, egs, qkm, bws):
        tot = gx[0:1] if bw else gx[c - 1:c]
        lhs = jnp.concatenate([k * eg, q * eg], axis=0).astype(_BF16)
        out.append((p, lhs, bx, bx * v, qk, k * jnp.exp(tot - gx), tot))
    return out


def _gdn_state_step(chains, left, diag_blocks):
    c = GDN_CHUNK
    ys = [jnp.dot(x[1], s.astype(_BF16), preferred_element_type=_F32) for s, x in chains]
    resid = [x[3] - x[2] * y[:c] for (s, x), y in zip(chains, ys)]
    deltas = [_mm(x[0], _block_diag(r, left)) for (s, x), r in zip(chains, resid)]
    os_ = [y[c:] + _mm(x[4], _block_diag(d, left)) for (s, x), y, d in zip(chains, ys, deltas)]
    upds = [lax.dot_general(x[5].astype(_BF16), d.astype(_BF16), _TN, preferred_element_type=_F32)
            for (s, x), d in zip(chains, deltas)]
    new_s = [s * jnp.exp(x[6]) + jnp.where(diag_blocks, u, 0.0) for (s, x), u in zip(chains, upds)]
    return list(zip(os_, new_s))


def _gdn_scan_kernel_bf(cq, ck, cv, cgf, cbf, cgb, cbb, fq, fk, fv, fg, fb, rq, rk, rv, rg, rb,
                        ocf_ref, ocb_ref, of_ref, ob_ref, sf_ref, sb_ref):
    step = pl.program_id(1)
    is_ctx = step == 0

    @pl.when(is_ctx)
    def _():
        sf_ref[...] = jnp.zeros_like(sf_ref)
        sb_ref[...] = jnp.zeros_like(sb_ref)

    c = GDN_CHUNK
    nc = TOK // c
    li = lax.broadcasted_iota(jnp.int32, (c, LANES), 1)
    ri = lax.broadcasted_iota(jnp.int32, (c, LANES), 0)
    lj = li & (c - 1)
    left = li < c
    eye2 = lj == ri
    ones64 = jnp.ones((c, c), _BF16)
    r2 = lax.broadcasted_iota(jnp.int32, (LANES, LANES), 0)
    c2 = lax.broadcasted_iota(jnp.int32, (LANES, LANES), 1)
    diag_blocks = (r2 // c) == (c2 // c)
    tri = {False: (ri >= lj, ri > lj), True: (ri <= lj, ri < lj)}
    same_blk = {b: (ri // b) == (lj // b) for b in (8, 16, 32, 64)}
    masks = (left, eye2, ones64, tri, same_blk)

    fvals = [jnp.where(is_ctx, cr[0], r[0]) for cr, r in zip((cq, ck, cv, cgf, cbf), (fq, fk, fv, fg, fb))]
    bvals = [jnp.where(is_ctx, cr[0], r[0]) for cr, r in zip((cq, ck, cv, cgb, cbb), (rq, rk, rv, rg, rb))]
    keys, probs = [], []
    for bw, vals in ((False, fvals), (True, bvals)):
        for n in range(nc):
            for p in range(GDN_PAIRS):
                keys.append((bw, n, p))
                probs.append(tuple(a[n * c:(n + 1) * c, p * LANES:(p + 1) * LANES] for a in vals) + (bw,))
    intra = dict(zip(keys, _gdn_intra(probs, masks)))

    chain_keys = [(bw, p) for bw in (False, True) for p in range(GDN_PAIRS)]
    states = {(bw, p): (sb_ref if bw else sf_ref)[p] for bw, p in chain_keys}
    outs = {}
    for t in range(nc):
        ns = {(bw, p): (nc - 1 - t if bw else t) for bw, p in chain_keys}
        res = _gdn_state_step([(states[kk], intra[(kk[0], ns[kk], kk[1])]) for kk in chain_keys], left, diag_blocks)
        for kk, (o, s) in zip(chain_keys, res):
            outs[(kk[0], ns[kk], kk[1])] = o
            states[kk] = s
    for bw, p in chain_keys:
        (sb_ref if bw else sf_ref)[p] = states[(bw, p)]

    for bw, o_ref, oc_ref in ((False, of_ref, ocf_ref), (True, ob_ref, ocb_ref)):
        o_all = jnp.concatenate(
            [jnp.concatenate([outs[(bw, n, p)] for p in range(GDN_PAIRS)], axis=1) for n in range(nc)], axis=0)

        @pl.when(is_ctx)
        def _(o_all=o_all, oc_ref=oc_ref):
            oc_ref[0] = o_all

        @pl.when(jnp.logical_not(is_ctx))
        def _(o_all=o_all, o_ref=o_ref):
            o_ref[0] = o_all


def _gdn_scan(q, k, v, gf, gb, bf, bb):
    b, lt, w = q.shape
    nl = lt // TOK - 1
    ctx = pl.BlockSpec((1, TOK, w), lambda i, s: (i, 0, 0))
    fwd = pl.BlockSpec((1, TOK, w), lambda i, s: (i, jnp.maximum(s, 1), 0))
    bwd = pl.BlockSpec((1, TOK, w), lambda i, s: (i, nl + 1 - jnp.maximum(s, 1), 0))
    fwd_o = pl.BlockSpec((1, TOK, w), lambda i, s: (i, jnp.maximum(s, 1) - 1, 0))
    bwd_o = pl.BlockSpec((1, TOK, w), lambda i, s: (i, nl - jnp.maximum(s, 1), 0))
    return pl.pallas_call(
        _gdn_scan_kernel_bf,
        grid=(b, nl + 1),
        in_specs=[ctx] * 7 + [fwd] * 5 + [bwd] * 5,
        out_specs=[ctx, ctx, fwd_o, bwd_o],
        out_shape=[jax.ShapeDtypeStruct((b, TOK, w), _F32)] * 2 + [jax.ShapeDtypeStruct((b, nl * TOK, w), _F32)] * 2,
        scratch_shapes=[pltpu.VMEM((GDN_PAIRS, LANES, LANES), _F32)] * 2,
        compiler_params=_cparams(2, ("parallel", "arbitrary")),
        name="gdn_scan",
    )(q, k, v, gf, bf, gb, bb, q, k, v, gf, bf, q, k, v, gb, bb)


def _out_proj_kernel(x_ref, of_ref, ob_ref, z_ref, gn_ref, hones_ref, b_ref, c_ref, ga_ref, w_ref, o_ref):
    o = of_ref[0] + ob_ref[0]
    ms = jnp.dot(o * o, hones_ref[...], precision=_HI, preferred_element_type=_F32) * (1.0 / GDN_DV)
    gdn = o * lax.rsqrt(ms + EPS) * gn_ref[...] * _silu(z_ref[0])
    mix = jnp.concatenate([gdn, b_ref[0], c_ref[0]], axis=-1).astype(_BF16)
    y = jnp.dot(mix, w_ref[...], preferred_element_type=_F32)
    o_ref[0] = x_ref[0] + ga_ref[0] * y


def _out_proj(x, o_f, o_b, z, gn_row, mla, hy, ga, w_out, toff):
    b, l, d = x.shape
    hones = jnp.asarray(_gdn_consts()[2])
    row = lambda n: pl.BlockSpec((1, TOK, n), lambda i, j: (i, j, 0))
    rowc = lambda n: pl.BlockSpec((1, TOK, n), lambda i, j: (i, j + toff, 0))
    whole = lambda a: pl.BlockSpec(a.shape, lambda i, j: (0,) * a.ndim)
    return pl.pallas_call(
        _out_proj_kernel,
        grid=(b, l // TOK),
        in_specs=[row(d), row(GDN_WIDTH), row(GDN_WIDTH), rowc(GDN_WIDTH), whole(gn_row), whole(hones),
                  rowc(MLA_WIDTH), rowc(HY_WIDTH),
                  pl.BlockSpec((1, 1, d), lambda i, j: (i, 0, 0)),
                  pl.BlockSpec((d, d), lambda i, j: (0, 0))],
        out_specs=row(d),
        out_shape=jax.ShapeDtypeStruct((b, l, d), _F32),
        compiler_params=_cparams(2),
        name="out_proj",
    )(x, o_f, o_b, z, gn_row, hones, mla, hy, ga, w_out)


def _ffn_kernel(xp_ref, x_ref, xn_ref, g_ref, sf_ref, cf_ref, gf_ref, wup_ref, cw_ref, cb_ref, wdn_ref,
                o_ref, h_ref, up_ref, act_ref, *, tl):
    i = pl.program_id(1)
    nt = pl.num_programs(1)
    g, sf, cf = g_ref[...], sf_ref[0], cf_ref[0]
    pv = (i > 0).astype(_F32)
    nv = (i < nt - 1).astype(_F32)
    h_ref[0:HALO] = (_norm_mod(xp_ref[0], g, sf, cf) * pv).astype(_BF16)
    h_ref[HALO:HALO + tl] = _norm_mod(x_ref[0], g, sf, cf).astype(_BF16)
    h_ref[HALO + tl:] = (_norm_mod(xn_ref[0], g, sf, cf) * nv).astype(_BF16)

    def chunk(c, carry):
        lo = pl.multiple_of(c * FFN_CHUNK, FFN_CHUNK)
        wg = wup_ref[:, pl.ds(lo, FFN_CHUNK)]
        wv = wup_ref[:, pl.ds(D_FF + lo, FFN_CHUNK)]
        ug = jnp.dot(h_ref[...], wg, preferred_element_type=_F32)
        up_ref[...] = ug
        uv = jnp.dot(h_ref[HALO:HALO + tl], wv, preferred_element_type=_F32)
        cw = cw_ref[:, pl.ds(lo, FFN_CHUNK)]
        cb = cb_ref[:, pl.ds(lo, FFN_CHUNK)]
        gate = (up_ref[HALO - 1:HALO - 1 + tl] * cw[0:1] + ug[HALO:HALO + tl] * cw[1:2]
                + up_ref[HALO + 1:HALO + 1 + tl] * cw[2:3] + cb)
        act_ref[:, pl.ds(lo, FFN_CHUNK)] = (_silu(gate) * uv).astype(_BF16)
        return carry

    lax.fori_loop(0, D_FF // FFN_CHUNK, chunk, 0)
    y = jnp.dot(act_ref[...], wdn_ref[...], preferred_element_type=_F32)
    o_ref[0] = x_ref[0] + gf_ref[0] * y


def _ffn(x, g, sf, cf, gf, w_up, conv_w, conv_b, w_down, tl):
    b, l, d = x.shape
    nh = tl // HALO
    last = l // HALO - 1
    vec = pl.BlockSpec((1, 1, d), lambda i, j: (i, 0, 0))
    whole = lambda a: pl.BlockSpec(a.shape, lambda i, j: (0,) * a.ndim)
    return pl.pallas_call(
        functools.partial(_ffn_kernel, tl=tl),
        grid=(b, l // tl),
        in_specs=[
            pl.BlockSpec((1, HALO, d), lambda i, j: (i, jnp.maximum(j * nh - 1, 0), 0)),
            pl.BlockSpec((1, tl, d), lambda i, j: (i, j, 0)),
            pl.BlockSpec((1, HALO, d), lambda i, j: (i, jnp.minimum((j + 1) * nh, last), 0)),
            pl.BlockSpec((1, d), lambda i, j: (0, 0)),
            vec, vec, vec,
            whole(w_up), whole(conv_w), whole(conv_b), whole(w_down),
        ],
        out_specs=pl.BlockSpec((1, tl, d), lambda i, j: (i, j, 0)),
        out_shape=jax.ShapeDtypeStruct((b, l, d), _F32),
        scratch_shapes=[
            pltpu.VMEM((tl + 2 * HALO, d), _BF16),
            pltpu.VMEM((tl + 2 * HALO, FFN_CHUNK), _F32),
            pltpu.VMEM((tl, D_FF), _BF16),
        ],
        compiler_params=_cparams(2),
        name="conv_ffn",
    )(x, x, x, g, sf, cf, gf, w_up, conv_w, conv_b, w_down)


def rms_norm(x, g):
    xf = x.astype(jnp.float32)
    y = xf * lax.rsqrt(jnp.mean(xf * xf, axis=-1, keepdims=True) + EPS)
    return (y * g.astype(jnp.float32)).astype(x.dtype)


def dwconv3(x, w, b=None):
    xp = jnp.pad(x, ((0, 0), (1, 1), (0, 0)))
    y = xp[:, :-2] * w[0] + xp[:, 1:-1] * w[1] + xp[:, 2:] * w[2]
    return y if b is None else y + b


def axial_rope(l):
    rows = l // GRID_W
    row = jnp.repeat(jnp.arange(rows, dtype=jnp.float32), GRID_W)
    col = jnp.tile(jnp.arange(GRID_W, dtype=jnp.float32), rows)
    inv = ROPE_BASE ** (-jnp.arange(0, ROPE_AXIS, 2, dtype=jnp.float32) / ROPE_AXIS)
    ang = jnp.concatenate([row[:, None] * inv, col[:, None] * inv], axis=-1)
    return jnp.cos(ang), jnp.sin(ang)


def apply_rope(x, cos, sin):
    half = x.shape[-1] // 2
    x1, x2 = x[..., :half], x[..., half:]
    cos, sin = cos[None, :, None, :], sin[None, :, None, :]
    return jnp.concatenate([x1 * cos - x2 * sin, x1 * sin + x2 * cos], axis=-1)


def mla_heads(p, q_norm_g, w_uq, kv_norm_g, w_ukv, q_head_g, k_head_g, rope):
    b, l, _ = p.shape
    c_q = rms_norm(p[..., :MLA_Q_RANK], q_norm_g)
    c_kv = rms_norm(p[..., MLA_Q_RANK:MLA_Q_RANK + MLA_KV_RANK], kv_norm_g)
    k_rope = p[..., MLA_Q_RANK + MLA_KV_RANK:MLA_IN]
    q = (c_q @ w_uq).reshape(b, l, MLA_HEADS, MLA_QK)
    kv = (c_kv @ w_ukv).reshape(b, l, MLA_HEADS, MLA_NOPE + MLA_V)
    k = jnp.concatenate([kv[..., :MLA_NOPE],
                         jnp.broadcast_to(k_rope[:, :, None, :], (b, l, MLA_HEADS, MLA_ROPE))], axis=-1)
    v = kv[..., MLA_NOPE:]
    q = rms_norm(q, q_head_g)
    k = rms_norm(k, k_head_g)
    if rope is not None:
        cos, sin = rope
        q = jnp.concatenate([q[..., :MLA_NOPE], apply_rope(q[..., MLA_NOPE:], cos, sin)], axis=-1)
        k = jnp.concatenate([k[..., :MLA_NOPE], apply_rope(k[..., MLA_NOPE:], cos, sin)], axis=-1)
    return q, k, v


def softmax_attend(q, k, v, scale):
    s = jnp.einsum('bqhd,bkhd->bhqk', q, k).astype(jnp.float32) * scale
    p = jax.nn.softmax(s, axis=-1).astype(v.dtype)
    return jnp.einsum('bhqk,bkhd->bqhd', p, v)


def mla_mixer(p_ctx, p_lat, q_norm_g, w_uq, kv_norm_g, w_ukv, q_head_g, k_head_g, rope, with_ctx):
    qc, kc, vc = mla_heads(p_ctx, q_norm_g, w_uq, kv_norm_g, w_ukv, q_head_g, k_head_g, None)
    ql, kl, vl = mla_heads(p_lat, q_norm_g, w_uq, kv_norm_g, w_ukv, q_head_g, k_head_g, rope)
    scale = MLA_QK ** -0.5
    k_all = jnp.concatenate([kl, kc], axis=1)
    v_all = jnp.concatenate([vl, vc], axis=1)
    b, l = ql.shape[0], ql.shape[1]
    nb = l // ATTN_BLOCK
    q_blocks = jnp.moveaxis(ql.reshape(b, nb, ATTN_BLOCK, MLA_HEADS, MLA_QK), 1, 0)
    o_lat = lax.map(lambda qb: softmax_attend(qb, k_all, v_all, scale), q_blocks)
    o_lat = jnp.moveaxis(o_lat, 0, 1).reshape(b, l, MLA_WIDTH)
    if not with_ctx:
        return jnp.zeros((b, qc.shape[1], MLA_WIDTH), o_lat.dtype), o_lat
    o_ctx = softmax_attend(qc, kc, vc, scale).reshape(b, qc.shape[1], MLA_WIDTH)
    return o_ctx, o_lat


def hyena_filter(l, w1, b1, w2, b2, w3, b3):
    t = jnp.arange(l, dtype=jnp.float32)
    t_norm = t / max(l - 1, 1)
    bands = jnp.linspace(1e-4, HY_BANDS - 1, HY_BANDS, dtype=jnp.float32)
    ang = 2.0 * math.pi * t[:, None] * bands[None, :] / l
    z = jnp.concatenate([t_norm[:, None], jnp.cos(ang), jnp.sin(ang)], axis=-1)
    h = jnp.sin(z @ w1 + b1)
    h = jnp.sin(h @ w2 + b2)
    h = (h @ w3 + b3).reshape(l, 2, HY_CH).astype(jnp.float32)
    deltas = jnp.abs(jnp.linspace(HY_MIN_DECAY, HY_MAX_DECAY, HY_CH, dtype=jnp.float32))
    h = h * jnp.exp(-t_norm[:, None, None] * deltas)
    buf = jnp.concatenate([h[:, 0], jnp.zeros((1, HY_CH), jnp.float32), h[:0:-1, 1]], axis=0)
    return buf / jnp.sum(jnp.abs(buf), axis=0, keepdims=True)


def hyena_mixer(p, conv_w, conv_b, w1, b1, w2, b2, w3, b3, d_skip):
    b, l, _ = p.shape
    u = dwconv3(p, conv_w, conv_b)
    x0, x1, v = u[..., :HY_CH], u[..., HY_CH:2 * HY_CH], u[..., 2 * HY_CH:]
    z = (v * x1).astype(jnp.float32)
    buf = hyena_filter(l, w1, b1, w2, b2, w3, b3)
    zf = jnp.fft.rfft(z, n=2 * l, axis=1)
    hf = jnp.fft.rfft(buf, n=2 * l, axis=0)
    y = jnp.fft.irfft(zf * hf[None], n=2 * l, axis=1)[:, :l] + z * d_skip
    return x0 * y.astype(x0.dtype)


def _pad_row(v, n=LANES):
    v = v.reshape(1, -1)
    return jnp.pad(v, ((0, 0), (0, n - v.shape[1])))


def kernel(x, c, ctx, c_ctx, ada_w, ada_b, mix_norm_g, w_in, gdn_conv_w, gdn_a_log, gdn_dt_bias, gdn_norm_g, mla_q_norm_g, mla_w_uq, mla_kv_norm_g, mla_w_ukv, mla_q_head_g, mla_k_head_g, hy_conv_w, hy_conv_b, hy_w1, hy_b1, hy_w2, hy_b2, hy_w3, hy_b3, hy_d, w_out, ffn_norm_g, ffn_w_up, ffn_conv_w, ffn_conv_b, ffn_w_down):
    bsz, seq, d = x.shape
    n_ctx = ctx.shape[1]
    assert n_ctx == TOK and seq % TOK == 0
    rope = axial_rope(seq)
    for i in range(DEPTH):
        last = i == DEPTH - 1
        mod_lat = (jax.nn.silu(c) @ ada_w[i] + ada_b[i])[:, None, :]
        mod_ctx = (jax.nn.silu(c_ctx) @ ada_w[i] + ada_b[i])[None, None, :]
        sa_l, ca_l, ga_l, sf_l, cf_l, gf_l = jnp.split(mod_lat, 6, axis=-1)
        sa_c, ca_c, ga_c, sf_c, cf_c, gf_c = (jnp.broadcast_to(t, (bsz, 1, d)) for t in jnp.split(mod_ctx, 6, axis=-1))

        w_in_p = _pad_w_in(w_in[i])
        g_mix = mix_norm_g[i][None, :]
        qkv, z, ab, mla_in, hy_in = _in_proj(x, ctx, g_mix, sa_l, ca_l, sa_c[:1], ca_c[:1], w_in_p)

        q, k, v, gf, gb, bf, bb = _gdn_prep(qkv, ab, gdn_conv_w[i], _pad_row(gdn_a_log[i]), _pad_row(gdn_dt_bias[i]))
        oc_f, oc_b, ol_f, ol_b = _gdn_scan(q, k, v, gf, gb, bf, bb)

        mla_c, mla_l = mla_mixer(mla_in[:, :TOK], mla_in[:, TOK:], mla_q_norm_g[i], mla_w_uq[i],
                                 mla_kv_norm_g[i], mla_w_ukv[i], mla_q_head_g[i], mla_k_head_g[i],
                                 rope, not last)
        hy_args = (hy_conv_w[i], hy_conv_b[i], hy_w1[i], hy_b1[i], hy_w2[i], hy_b2[i], hy_w3[i], hy_b3[i], hy_d[i])
        hy_l = hyena_mixer(hy_in[:, TOK:], *hy_args)
        hy_c = hyena_mixer(hy_in[:, :TOK], *hy_args)
        mla_all = jnp.concatenate([mla_c, mla_l], axis=1)
        hy_all = jnp.concatenate([hy_c, hy_l], axis=1)

        w_out_b = w_out[i].astype(_BF16)
        w_up_b = ffn_w_up[i].astype(_BF16)
        w_dn_b = ffn_w_down[i].astype(_BF16)
        g_ffn = ffn_norm_g[i][None, :]
        cb = ffn_conv_b[i][None, :]
        gn_row = jnp.tile(gdn_norm_g[i], GDN_HEADS)[None, :]

        x = _out_proj(x, ol_f, ol_b, z, gn_row, mla_all, hy_all, ga_l, w_out_b, 1)
        x = _ffn(x, g_ffn, sf_l, cf_l, gf_l, w_up_b, ffn_conv_w[i], cb, w_dn_b, 512)

        if not last:
            ctx = _out_proj(ctx, oc_f, oc_b, z, gn_row, mla_all, hy_all, ga_c, w_out_b, 0)
            ctx = _ffn(ctx, g_ffn, sf_c, cf_c, gf_c, w_up_b, ffn_conv_w[i], cb, w_dn_b, TOK)
    return x
```

```python
import functools
import math

import jax
import jax.numpy as jnp
import numpy as np
from jax import lax
from jax.experimental import pallas as pl
from jax.experimental.pallas import tpu as pltpu

D_MODEL = 1024
DEPTH = 2
GRID_W = 64
EPS = 1e-6

GDN_HEADS = 6
GDN_DK = 64
GDN_DV = 64
GDN_CHUNK = 64

MLA_HEADS = 6
MLA_Q_RANK = 256
MLA_KV_RANK = 128
MLA_NOPE = 64
MLA_ROPE = 32
MLA_V = 64
MLA_QK = MLA_NOPE + MLA_ROPE
ATTN_BLOCK = 128
ROPE_BASE = 10000.0
ROPE_AXIS = MLA_ROPE // 2

HY_CH = 256
HY_BANDS = 16
HY_EMB = 1 + 2 * HY_BANDS
HY_HIDDEN = 64
HY_TARGET = 1e-2
HY_FAST_DECAY_PCT = 0.3
HY_SLOW_DECAY_PCT = 1.5
HY_MAX_DECAY = math.log(HY_TARGET) / HY_FAST_DECAY_PCT
HY_MIN_DECAY = math.log(HY_TARGET) / HY_SLOW_DECAY_PCT

D_FF = 2816

GDN_WIDTH = GDN_HEADS * GDN_DV
MLA_WIDTH = MLA_HEADS * MLA_V
HY_WIDTH = HY_CH
GDN_QKV = GDN_HEADS * (2 * GDN_DK + GDN_DV)
GDN_IN = GDN_QKV + GDN_WIDTH + 4 * GDN_HEADS
MLA_IN = MLA_Q_RANK + MLA_KV_RANK + MLA_ROPE
HY_IN = 3 * HY_CH

LANES = 128
SUBLANES = 8
SUBLANES_BF16 = 16
VMEM_LIMIT_BYTES = 56 * 1024 * 1024

TOK = 256
GDN_PAIRS = GDN_HEADS // 2
N_GATE = 4 * GDN_HEADS

AB_PAD = LANES
MLA_PAD = 512
IN_GROUPS = (GDN_QKV, GDN_WIDTH, AB_PAD, MLA_PAD, HY_IN)
IN_TOTAL = sum(IN_GROUPS)

FFN_CHUNK = 256
HALO = SUBLANES_BF16

_BF16 = jnp.bfloat16
_F32 = jnp.float32
_HI = lax.Precision.HIGHEST
_NT = (((1,), (1,)), ((), ()))
_TN = (((0,), (0,)), ((), ()))


def _cparams(n_axes, sem=None):
    return pltpu.CompilerParams(
        dimension_semantics=sem or ("parallel",) * n_axes, vmem_limit_bytes=VMEM_LIMIT_BYTES)


def _norm_mod(x, g, shift, scale):
    ms = jnp.mean(x * x, axis=-1, keepdims=True)
    y = x * lax.rsqrt(ms + EPS) * g
    return y * (1.0 + scale) + shift


def _silu(x):
    return x * jax.nn.sigmoid(x)


ADA_ROWS = 16
ADA_TN = 1024


def _ada_kernel(c_ref, w_ref, b_ref, o_ref):
    o_ref[...] = jnp.dot(_silu(c_ref[...]), w_ref[...], precision=_HI, preferred_element_type=_F32) + b_ref[...]


def _ada_mod(cond, w, b):
    d, n = w.shape
    return pl.pallas_call(
        _ada_kernel,
        grid=(n // ADA_TN,),
        in_specs=[pl.BlockSpec((ADA_ROWS, d), lambda j: (0, 0)),
                  pl.BlockSpec((d, ADA_TN), lambda j: (0, j)),
                  pl.BlockSpec((1, ADA_TN), lambda j: (0, j))],
        out_specs=pl.BlockSpec((ADA_ROWS, ADA_TN), lambda j: (0, j)),
        out_shape=jax.ShapeDtypeStruct((ADA_ROWS, n), _F32),
        compiler_params=_cparams(1),
        name="ada_mod",
    )(cond, w, b[None, :])


def _in_proj_kernel(x_ref, c_ref, g_ref, sl_ref, cl_ref, sc_ref, cc_ref, w_ref, *out_refs):
    is_ctx = pl.program_id(1) == 0
    x = jnp.where(is_ctx, c_ref[0], x_ref[0])
    shift = jnp.where(is_ctx, sc_ref[0], sl_ref[0])
    scale = jnp.where(is_ctx, cc_ref[0], cl_ref[0])
    h = _norm_mod(x, g_ref[...], shift, scale)
    p = jnp.dot(h.astype(_BF16), w_ref[...], preferred_element_type=_F32)
    off = 0
    for o_ref, n in zip(out_refs, IN_GROUPS):
        o_ref[0] = p[:, off:off + n]
        off += n


def _in_proj(x, ctx, g, shift_l, scale_l, shift_c, scale_c, w_pad):
    b, l, d = x.shape
    nt = 1 + l // TOK
    vec_l = pl.BlockSpec((1, 1, d), lambda i, j: (i, 0, 0))
    vec_c = pl.BlockSpec((1, 1, d), lambda i, j: (0, 0, 0))
    return pl.pallas_call(
        _in_proj_kernel,
        grid=(b, nt),
        in_specs=[
            pl.BlockSpec((1, TOK, d), lambda i, j: (i, jnp.maximum(j - 1, 0), 0)),
            pl.BlockSpec((1, TOK, d), lambda i, j: (i, 0, 0)),
            pl.BlockSpec((1, d), lambda i, j: (0, 0)),
            vec_l, vec_l, vec_c, vec_c,
            pl.BlockSpec((d, IN_TOTAL), lambda i, j: (0, 0)),
        ],
        out_specs=[pl.BlockSpec((1, TOK, n), lambda i, j: (i, j, 0)) for n in IN_GROUPS],
        out_shape=[jax.ShapeDtypeStruct((b, nt * TOK, n), _F32) for n in IN_GROUPS],
        compiler_params=_cparams(2),
        name="in_proj",
    )(x, ctx, g, shift_l, scale_l, shift_c, scale_c, w_pad)


def _pad_w_in(w_in):
    s1 = GDN_QKV + GDN_WIDTH
    s2 = GDN_IN
    s3 = GDN_IN + MLA_IN
    d = w_in.shape[0]
    z = lambda n: jnp.zeros((d, n), w_in.dtype)
    parts = [w_in[:, :s1], w_in[:, s1:s2], z(AB_PAD - N_GATE),
             w_in[:, s2:s3], z(MLA_PAD - MLA_IN), w_in[:, s3:]]
    return jnp.concatenate(parts, axis=1).astype(_BF16)


def _gdn_consts():
    r = np.arange(TOK)
    same = (r[:, None] // GDN_CHUNK) == (r[None, :] // GDN_CHUNK)
    tril = (same & (r[None, :] <= r[:, None])).astype(np.float32)
    triu = (same & (r[None, :] >= r[:, None])).astype(np.float32)
    c = np.arange(GDN_WIDTH)
    head_ones = (c[:, None] // GDN_DK == c[None, :] // GDN_DK).astype(np.float32)
    expand = np.zeros((LANES, 4 * GDN_WIDTH), np.float32)
    for k in range(4):
        for h in range(GDN_HEADS):
            expand[k * GDN_HEADS + h, k * GDN_WIDTH + h * GDN_DK:k * GDN_WIDTH + (h + 1) * GDN_DK] = 1.0
    return tril, triu, head_ones, expand


def _gdn_prep_kernel(xp_ref, x_ref, xn_ref, ab_ref, cw_ref, alog_ref, dt_ref, tril_ref, triu_ref, hones_ref,
                     exp_ref, q_ref, k_ref, v_ref, gf_ref, gb_ref, bf_ref, bb_ref, xe_ref):
    j = pl.program_id(1)
    nt = pl.num_programs(1)
    pv = (j >= 2).astype(_F32)
    nv = jnp.logical_and(j >= 1, j < nt - 1).astype(_F32)
    xe_ref[0:SUBLANES] = xp_ref[0] * pv
    xe_ref[SUBLANES:SUBLANES + TOK] = x_ref[0]
    xe_ref[SUBLANES + TOK:] = xn_ref[0] * nv
    cw = cw_ref[...]
    y = (xe_ref[SUBLANES - 1:SUBLANES - 1 + TOK] * cw[0:1] + x_ref[0] * cw[1:2]
         + xe_ref[SUBLANES + 1:SUBLANES + 1 + TOK] * cw[2:3])
    y = _silu(y)
    hk = GDN_HEADS * GDN_DK
    q, k, v = y[:, :hk], y[:, hk:2 * hk], y[:, 2 * hk:]
    hones = hones_ref[...]
    qs = jnp.dot(q * q, hones, precision=_HI, preferred_element_type=_F32)
    ks = jnp.dot(k * k, hones, precision=_HI, preferred_element_type=_F32)
    q_ref[0] = q * lax.rsqrt(qs + EPS) * (GDN_DK ** -0.5)
    k_ref[0] = k * lax.rsqrt(ks + EPS)
    v_ref[0] = v

    ab = ab_ref[0]
    lane = lax.broadcasted_iota(jnp.int32, ab.shape, 1)
    a_in = ab + dt_ref[...]
    softplus = jnp.maximum(a_in, 0.0) + jnp.log(1.0 + jnp.exp(-jnp.abs(a_in)))
    g = jnp.where(lane < 2 * GDN_HEADS, -jnp.exp(alog_ref[...]) * softplus, 0.0)
    gc_f = jnp.dot(tril_ref[...], g, precision=_HI, preferred_element_type=_F32)
    gc_b = jnp.dot(triu_ref[...], g, precision=_HI, preferred_element_type=_F32)
    cols = jnp.where(lane < GDN_HEADS, gc_f, jnp.where(lane < 2 * GDN_HEADS, gc_b, jax.nn.sigmoid(ab)))
    wide = jnp.dot(cols, exp_ref[...], precision=_HI, preferred_element_type=_F32)
    gf_ref[0] = wide[:, 0:GDN_WIDTH]
    gb_ref[0] = wide[:, GDN_WIDTH:2 * GDN_WIDTH]
    bf_ref[0] = wide[:, 2 * GDN_WIDTH:3 * GDN_WIDTH]
    bb_ref[0] = wide[:, 3 * GDN_WIDTH:]


def _gdn_prep(qkv, ab, conv_w, a_log_row, dt_row):
    b, lt, _ = qkv.shape
    nt = lt // TOK
    nh = TOK // SUBLANES
    last = lt // SUBLANES - 1
    consts = [jnp.asarray(a) for a in _gdn_consts()]
    whole = lambda a: pl.BlockSpec(a.shape, lambda i, j: (0,) * a.ndim)
    row = lambda n: pl.BlockSpec((1, TOK, n), lambda i, j: (i, j, 0))
    return pl.pallas_call(
        _gdn_prep_kernel,
        grid=(b, nt),
        in_specs=[
            pl.BlockSpec((1, SUBLANES, GDN_QKV), lambda i, j: (i, jnp.maximum(j * nh - 1, 0), 0)),
            row(GDN_QKV),
            pl.BlockSpec((1, SUBLANES, GDN_QKV), lambda i, j: (i, jnp.minimum((j + 1) * nh, last), 0)),
            row(AB_PAD), whole(conv_w), whole(a_log_row), whole(dt_row),
        ] + [whole(a) for a in consts],
        out_specs=[row(GDN_WIDTH)] * 7,
        out_shape=[jax.ShapeDtypeStruct((b, lt, GDN_WIDTH), _F32)] * 7,
        scratch_shapes=[pltpu.VMEM((TOK + 2 * SUBLANES, GDN_QKV), _F32)],
        compiler_params=_cparams(2),
        name="gdn_prep",
    )(qkv, qkv, qkv, ab, conv_w, a_log_row, dt_row, *consts)


def _block_diag(z, left):
    return jnp.concatenate([jnp.where(left, z, 0.0), jnp.where(left, 0.0, z)], axis=0).astype(_BF16)


def _mm(a, b):
    return jnp.dot(a.astype(_BF16), b, preferred_element_type=_F32)


def _gdn_chunk(q, k, v, gx, bx, s, backward, masks):
    left, eye2, incl, strict, ones64, diag_blocks = masks
    c = GDN_CHUNK
    yk = _block_diag(k, left)
    qk_kk = lax.dot_general(jnp.concatenate([q, k], axis=0).astype(_BF16), yk, _NT, preferred_element_type=_F32)
    qk, kk = qk_kk[:c], qk_kk[c:]
    d0 = jnp.where(eye2, gx, 0.0)
    t_hi = d0.astype(_BF16)
    r1 = d0 - t_hi.astype(_F32)
    t_mid = r1.astype(_BF16)
    t_lo = (r1 - t_mid.astype(_F32)).astype(_BF16)
    r3 = jnp.dot(ones64, jnp.concatenate([t_hi, t_mid, t_lo], axis=1), preferred_element_type=_F32)
    r = r3[:, :LANES] + r3[:, LANES:2 * LANES] + r3[:, 2 * LANES:]
    dec = jnp.where(incl, jnp.exp(jnp.where(incl, gx - r, 0.0)), 0.0)
    a = bx * kk * jnp.where(strict, dec, 0.0)
    qk = qk * dec
    eg = jnp.exp(gx)
    p = jnp.where(eye2, 1.0, 0.0) - a
    pw = a
    pw_bd = _block_diag(pw, left)
    for _ in range(5):
        pw = _mm(pw, pw_bd)
        pw_bd = _block_diag(pw, left)
        p = p + _mm(p, pw_bd)
    rhs = jnp.concatenate([_block_diag(bx * v, left), _block_diag(bx * k * eg, left)], axis=1)
    uw = _mm(p, rhs)
    u, w = uw[:, :LANES], uw[:, LANES:]
    qd = q * eg
    tot = gx[0:1] if backward else gx[c - 1:c]
    kd = k * jnp.exp(tot - gx)
    sb = s.astype(_BF16)
    ws_qs = _mm(jnp.concatenate([w, qd], axis=0), sb)
    delta = u - ws_qs[:c]
    o = ws_qs[c:] + _mm(qk, _block_diag(delta, left))
    upd = lax.dot_general(kd.astype(_BF16), delta.astype(_BF16), _TN, preferred_element_type=_F32)
    s = s * jnp.exp(tot) + jnp.where(diag_blocks, upd, 0.0)
    return o, s


def _gdn_scan_kernel(cq, ck, cv, cgf, cbf, cgb, cbb, fq, fk, fv, fg, fb, rq, rk, rv, rg, rb,
                     ocf_ref, ocb_ref, of_ref, ob_ref, sf_ref, sb_ref):
    step = pl.program_id(1)
    is_ctx = step == 0

    @pl.when(is_ctx)
    def _():
        sf_ref[...] = jnp.zeros_like(sf_ref)
        sb_ref[...] = jnp.zeros_like(sb_ref)

    c = GDN_CHUNK
    li = lax.broadcasted_iota(jnp.int32, (c, LANES), 1)
    ri = lax.broadcasted_iota(jnp.int32, (c, LANES), 0)
    lj = li & (c - 1)
    left = li < c
    eye2 = lj == ri
    ones64 = jnp.ones((c, c), _BF16)
    r2 = lax.broadcasted_iota(jnp.int32, (LANES, LANES), 0)
    c2 = lax.broadcasted_iota(jnp.int32, (LANES, LANES), 1)
    diag_blocks = (r2 // c) == (c2 // c)
    masks_f = (left, eye2, ri >= lj, ri > lj, ones64, diag_blocks)
    masks_b = (left, eye2, ri <= lj, ri < lj, ones64, diag_blocks)

    def run(refs, ctx_refs, s_ref, o_ref, oc_ref, backward, masks):
        vals = [jnp.where(is_ctx, cr[0], r[0]) for cr, r in zip(ctx_refs, refs)]
        order = range(TOK // c - 1, -1, -1) if backward else range(TOK // c)
        outs = {}
        for p in range(GDN_PAIRS):
            s = s_ref[p]
            for n in order:
                blk = [a[n * c:(n + 1) * c, p * LANES:(p + 1) * LANES] for a in vals]
                o, s = _gdn_chunk(*blk, s, backward, masks)
                outs[(n, p)] = o
            s_ref[p] = s
        o_all = jnp.concatenate(
            [jnp.concatenate([outs[(n, p)] for p in range(GDN_PAIRS)], axis=1) for n in range(TOK // c)], axis=0)

        @pl.when(is_ctx)
        def _():
            oc_ref[0] = o_all

        @pl.when(jnp.logical_not(is_ctx))
        def _():
            o_ref[0] = o_all

    run((fq, fk, fv, fg, fb), (cq, ck, cv, cgf, cbf), sf_ref, of_ref, ocf_ref, False, masks_f)
    run((rq, rk, rv, rg, rb), (cq, ck, cv, cgb, cbb), sb_ref, ob_ref, ocb_ref, True, masks_b)


def _gdn_intra(probs, masks):
    left, eye2, ones64, tri, same_blk = masks
    c = GDN_CHUNK
    qs, ks, vs, gxs, bxs, bws = zip(*probs)
    qk_kk = [lax.dot_general(jnp.concatenate([q, k], axis=0).astype(_BF16), _block_diag(k, left), _NT,
                             preferred_element_type=_F32) for q, k in zip(qs, ks)]
    rs = []
    for gx in gxs:
        d0 = jnp.where(eye2, gx, 0.0)
        t_hi = d0.astype(_BF16)
        r1 = d0 - t_hi.astype(_F32)
        t_mid = r1.astype(_BF16)
        t_lo = (r1 - t_mid.astype(_F32)).astype(_BF16)
        r3 = jnp.dot(ones64, jnp.concatenate([t_hi, t_mid, t_lo], axis=1), preferred_element_type=_F32)
        rs.append(r3[:, :LANES] + r3[:, LANES:2 * LANES] + r3[:, 2 * LANES:])
    a_s, qkm, egs = [], [], []
    for x, gx, bx, r, bw in zip(qk_kk, gxs, bxs, rs, bws):
        incl, strict = tri[bw]
        dec = jnp.where(incl, jnp.exp(jnp.where(incl, gx - r, 0.0)), 0.0)
        a_s.append(bx * x[c:] * jnp.where(strict, dec, 0.0))
        qkm.append(x[:c] * dec)
        egs.append(jnp.exp(gx))
    eye_f = jnp.where(eye2, 1.0, 0.0)
    base = same_blk[8]
    d1 = [jnp.where(base, a, 0.0) for a in a_s]
    ps = [eye_f - d for d in d1]
    d2 = [_mm(d, _block_diag(d, left)) for d in d1]
    d2_bd = [_block_diag(d, left) for d in d2]
    ps = [p + _mm(p, bd) for p, bd in zip(ps, d2_bd)]
    d4 = [_mm(d, bd) for d, bd in zip(d2, d2_bd)]
    ps = [p + _mm(p, _block_diag(d, left)) for p, d in zip(ps, d4)]
    for blk in (8, 16, 32):
        off = jnp.logical_and(same_blk[2 * blk], jnp.logical_not(same_blk[blk]))
        t1 = [_mm(p, _block_diag(jnp.where(off, a, 0.0), left)) for p, a in zip(ps, a_s)]
        ps = [p - _mm(t, _block_diag(p, left)) for p, t in zip(ps, t1)]
    out = []
    for p, q, k, v, gx, bx, eg, qk, bw in zip(ps, qs, ks, vs, gxs, bxs, egs, qkm, bws):
        tot = gx[0:1] if bw else gx[c - 1:c]
        lhs = jnp.concatenate([k * eg, q * eg], axis=0).astype(_BF16)
        out.append((p, lhs, bx, bx * v, qk, k * jnp.exp(tot - gx), tot))
    return out


def _gdn_state_step(chains, left, diag_blocks):
    c = GDN_CHUNK
    ys = [jnp.dot(x[1], s.astype(_BF16), preferred_element_type=_F32) for s, x in chains]
    resid = [x[3] - x[2] * y[:c] for (s, x), y in zip(chains, ys)]
    deltas = [_mm(x[0], _block_diag(r, left)) for (s, x), r in zip(chains, resid)]
    os_ = [y[c:] + _mm(x[4], _block_diag(d, left)) for (s, x), y, d in zip(chains, ys, deltas)]
    upds = [lax.dot_general(x[5].astype(_BF16), d.astype(_BF16), _TN, preferred_element_type=_F32)
            for (s, x), d in zip(chains, deltas)]
    new_s = [s * jnp.exp(x[6]) + jnp.where(diag_blocks, u, 0.0) for (s, x), u in zip(chains, upds)]
    return list(zip(os_, new_s))


def _gdn_scan_kernel_bf(cq, ck, cv, cgf, cbf, cgb, cbb, fq, fk, fv, fg, fb, rq, rk, rv, rg, rb,
                        ocf_ref, ocb_ref, of_ref, ob_ref, sf_ref, sb_ref):
    step = pl.program_id(1)
    is_ctx = step == 0

    @pl.when(is_ctx)
    def _():
        sf_ref[...] = jnp.zeros_like(sf_ref)
        sb_ref[...] = jnp.zeros_like(sb_ref)

    c = GDN_CHUNK
    nc = TOK // c
    li = lax.broadcasted_iota(jnp.int32, (c, LANES), 1)
    ri = lax.broadcasted_iota(jnp.int32, (c, LANES), 0)
    lj = li & (c - 1)
    left = li < c
    eye2 = lj == ri
    ones64 = jnp.ones((c, c), _BF16)
    r2 = lax.broadcasted_iota(jnp.int32, (LANES, LANES), 0)
    c2 = lax.broadcasted_iota(jnp.int32, (LANES, LANES), 1)
    diag_blocks = (r2 // c) == (c2 // c)
    tri = {False: (ri >= lj, ri > lj), True: (ri <= lj, ri < lj)}
    same_blk = {b: (ri // b) == (lj // b) for b in (8, 16, 32, 64)}
    masks = (left, eye2, ones64, tri, same_blk)

    fvals = [jnp.where(is_ctx, cr[0], r[0]) for cr, r in zip((cq, ck, cv, cgf, cbf), (fq, fk, fv, fg, fb))]
    bvals = [jnp.where(is_ctx, cr[0], r[0]) for cr, r in zip((cq, ck, cv, cgb, cbb), (rq, rk, rv, rg, rb))]
    keys, probs = [], []
    for bw, vals in ((False, fvals), (True, bvals)):
        for n in range(nc):
            for p in range(GDN_PAIRS):
                keys.append((bw, n, p))
                probs.append(tuple(a[n * c:(n + 1) * c, p * LANES:(p + 1) * LANES] for a in vals) + (bw,))
    intra = dict(zip(keys, _gdn_intra(probs, masks)))

    chain_keys = [(bw, p) for bw in (False, True) for p in range(GDN_PAIRS)]
    states = {(bw, p): (sb_ref if bw else sf_ref)[p] for bw, p in chain_keys}
    outs = {}
    for t in range(nc):
        ns = {(bw, p): (nc - 1 - t if bw else t) for bw, p in chain_keys}
        res = _gdn_state_step([(states[kk], intra[(kk[0], ns[kk], kk[1])]) for kk in chain_keys], left, diag_blocks)
        for kk, (o, s) in zip(chain_keys, res):
            outs[(kk[0], ns[kk], kk[1])] = o
            states[kk] = s
    for bw, p in chain_keys:
        (sb_ref if bw else sf_ref)[p] = states[(bw, p)]

    for bw, o_ref, oc_ref in ((False, of_ref, ocf_ref), (True, ob_ref, ocb_ref)):
        o_all = jnp.concatenate(
            [jnp.concatenate([outs[(bw, n, p)] for p in range(GDN_PAIRS)], axis=1) for n in range(nc)], axis=0)

        @pl.when(is_ctx)
        def _(o_all=o_all, oc_ref=oc_ref):
            oc_ref[0] = o_all

        @pl.when(jnp.logical_not(is_ctx))
        def _(o_all=o_all, o_ref=o_ref):
            o_ref[0] = o_all


def _gdn_scan(q, k, v, gf, gb, bf, bb):
    b, lt, w = q.shape
    nl = lt // TOK - 1
    ctx = pl.BlockSpec((1, TOK, w), lambda i, s: (i, 0, 0))
    fwd = pl.BlockSpec((1, TOK, w), lambda i, s: (i, jnp.maximum(s, 1), 0))
    bwd = pl.BlockSpec((1, TOK, w), lambda i, s: (i, nl + 1 - jnp.maximum(s, 1), 0))
    fwd_o = pl.BlockSpec((1, TOK, w), lambda i, s: (i, jnp.maximum(s, 1) - 1, 0))
    bwd_o = pl.BlockSpec((1, TOK, w), lambda i, s: (i, nl - jnp.maximum(s, 1), 0))
    return pl.pallas_call(
        _gdn_scan_kernel_bf,
        grid=(b, nl + 1),
        in_specs=[ctx] * 7 + [fwd] * 5 + [bwd] * 5,
        out_specs=[ctx, ctx, fwd_o, bwd_o],
        out_shape=[jax.ShapeDtypeStruct((b, TOK, w), _F32)] * 2 + [jax.ShapeDtypeStruct((b, nl * TOK, w), _F32)] * 2,
        scratch_shapes=[pltpu.VMEM((GDN_PAIRS, LANES, LANES), _F32)] * 2,
        compiler_params=_cparams(2, ("parallel", "arbitrary")),
        name="gdn_scan",
    )(q, k, v, gf, bf, gb, bb, q, k, v, gf, bf, q, k, v, gb, bb)


MLA_HEAD_PAD = LANES
MLA_WIDE = MLA_HEADS * MLA_HEAD_PAD
ATTN_TQ = 512


def _rope_tables(n_ctx, seq):
    rows = seq // GRID_W
    row = np.repeat(np.arange(rows, dtype=np.float64), GRID_W)
    col = np.tile(np.arange(GRID_W, dtype=np.float64), rows)
    inv = ROPE_BASE ** (-np.arange(0, ROPE_AXIS, 2, dtype=np.float64) / ROPE_AXIS)
    ang = np.concatenate([row[:, None] * inv, col[:, None] * inv], axis=-1)
    cos, sin = np.cos(ang), np.sin(ang)
    half = MLA_ROPE // 2
    c = np.ones((n_ctx + seq, MLA_HEAD_PAD))
    s = np.zeros((n_ctx + seq, MLA_HEAD_PAD))
    c[n_ctx:, MLA_NOPE:MLA_NOPE + half] = cos
    c[n_ctx:, MLA_NOPE + half:MLA_QK] = cos
    s[n_ctx:, MLA_NOPE:MLA_NOPE + half] = -sin
    s[n_ctx:, MLA_NOPE + half:MLA_QK] = sin
    return c.astype(np.float32), s.astype(np.float32)


def _mla_prep_kernel(p_ref, qn_ref, kvn_ref, wq_ref, wk_ref, wv_ref, sel_ref, qg_ref, kg_ref, cos_ref, sin_ref,
                     qc_ref, ql_ref, k_ref, v_ref):
    p = p_ref[0]
    cq = p[:, :MLA_Q_RANK]
    ckv = p[:, MLA_Q_RANK:MLA_Q_RANK + MLA_KV_RANK]
    kr = p[:, MLA_Q_RANK + MLA_KV_RANK:]
    cq = (cq * lax.rsqrt(jnp.mean(cq * cq, axis=-1, keepdims=True) + EPS) * qn_ref[...]).astype(_BF16)
    ckv = (ckv * lax.rsqrt(jnp.mean(ckv * ckv, axis=-1, keepdims=True) + EPS) * kvn_ref[...]).astype(_BF16)
    q = jnp.dot(cq, wq_ref[...], preferred_element_type=_F32)
    k = (jnp.dot(ckv, wk_ref[...], preferred_element_type=_F32)
         + jnp.dot(kr, sel_ref[...], precision=_HI, preferred_element_type=_F32))
    lane = lax.broadcasted_iota(jnp.int32, (TOK, MLA_WIDE), 1) & (MLA_HEAD_PAD - 1)
    v = jnp.dot(ckv, wv_ref[...], preferred_element_type=_F32) + jnp.where(lane == MLA_V, 1.0, 0.0)
    cos = jnp.concatenate([cos_ref[...]] * MLA_HEADS, axis=1)
    sin = jnp.concatenate([sin_ref[...]] * MLA_HEADS, axis=1)
    half = MLA_ROPE // 2
    first = jnp.logical_and(lane >= MLA_NOPE, lane < MLA_NOPE + half)
    second = jnp.logical_and(lane >= MLA_NOPE + half, lane < MLA_QK)

    def head_norm_rope(x, g):
        parts = []
        for h in range(MLA_HEADS):
            xh = x[:, h * MLA_HEAD_PAD:(h + 1) * MLA_HEAD_PAD]
            ms = jnp.sum(xh * xh, axis=-1, keepdims=True) * (1.0 / MLA_QK)
            parts.append(xh * lax.rsqrt(ms + EPS))
        xn = jnp.concatenate(parts, axis=1) * g
        up = pltpu.roll(xn, half, 1)
        down = pltpu.roll(xn, MLA_WIDE - half, 1)
        swapped = jnp.where(first, down, jnp.where(second, up, 0.0))
        return xn * cos + swapped * sin

    qf = head_norm_rope(q, qg_ref[...]) * (MLA_QK ** -0.5)
    kf = head_norm_rope(k, kg_ref[...])
    is_ctx = pl.program_id(1) == 0
    for h in range(MLA_HEADS):
        sl = slice(h * MLA_HEAD_PAD, (h + 1) * MLA_HEAD_PAD)
        k_ref[0, h] = kf[:, sl].astype(_BF16)
        v_ref[0, h] = v[:, sl].astype(_BF16)

    @pl.when(is_ctx)
    def _():
        for h in range(MLA_HEADS):
            qc_ref[0, h] = qf[:, h * MLA_HEAD_PAD:(h + 1) * MLA_HEAD_PAD].astype(_BF16)

    @pl.when(jnp.logical_not(is_ctx))
    def _():
        for h in range(MLA_HEADS):
            ql_ref[0, h] = qf[:, h * MLA_HEAD_PAD:(h + 1) * MLA_HEAD_PAD].astype(_BF16)


def _mla_weights(w_uq, w_ukv, q_head_g, k_head_g):
    pad = MLA_HEAD_PAD
    wq = jnp.pad(w_uq.reshape(MLA_Q_RANK, MLA_HEADS, MLA_QK), ((0, 0), (0, 0), (0, pad - MLA_QK)))
    wkv = w_ukv.reshape(MLA_KV_RANK, MLA_HEADS, MLA_NOPE + MLA_V)
    wk = jnp.pad(wkv[:, :, :MLA_NOPE], ((0, 0), (0, 0), (0, pad - MLA_NOPE)))
    wv = jnp.pad(wkv[:, :, MLA_NOPE:], ((0, 0), (0, 0), (0, pad - MLA_V)))
    sel = np.zeros((MLA_PAD - MLA_Q_RANK - MLA_KV_RANK, MLA_WIDE), np.float32)
    for h in range(MLA_HEADS):
        for r in range(MLA_ROPE):
            sel[r, h * pad + MLA_NOPE + r] = 1.0
    tile_g = lambda g: jnp.tile(jnp.pad(g, (0, pad - MLA_QK)), MLA_HEADS)[None, :]
    flat = lambda w: w.reshape(w.shape[0], MLA_WIDE).astype(_BF16)
    return flat(wq), flat(wk), flat(wv), jnp.asarray(sel), tile_g(q_head_g), tile_g(k_head_g)


def _mla_prep(mla_in, q_norm_g, kv_norm_g, weights, seq):
    b, lt, _ = mla_in.shape
    nt = lt // TOK
    wq, wk, wv, sel, qg, kg = weights
    cos, sin = (jnp.asarray(t) for t in _rope_tables(lt - seq, seq))
    whole = lambda a: pl.BlockSpec(a.shape, lambda i, j: (0,) * a.ndim)
    tab = pl.BlockSpec((TOK, MLA_HEAD_PAD), lambda i, j: (j, 0))
    hd = lambda f: pl.BlockSpec((1, MLA_HEADS, TOK, MLA_HEAD_PAD), f)
    qn, kvn = q_norm_g[None, :], kv_norm_g[None, :]
    shp = lambda t: jax.ShapeDtypeStruct((b, MLA_HEADS, t, MLA_HEAD_PAD), _BF16)
    return pl.pallas_call(
        _mla_prep_kernel,
        grid=(b, nt),
        in_specs=[pl.BlockSpec((1, TOK, MLA_PAD), lambda i, j: (i, j, 0)), whole(qn), whole(kvn),
                  whole(wq), whole(wk), whole(wv), whole(sel), whole(qg), whole(kg), tab, tab],
        out_specs=[hd(lambda i, j: (i, 0, 0, 0)), hd(lambda i, j: (i, 0, jnp.maximum(j, 1) - 1, 0)),
                   hd(lambda i, j: (i, 0, j, 0)), hd(lambda i, j: (i, 0, j, 0))],
        out_shape=[shp(TOK), shp(lt - TOK), shp(lt), shp(lt)],
        compiler_params=_cparams(2, ("parallel", "arbitrary")),
        name="mla_prep",
    )(mla_in, qn, kvn, wq, wk, wv, sel, qg, kg, cos, sin)


def _attn_kernel(q_ref, k_ref, v_ref, o_ref):
    outs = []
    for h in range(2):
        s = lax.dot_general(q_ref[0, h], k_ref[0, h], _NT, preferred_element_type=_F32)
        m = jnp.max(s, axis=-1, keepdims=True)
        p = jnp.exp(s - m).astype(_BF16)
        o = jnp.dot(p, v_ref[0, h], preferred_element_type=_F32)
        outs.append(o[:, :MLA_V] / o[:, MLA_V:MLA_V + 1])
    o_ref[0] = jnp.concatenate(outs, axis=1)


def _attention(q, k, v, n_keys, tq):
    b, h, t, w = q.shape
    return pl.pallas_call(
        _attn_kernel,
        grid=(b, h // 2, t // tq),
        in_specs=[pl.BlockSpec((1, 2, tq, w), lambda i, j, l: (i, j, l, 0)),
                  pl.BlockSpec((1, 2, n_keys, w), lambda i, j, l: (i, j, 0, 0)),
                  pl.BlockSpec((1, 2, n_keys, w), lambda i, j, l: (i, j, 0, 0))],
        out_specs=pl.BlockSpec((1, tq, 2 * MLA_V), lambda i, j, l: (i, l, j)),
        out_shape=jax.ShapeDtypeStruct((b, t, h * MLA_V), _F32),
        compiler_params=_cparams(3),
        name="mla_attention",
    )(q, k, v)


HY_BLK = 256
HY_LO = 128


def _hy_prep_kernel(xp_ref, x_ref, xn_ref, cw_ref, cb_ref, x0_ref, z_ref, xe_ref):
    j = pl.program_id(1)
    nt = pl.num_programs(1)
    pv = (j >= 2).astype(_F32)
    nv = jnp.logical_and(j >= 1, j < nt - 1).astype(_F32)
    xe_ref[0:SUBLANES] = xp_ref[0] * pv
    xe_ref[SUBLANES:SUBLANES + TOK] = x_ref[0]
    xe_ref[SUBLANES + TOK:] = xn_ref[0] * nv
    cw = cw_ref[...]
    u = (xe_ref[SUBLANES - 1:SUBLANES - 1 + TOK] * cw[0:1] + x_ref[0] * cw[1:2]
         + xe_ref[SUBLANES + 1:SUBLANES + 1 + TOK] * cw[2:3] + cb_ref[...])
    x0_ref[0] = u[:, :HY_CH]
    z_ref[0] = u[:, 2 * HY_CH:] * u[:, HY_CH:2 * HY_CH]


def _hy_prep(hy_in, conv_w, conv_b):
    b, lt, _ = hy_in.shape
    nh = TOK // SUBLANES
    last = lt // SUBLANES - 1
    whole = lambda a: pl.BlockSpec(a.shape, lambda i, j: (0,) * a.ndim)
    row = lambda n: pl.BlockSpec((1, TOK, n), lambda i, j: (i, j, 0))
    cb = conv_b[None, :]
    return pl.pallas_call(
        _hy_prep_kernel,
        grid=(b, lt // TOK),
        in_specs=[pl.BlockSpec((1, SUBLANES, HY_IN), lambda i, j: (i, jnp.maximum(j * nh - 1, 0), 0)),
                  row(HY_IN),
                  pl.BlockSpec((1, SUBLANES, HY_IN), lambda i, j: (i, jnp.minimum((j + 1) * nh, last), 0)),
                  whole(conv_w), whole(cb)],
        out_specs=[row(HY_CH), row(HY_CH)],
        out_shape=[jax.ShapeDtypeStruct((b, lt, HY_CH), _F32)] * 2,
        scratch_shapes=[pltpu.VMEM((TOK + 2 * SUBLANES, HY_IN), _F32)],
        compiler_params=_cparams(2),
        name="hyena_prep",
    )(hy_in, hy_in, hy_in, conv_w, cb)


def _hy_filter_consts(l):
    def emb(t):
        t = t.astype(np.float64)
        t_norm = t / max(l - 1, 1)
        bands = np.linspace(1e-4, HY_BANDS - 1, HY_BANDS)
        ang = 2.0 * math.pi * t[:, None] * bands[None, :] / l
        z = np.concatenate([t_norm[:, None], np.cos(ang), np.sin(ang)], axis=-1)
        return np.pad(z, ((0, 0), (0, LANES - HY_EMB))), t_norm[:, None]
    r = np.arange(l)
    e_rev, tn_rev = emb(l - 1 - r)
    e_sh, tn_sh = emb(r + 1)
    deltas = np.abs(np.linspace(HY_MIN_DECAY, HY_MAX_DECAY, HY_CH))[None, :]
    f = lambda a: np.asarray(a, np.float32)
    return f(e_rev), f(e_sh), f(tn_rev), f(tn_sh), f(deltas)


def _hy_filter_kernel(er_ref, es_ref, tr_ref, ts_ref, dl_ref, w1_ref, b1_ref, w2_ref, b2_ref, w3_ref, b3_ref, o_ref):
    l = er_ref.shape[0]

    def mlp(e, col):
        h = jnp.sin(jnp.dot(e, w1_ref[...], precision=_HI, preferred_element_type=_F32) + b1_ref[...])
        h = jnp.sin(jnp.dot(h, w2_ref[...], precision=_HI, preferred_element_type=_F32) + b2_ref[...])
        return (jnp.dot(h, w3_ref[:, col * HY_CH:(col + 1) * HY_CH], precision=_HI, preferred_element_type=_F32)
                + b3_ref[:, col * HY_CH:(col + 1) * HY_CH])

    hf = mlp(er_ref[...], 0) * jnp.exp(-tr_ref[...] * dl_ref[...])
    hb = mlp(es_ref[...], 1) * jnp.exp(-ts_ref[...] * dl_ref[...])
    row = lax.broadcasted_iota(jnp.int32, hb.shape, 0)
    hb = jnp.where(row < l - 1, hb, 0.0)
    norm = jnp.sum(jnp.abs(hf), axis=0, keepdims=True) + jnp.sum(jnp.abs(hb), axis=0, keepdims=True)
    o_ref[...] = jnp.transpose(jnp.concatenate([hf, hb], axis=0) / norm)


def _hy_filter(l, w1, b1, w2, b2, w3, b3):
    consts = [jnp.asarray(a) for a in _hy_filter_consts(l)]
    w1p = jnp.pad(w1, ((0, LANES - HY_EMB), (0, 0)))
    args = consts + [w1p, b1[None, :], w2, b2[None, :], w3, b3[None, :]]
    return pl.pallas_call(
        _hy_filter_kernel,
        out_shape=jax.ShapeDtypeStruct((HY_CH, 2 * l), _F32),
        compiler_params=pltpu.CompilerParams(vmem_limit_bytes=VMEM_LIMIT_BYTES),
        name="hyena_filter",
    )(*args)


def _hy_conv_kernel(f_ref, z_ref, y_ref, g_ref, *, nblk, nb):
    base = pltpu.roll(jnp.broadcast_to(f_ref[0], (SUBLANES, f_ref.shape[2])), 1, 1, stride=1, stride_axis=0)
    for a in range(HY_LO // SUBLANES):
        rows = base if a == 0 else pltpu.roll(base, SUBLANES * a, 1)
        g_ref[a * SUBLANES:(a + 1) * SUBLANES, :] = rows
    z = z_ref[0]
    lane = lax.broadcasted_iota(jnp.int32, (HY_BLK, z.shape[1]), 1)
    y = jnp.zeros((HY_BLK, z.shape[1]), _F32)
    for d in range(-(nblk - 1), nblk):
        o = HY_BLK * (nblk - d)
        t_d = jnp.concatenate([g_ref[:, o:o + HY_BLK], g_ref[:, o - HY_LO:o - HY_LO + HY_BLK]], axis=0).astype(_BF16)
        r = jnp.dot(t_d, z, preferred_element_type=_F32)
        if d > 0:
            r = jnp.where(lane >= nb * d, pltpu.roll(r, nb * d, 1), 0.0)
        elif d < 0:
            r = jnp.where(lane < nb * (nblk + d), pltpu.roll(r, z.shape[1] + nb * d, 1), 0.0)
        y = y + r
    y_ref[0] = y


def _hy_conv(fline, z):
    b, l, c = z.shape
    nblk = l // HY_BLK
    cols = max(nblk * b, LANES)
    zall = z.reshape(b, nblk, HY_BLK, c).transpose(3, 2, 1, 0).reshape(c, HY_BLK, nblk * b)
    zall = jnp.pad(zall, ((0, 0), (0, 0), (0, cols - nblk * b))).astype(_BF16)
    y = pl.pallas_call(
        functools.partial(_hy_conv_kernel, nblk=nblk, nb=b),
        grid=(c,),
        in_specs=[pl.BlockSpec((1, 1, 2 * l), lambda i: (i, 0, 0)),
                  pl.BlockSpec((1, HY_BLK, cols), lambda i: (i, 0, 0))],
        out_specs=pl.BlockSpec((1, HY_BLK, cols), lambda i: (i, 0, 0)),
        out_shape=jax.ShapeDtypeStruct((c, HY_BLK, cols), _F32),
        scratch_shapes=[pltpu.VMEM((HY_LO, 2 * l), _F32)],
        compiler_params=_cparams(1),
        name="hyena_conv",
    )(fline.reshape(c, 1, 2 * l), zall)
    return y[:, :, :nblk * b].reshape(c, HY_BLK, nblk, b).transpose(3, 2, 1, 0).reshape(b, l, c)


def _out_proj_kernel(x_ref, of_ref, ob_ref, z_ref, gn_ref, hones_ref, mla_ref, hx_ref, hz_ref, hy_ref, hd_ref,
                     ga_ref, w_ref, o_ref):
    o = of_ref[0] + ob_ref[0]
    ms = jnp.dot(o * o, hones_ref[...], precision=_HI, preferred_element_type=_F32) * (1.0 / GDN_DV)
    gdn = o * lax.rsqrt(ms + EPS) * gn_ref[...] * _silu(z_ref[0])
    hy = hx_ref[0] * (hy_ref[0] + hz_ref[0] * hd_ref[...])
    mix = jnp.concatenate([gdn, mla_ref[0], hy], axis=-1).astype(_BF16)
    y = jnp.dot(mix, w_ref[...], preferred_element_type=_F32)
    o_ref[0] = x_ref[0] + ga_ref[0] * y


def _out_proj(x, o_f, o_b, z, gn_row, mla, hx0, hz, hy, hd_row, ga, w_out, toff):
    b, l, d = x.shape
    hones = jnp.asarray(_gdn_consts()[2])
    row = lambda n: pl.BlockSpec((1, TOK, n), lambda i, j: (i, j, 0))
    rowc = lambda n: pl.BlockSpec((1, TOK, n), lambda i, j: (i, j + toff, 0))
    whole = lambda a: pl.BlockSpec(a.shape, lambda i, j: (0,) * a.ndim)
    return pl.pallas_call(
        _out_proj_kernel,
        grid=(b, l // TOK),
        in_specs=[row(d), row(GDN_WIDTH), row(GDN_WIDTH), rowc(GDN_WIDTH), whole(gn_row), whole(hones),
                  row(MLA_WIDTH), rowc(HY_WIDTH), rowc(HY_WIDTH), row(HY_WIDTH), whole(hd_row),
                  pl.BlockSpec((1, 1, d), lambda i, j: (i, 0, 0)),
                  pl.BlockSpec((d, d), lambda i, j: (0, 0))],
        out_specs=row(d),
        out_shape=jax.ShapeDtypeStruct((b, l, d), _F32),
        compiler_params=_cparams(2),
        name="out_proj",
    )(x, o_f, o_b, z, gn_row, hones, mla, hx0, hz, hy, hd_row, ga, w_out)


def _ffn_kernel(xp_ref, x_ref, xn_ref, g_ref, sf_ref, cf_ref, gf_ref, wup_ref, cw_ref, cb_ref, wdn_ref,
                o_ref, h_ref, up_ref, act_ref, *, tl):
    i = pl.program_id(1)
    nt = pl.num_programs(1)
    g, sf, cf = g_ref[...], sf_ref[0], cf_ref[0]
    pv = (i > 0).astype(_F32)
    nv = (i < nt - 1).astype(_F32)
    h_ref[0:HALO] = (_norm_mod(xp_ref[0], g, sf, cf) * pv).astype(_BF16)
    h_ref[HALO:HALO + tl] = _norm_mod(x_ref[0], g, sf, cf).astype(_BF16)
    h_ref[HALO + tl:] = (_norm_mod(xn_ref[0], g, sf, cf) * nv).astype(_BF16)

    def chunk(c, carry):
        lo = pl.multiple_of(c * FFN_CHUNK, FFN_CHUNK)
        wg = wup_ref[:, pl.ds(lo, FFN_CHUNK)]
        wv = wup_ref[:, pl.ds(D_FF + lo, FFN_CHUNK)]
        ug = jnp.dot(h_ref[...], wg, preferred_element_type=_F32)
        up_ref[...] = ug
        uv = jnp.dot(h_ref[HALO:HALO + tl], wv, preferred_element_type=_F32)
        cw = cw_ref[:, pl.ds(lo, FFN_CHUNK)]
        cb = cb_ref[:, pl.ds(lo, FFN_CHUNK)]
        gate = (up_ref[HALO - 1:HALO - 1 + tl] * cw[0:1] + ug[HALO:HALO + tl] * cw[1:2]
                + up_ref[HALO + 1:HALO + 1 + tl] * cw[2:3] + cb)
        act_ref[:, pl.ds(lo, FFN_CHUNK)] = (_silu(gate) * uv).astype(_BF16)
        return carry

    lax.fori_loop(0, D_FF // FFN_CHUNK, chunk, 0)
    y = jnp.dot(act_ref[...], wdn_ref[...], preferred_element_type=_F32)
    o_ref[0] = x_ref[0] + gf_ref[0] * y


def _ffn(x, g, sf, cf, gf, w_up, conv_w, conv_b, w_down, tl):
    b, l, d = x.shape
    nh = tl // HALO
    last = l // HALO - 1
    vec = pl.BlockSpec((1, 1, d), lambda i, j: (i, 0, 0))
    whole = lambda a: pl.BlockSpec(a.shape, lambda i, j: (0,) * a.ndim)
    return pl.pallas_call(
        functools.partial(_ffn_kernel, tl=tl),
        grid=(b, l // tl),
        in_specs=[
            pl.BlockSpec((1, HALO, d), lambda i, j: (i, jnp.maximum(j * nh - 1, 0), 0)),
            pl.BlockSpec((1, tl, d), lambda i, j: (i, j, 0)),
            pl.BlockSpec((1, HALO, d), lambda i, j: (i, jnp.minimum((j + 1) * nh, last), 0)),
            pl.BlockSpec((1, d), lambda i, j: (0, 0)),
            vec, vec, vec,
            whole(w_up), whole(conv_w), whole(conv_b), whole(w_down),
        ],
        out_specs=pl.BlockSpec((1, tl, d), lambda i, j: (i, j, 0)),
        out_shape=jax.ShapeDtypeStruct((b, l, d), _F32),
        scratch_shapes=[
            pltpu.VMEM((tl + 2 * HALO, d), _BF16),
            pltpu.VMEM((tl + 2 * HALO, FFN_CHUNK), _F32),
            pltpu.VMEM((tl, D_FF), _BF16),
        ],
        compiler_params=_cparams(2),
        name="conv_ffn",
    )(x, x, x, g, sf, cf, gf, w_up, conv_w, conv_b, w_down)


def rms_norm(x, g):
    xf = x.astype(jnp.float32)
    y = xf * lax.rsqrt(jnp.mean(xf * xf, axis=-1, keepdims=True) + EPS)
    return (y * g.astype(jnp.float32)).astype(x.dtype)


def dwconv3(x, w, b=None):
    xp = jnp.pad(x, ((0, 0), (1, 1), (0, 0)))
    y = xp[:, :-2] * w[0] + xp[:, 1:-1] * w[1] + xp[:, 2:] * w[2]
    return y if b is None else y + b


def axial_rope(l):
    rows = l // GRID_W
    row = jnp.repeat(jnp.arange(rows, dtype=jnp.float32), GRID_W)
    col = jnp.tile(jnp.arange(GRID_W, dtype=jnp.float32), rows)
    inv = ROPE_BASE ** (-jnp.arange(0, ROPE_AXIS, 2, dtype=jnp.float32) / ROPE_AXIS)
    ang = jnp.concatenate([row[:, None] * inv, col[:, None] * inv], axis=-1)
    return jnp.cos(ang), jnp.sin(ang)


def apply_rope(x, cos, sin):
    half = x.shape[-1] // 2
    x1, x2 = x[..., :half], x[..., half:]
    cos, sin = cos[None, :, None, :], sin[None, :, None, :]
    return jnp.concatenate([x1 * cos - x2 * sin, x1 * sin + x2 * cos], axis=-1)


def mla_heads(p, q_norm_g, w_uq, kv_norm_g, w_ukv, q_head_g, k_head_g, rope):
    b, l, _ = p.shape
    c_q = rms_norm(p[..., :MLA_Q_RANK], q_norm_g)
    c_kv = rms_norm(p[..., MLA_Q_RANK:MLA_Q_RANK + MLA_KV_RANK], kv_norm_g)
    k_rope = p[..., MLA_Q_RANK + MLA_KV_RANK:MLA_IN]
    q = (c_q @ w_uq).reshape(b, l, MLA_HEADS, MLA_QK)
    kv = (c_kv @ w_ukv).reshape(b, l, MLA_HEADS, MLA_NOPE + MLA_V)
    k = jnp.concatenate([kv[..., :MLA_NOPE],
                         jnp.broadcast_to(k_rope[:, :, None, :], (b, l, MLA_HEADS, MLA_ROPE))], axis=-1)
    v = kv[..., MLA_NOPE:]
    q = rms_norm(q, q_head_g)
    k = rms_norm(k, k_head_g)
    if rope is not None:
        cos, sin = rope
        q = jnp.concatenate([q[..., :MLA_NOPE], apply_rope(q[..., MLA_NOPE:], cos, sin)], axis=-1)
        k = jnp.concatenate([k[..., :MLA_NOPE], apply_rope(k[..., MLA_NOPE:], cos, sin)], axis=-1)
    return q, k, v


def softmax_attend(q, k, v, scale):
    s = jnp.einsum('bqhd,bkhd->bhqk', q, k).astype(jnp.float32) * scale
    p = jax.nn.softmax(s, axis=-1).astype(v.dtype)
    return jnp.einsum('bhqk,bkhd->bqhd', p, v)


def mla_mixer(p_ctx, p_lat, q_norm_g, w_uq, kv_norm_g, w_ukv, q_head_g, k_head_g, rope, with_ctx):
    qc, kc, vc = mla_heads(p_ctx, q_norm_g, w_uq, kv_norm_g, w_ukv, q_head_g, k_head_g, None)
    ql, kl, vl = mla_heads(p_lat, q_norm_g, w_uq, kv_norm_g, w_ukv, q_head_g, k_head_g, rope)
    scale = MLA_QK ** -0.5
    k_all = jnp.concatenate([kl, kc], axis=1)
    v_all = jnp.concatenate([vl, vc], axis=1)
    b, l = ql.shape[0], ql.shape[1]
    nb = l // ATTN_BLOCK
    q_blocks = jnp.moveaxis(ql.reshape(b, nb, ATTN_BLOCK, MLA_HEADS, MLA_QK), 1, 0)
    o_lat = lax.map(lambda qb: softmax_attend(qb, k_all, v_all, scale), q_blocks)
    o_lat = jnp.moveaxis(o_lat, 0, 1).reshape(b, l, MLA_WIDTH)
    if not with_ctx:
        return jnp.zeros((b, qc.shape[1], MLA_WIDTH), o_lat.dtype), o_lat
    o_ctx = softmax_attend(qc, kc, vc, scale).reshape(b, qc.shape[1], MLA_WIDTH)
    return o_ctx, o_lat


def hyena_filter(l, w1, b1, w2, b2, w3, b3):
    t = jnp.arange(l, dtype=jnp.float32)
    t_norm = t / max(l - 1, 1)
    bands = jnp.linspace(1e-4, HY_BANDS - 1, HY_BANDS, dtype=jnp.float32)
    ang = 2.0 * math.pi * t[:, None] * bands[None, :] / l
    z = jnp.concatenate([t_norm[:, None], jnp.cos(ang), jnp.sin(ang)], axis=-1)
    h = jnp.sin(z @ w1 + b1)
    h = jnp.sin(h @ w2 + b2)
    h = (h @ w3 + b3).reshape(l, 2, HY_CH).astype(jnp.float32)
    deltas = jnp.abs(jnp.linspace(HY_MIN_DECAY, HY_MAX_DECAY, HY_CH, dtype=jnp.float32))
    h = h * jnp.exp(-t_norm[:, None, None] * deltas)
    buf = jnp.concatenate([h[:, 0], jnp.zeros((1, HY_CH), jnp.float32), h[:0:-1, 1]], axis=0)
    return buf / jnp.sum(jnp.abs(buf), axis=0, keepdims=True)


def hyena_mixer(p, conv_w, conv_b, w1, b1, w2, b2, w3, b3, d_skip):
    b, l, _ = p.shape
    u = dwconv3(p, conv_w, conv_b)
    x0, x1, v = u[..., :HY_CH], u[..., HY_CH:2 * HY_CH], u[..., 2 * HY_CH:]
    z = (v * x1).astype(jnp.float32)
    buf = hyena_filter(l, w1, b1, w2, b2, w3, b3)
    zf = jnp.fft.rfft(z, n=2 * l, axis=1)
    hf = jnp.fft.rfft(buf, n=2 * l, axis=0)
    y = jnp.fft.irfft(zf * hf[None], n=2 * l, axis=1)[:, :l] + z * d_skip
    return x0 * y.astype(x0.dtype)


def _pad_row(v, n=LANES):
    v = v.reshape(1, -1)
    return jnp.pad(v, ((0, 0), (0, n - v.shape[1])))


def kernel(x, c, ctx, c_ctx, ada_w, ada_b, mix_norm_g, w_in, gdn_conv_w, gdn_a_log, gdn_dt_bias, gdn_norm_g, mla_q_norm_g, mla_w_uq, mla_kv_norm_g, mla_w_ukv, mla_q_head_g, mla_k_head_g, hy_conv_w, hy_conv_b, hy_w1, hy_b1, hy_w2, hy_b2, hy_w3, hy_b3, hy_d, w_out, ffn_norm_g, ffn_w_up, ffn_conv_w, ffn_conv_b, ffn_w_down):
    bsz, seq, d = x.shape
    n_ctx = ctx.shape[1]
    assert n_ctx == TOK and seq % TOK == 0 and bsz < ADA_ROWS
    cond = jnp.concatenate([c, c_ctx[None, :], jnp.zeros((ADA_ROWS - bsz - 1, d), c.dtype)], axis=0)
    for i in range(DEPTH):
        last = i == DEPTH - 1
        mod = _ada_mod(cond, ada_w[i], ada_b[i])
        mod_lat = mod[:bsz, None, :]
        mod_ctx = mod[bsz][None, None, :]
        sa_l, ca_l, ga_l, sf_l, cf_l, gf_l = jnp.split(mod_lat, 6, axis=-1)
        sa_c, ca_c, ga_c, sf_c, cf_c, gf_c = (jnp.broadcast_to(t, (bsz, 1, d)) for t in jnp.split(mod_ctx, 6, axis=-1))

        w_in_p = _pad_w_in(w_in[i])
        g_mix = mix_norm_g[i][None, :]
        qkv, z, ab, mla_in, hy_in = _in_proj(x, ctx, g_mix, sa_l, ca_l, sa_c[:1], ca_c[:1], w_in_p)

        q, k, v, gf, gb, bf, bb = _gdn_prep(qkv, ab, gdn_conv_w[i], _pad_row(gdn_a_log[i]), _pad_row(gdn_dt_bias[i]))
        oc_f, oc_b, ol_f, ol_b = _gdn_scan(q, k, v, gf, gb, bf, bb)

        mla_w = _mla_weights(mla_w_uq[i], mla_w_ukv[i], mla_q_head_g[i], mla_k_head_g[i])
        q_ctx, q_lat, k_all, v_all = _mla_prep(mla_in, mla_q_norm_g[i], mla_kv_norm_g[i], mla_w, seq)
        mla_l = _attention(q_lat, k_all, v_all, n_ctx + seq, ATTN_TQ)
        hy_x0, hy_z = _hy_prep(hy_in, hy_conv_w[i], hy_conv_b[i])
        hy_mlp = (hy_w1[i], hy_b1[i], hy_w2[i], hy_b2[i], hy_w3[i], hy_b3[i])
        hy_l = _hy_conv(_hy_filter(seq, *hy_mlp), hy_z[:, TOK:])
        hd_row = hy_d[i][None, :]

        w_out_b = w_out[i].astype(_BF16)
        w_up_b = ffn_w_up[i].astype(_BF16)
        w_dn_b = ffn_w_down[i].astype(_BF16)
        g_ffn = ffn_norm_g[i][None, :]
        cb = ffn_conv_b[i][None, :]
        gn_row = jnp.tile(gdn_norm_g[i], GDN_HEADS)[None, :]

        x = _out_proj(x, ol_f, ol_b, z, gn_row, mla_l, hy_x0, hy_z, hy_l, hd_row, ga_l, w_out_b, 1)
        x = _ffn(x, g_ffn, sf_l, cf_l, gf_l, w_up_b, ffn_conv_w[i], cb, w_dn_b, 512)

        if not last:
            mla_c = _attention(q_ctx, k_all, v_all, n_ctx, TOK)
            hy_c = _hy_conv(_hy_filter(n_ctx, *hy_mlp), hy_z[:, :TOK])
            ctx = _out_proj(ctx, oc_f, oc_b, z, gn_row, mla_c, hy_x0, hy_z, hy_c, hd_row, ga_c, w_out_b, 0)
            ctx = _ffn(ctx, g_ffn, sf_c, cf_c, gf_c, w_up_b, ffn_conv_w[i], cb, w_dn_b, TOK)
    return x
```

```python
import functools
import math

import jax
import jax.numpy as jnp
import numpy as np
from jax import lax
from jax.experimental import pallas as pl
from jax.experimental.pallas import tpu as pltpu

D_MODEL = 1024
DEPTH = 2
GRID_W = 64
EPS = 1e-6

GDN_HEADS = 6
GDN_DK = 64
GDN_DV = 64
GDN_CHUNK = 64

MLA_HEADS = 6
MLA_Q_RANK = 256
MLA_KV_RANK = 128
MLA_NOPE = 64
MLA_ROPE = 32
MLA_V = 64
MLA_QK = MLA_NOPE + MLA_ROPE
ATTN_BLOCK = 128
ROPE_BASE = 10000.0
ROPE_AXIS = MLA_ROPE // 2

HY_CH = 256
HY_BANDS = 16
HY_EMB = 1 + 2 * HY_BANDS
HY_HIDDEN = 64
HY_TARGET = 1e-2
HY_FAST_DECAY_PCT = 0.3
HY_SLOW_DECAY_PCT = 1.5
HY_MAX_DECAY = math.log(HY_TARGET) / HY_FAST_DECAY_PCT
HY_MIN_DECAY = math.log(HY_TARGET) / HY_SLOW_DECAY_PCT

D_FF = 2816

GDN_WIDTH = GDN_HEADS * GDN_DV
MLA_WIDTH = MLA_HEADS * MLA_V
HY_WIDTH = HY_CH
GDN_QKV = GDN_HEADS * (2 * GDN_DK + GDN_DV)
GDN_IN = GDN_QKV + GDN_WIDTH + 4 * GDN_HEADS
MLA_IN = MLA_Q_RANK + MLA_KV_RANK + MLA_ROPE
HY_IN = 3 * HY_CH

LANES = 128
SUBLANES = 8
SUBLANES_BF16 = 16
VMEM_LIMIT_BYTES = 56 * 1024 * 1024

TOK = 256
GDN_PAIRS = GDN_HEADS // 2
N_GATE = 4 * GDN_HEADS

AB_PAD = LANES
MLA_PAD = 512
IN_GROUPS = (GDN_QKV, GDN_WIDTH, AB_PAD, MLA_PAD, HY_IN)
IN_TOTAL = sum(IN_GROUPS)

FFN_CHUNK = 256
HALO = SUBLANES_BF16

_BF16 = jnp.bfloat16
_F32 = jnp.float32
_HI = lax.Precision.HIGHEST
_NT = (((1,), (1,)), ((), ()))
_TN = (((0,), (0,)), ((), ()))


def _cparams(n_axes, sem=None):
    return pltpu.CompilerParams(
        dimension_semantics=sem or ("parallel",) * n_axes, vmem_limit_bytes=VMEM_LIMIT_BYTES)


def _norm_mod(x, g, shift, scale):
    ms = jnp.mean(x * x, axis=-1, keepdims=True)
    y = x * lax.rsqrt(ms + EPS) * g
    return y * (1.0 + scale) + shift


def _silu(x):
    return x * jax.nn.sigmoid(x)


ADA_ROWS = 16
ADA_TN = 1024


def _ada_kernel(c_ref, w_ref, b_ref, o_ref):
    o_ref[...] = jnp.dot(_silu(c_ref[...]), w_ref[...], precision=_HI, preferred_element_type=_F32) + b_ref[...]


def _ada_mod(cond, w, b):
    d, n = w.shape
    return pl.pallas_call(
        _ada_kernel,
        grid=(n // ADA_TN,),
        in_specs=[pl.BlockSpec((ADA_ROWS, d), lambda j: (0, 0)),
                  pl.BlockSpec((d, ADA_TN), lambda j: (0, j)),
                  pl.BlockSpec((1, ADA_TN), lambda j: (0, j))],
        out_specs=pl.BlockSpec((ADA_ROWS, ADA_TN), lambda j: (0, j)),
        out_shape=jax.ShapeDtypeStruct((ADA_ROWS, n), _F32),
        compiler_params=_cparams(1),
        name="ada_mod",
    )(cond, w, b[None, :])


def _in_proj_kernel(x_ref, c_ref, g_ref, sl_ref, cl_ref, sc_ref, cc_ref, w_ref, *out_refs):
    is_ctx = pl.program_id(1) == 0
    x = jnp.where(is_ctx, c_ref[0], x_ref[0])
    shift = jnp.where(is_ctx, sc_ref[0], sl_ref[0])
    scale = jnp.where(is_ctx, cc_ref[0], cl_ref[0])
    h = _norm_mod(x, g_ref[...], shift, scale)
    p = jnp.dot(h.astype(_BF16), w_ref[...], preferred_element_type=_F32)
    off = 0
    for o_ref, n in zip(out_refs, IN_GROUPS):
        o_ref[0] = p[:, off:off + n]
        off += n


def _in_proj(x, ctx, g, shift_l, scale_l, shift_c, scale_c, w_pad):
    b, l, d = x.shape
    nt = 1 + l // TOK
    vec_l = pl.BlockSpec((1, 1, d), lambda i, j: (i, 0, 0))
    vec_c = pl.BlockSpec((1, 1, d), lambda i, j: (0, 0, 0))
    return pl.pallas_call(
        _in_proj_kernel,
        grid=(b, nt),
        in_specs=[
            pl.BlockSpec((1, TOK, d), lambda i, j: (i, jnp.maximum(j - 1, 0), 0)),
            pl.BlockSpec((1, TOK, d), lambda i, j: (i, 0, 0)),
            pl.BlockSpec((1, d), lambda i, j: (0, 0)),
            vec_l, vec_l, vec_c, vec_c,
            pl.BlockSpec((d, IN_TOTAL), lambda i, j: (0, 0)),
        ],
        out_specs=[pl.BlockSpec((1, TOK, n), lambda i, j: (i, j, 0)) for n in IN_GROUPS],
        out_shape=[jax.ShapeDtypeStruct((b, nt * TOK, n), _F32) for n in IN_GROUPS],
        compiler_params=_cparams(2),
        name="in_proj",
    )(x, ctx, g, shift_l, scale_l, shift_c, scale_c, w_pad)


def _pad_w_in(w_in):
    s1 = GDN_QKV + GDN_WIDTH
    s2 = GDN_IN
    s3 = GDN_IN + MLA_IN
    d = w_in.shape[0]
    z = lambda n: jnp.zeros((d, n), w_in.dtype)
    parts = [w_in[:, :s1], w_in[:, s1:s2], z(AB_PAD - N_GATE),
             w_in[:, s2:s3], z(MLA_PAD - MLA_IN), w_in[:, s3:]]
    return jnp.concatenate(parts, axis=1).astype(_BF16)


def _gdn_consts():
    r = np.arange(TOK)
    same = (r[:, None] // GDN_CHUNK) == (r[None, :] // GDN_CHUNK)
    tril = (same & (r[None, :] <= r[:, None])).astype(np.float32)
    triu = (same & (r[None, :] >= r[:, None])).astype(np.float32)
    c = np.arange(GDN_WIDTH)
    head_ones = (c[:, None] // GDN_DK == c[None, :] // GDN_DK).astype(np.float32)
    expand = np.zeros((LANES, 4 * GDN_WIDTH), np.float32)
    for k in range(4):
        for h in range(GDN_HEADS):
            expand[k * GDN_HEADS + h, k * GDN_WIDTH + h * GDN_DK:k * GDN_WIDTH + (h + 1) * GDN_DK] = 1.0
    return tril, triu, head_ones, expand


def _gdn_prep_kernel(xp_ref, x_ref, xn_ref, ab_ref, cw_ref, alog_ref, dt_ref, tril_ref, triu_ref, hones_ref,
                     exp_ref, q_ref, k_ref, v_ref, gf_ref, gb_ref, bf_ref, bb_ref, xe_ref):
    j = pl.program_id(1)
    nt = pl.num_programs(1)
    pv = (j >= 2).astype(_F32)
    nv = jnp.logical_and(j >= 1, j < nt - 1).astype(_F32)
    xe_ref[0:SUBLANES] = xp_ref[0] * pv
    xe_ref[SUBLANES:SUBLANES + TOK] = x_ref[0]
    xe_ref[SUBLANES + TOK:] = xn_ref[0] * nv
    cw = cw_ref[...]
    y = (xe_ref[SUBLANES - 1:SUBLANES - 1 + TOK] * cw[0:1] + x_ref[0] * cw[1:2]
         + xe_ref[SUBLANES + 1:SUBLANES + 1 + TOK] * cw[2:3])
    y = _silu(y)
    hk = GDN_HEADS * GDN_DK
    q, k, v = y[:, :hk], y[:, hk:2 * hk], y[:, 2 * hk:]
    hones = hones_ref[...]
    qs = jnp.dot(q * q, hones, precision=_HI, preferred_element_type=_F32)
    ks = jnp.dot(k * k, hones, precision=_HI, preferred_element_type=_F32)
    q_ref[0] = q * lax.rsqrt(qs + EPS) * (GDN_DK ** -0.5)
    k_ref[0] = k * lax.rsqrt(ks + EPS)
    v_ref[0] = v

    ab = ab_ref[0]
    lane = lax.broadcasted_iota(jnp.int32, ab.shape, 1)
    a_in = ab + dt_ref[...]
    softplus = jnp.maximum(a_in, 0.0) + jnp.log(1.0 + jnp.exp(-jnp.abs(a_in)))
    g = jnp.where(lane < 2 * GDN_HEADS, -jnp.exp(alog_ref[...]) * softplus, 0.0)
    gc_f = jnp.dot(tril_ref[...], g, precision=_HI, preferred_element_type=_F32)
    gc_b = jnp.dot(triu_ref[...], g, precision=_HI, preferred_element_type=_F32)
    cols = jnp.where(lane < GDN_HEADS, gc_f, jnp.where(lane < 2 * GDN_HEADS, gc_b, jax.nn.sigmoid(ab)))
    wide = jnp.dot(cols, exp_ref[...], precision=_HI, preferred_element_type=_F32)
    gf_ref[0] = wide[:, 0:GDN_WIDTH]
    gb_ref[0] = wide[:, GDN_WIDTH:2 * GDN_WIDTH]
    bf_ref[0] = wide[:, 2 * GDN_WIDTH:3 * GDN_WIDTH]
    bb_ref[0] = wide[:, 3 * GDN_WIDTH:]


def _gdn_prep(qkv, ab, conv_w, a_log_row, dt_row):
    b, lt, _ = qkv.shape
    nt = lt // TOK
    nh = TOK // SUBLANES
    last = lt // SUBLANES - 1
    consts = [jnp.asarray(a) for a in _gdn_consts()]
    whole = lambda a: pl.BlockSpec(a.shape, lambda i, j: (0,) * a.ndim)
    row = lambda n: pl.BlockSpec((1, TOK, n), lambda i, j: (i, j, 0))
    return pl.pallas_call(
        _gdn_prep_kernel,
        grid=(b, nt),
        in_specs=[
            pl.BlockSpec((1, SUBLANES, GDN_QKV), lambda i, j: (i, jnp.maximum(j * nh - 1, 0), 0)),
            row(GDN_QKV),
            pl.BlockSpec((1, SUBLANES, GDN_QKV), lambda i, j: (i, jnp.minimum((j + 1) * nh, last), 0)),
            row(AB_PAD), whole(conv_w), whole(a_log_row), whole(dt_row),
        ] + [whole(a) for a in consts],
        out_specs=[row(GDN_WIDTH)] * 7,
        out_shape=[jax.ShapeDtypeStruct((b, lt, GDN_WIDTH), _F32)] * 7,
        scratch_shapes=[pltpu.VMEM((TOK + 2 * SUBLANES, GDN_QKV), _F32)],
        compiler_params=_cparams(2),
        name="gdn_prep",
    )(qkv, qkv, qkv, ab, conv_w, a_log_row, dt_row, *consts)


def _block_diag(z, left):
    return jnp.concatenate([jnp.where(left, z, 0.0), jnp.where(left, 0.0, z)], axis=0).astype(_BF16)


def _mm(a, b):
    return jnp.dot(a.astype(_BF16), b, preferred_element_type=_F32)


def _gdn_chunk(q, k, v, gx, bx, s, backward, masks):
    left, eye2, incl, strict, ones64, diag_blocks = masks
    c = GDN_CHUNK
    yk = _block_diag(k, left)
    qk_kk = lax.dot_general(jnp.concatenate([q, k], axis=0).astype(_BF16), yk, _NT, preferred_element_type=_F32)
    qk, kk = qk_kk[:c], qk_kk[c:]
    d0 = jnp.where(eye2, gx, 0.0)
    t_hi = d0.astype(_BF16)
    r1 = d0 - t_hi.astype(_F32)
    t_mid = r1.astype(_BF16)
    t_lo = (r1 - t_mid.astype(_F32)).astype(_BF16)
    r3 = jnp.dot(ones64, jnp.concatenate([t_hi, t_mid, t_lo], axis=1), preferred_element_type=_F32)
    r = r3[:, :LANES] + r3[:, LANES:2 * LANES] + r3[:, 2 * LANES:]
    dec = jnp.where(incl, jnp.exp(jnp.where(incl, gx - r, 0.0)), 0.0)
    a = bx * kk * jnp.where(strict, dec, 0.0)
    qk = qk * dec
    eg = jnp.exp(gx)
    p = jnp.where(eye2, 1.0, 0.0) - a
    pw = a
    pw_bd = _block_diag(pw, left)
    for _ in range(5):
        pw = _mm(pw, pw_bd)
        pw_bd = _block_diag(pw, left)
        p = p + _mm(p, pw_bd)
    rhs = jnp.concatenate([_block_diag(bx * v, left), _block_diag(bx * k * eg, left)], axis=1)
    uw = _mm(p, rhs)
    u, w = uw[:, :LANES], uw[:, LANES:]
    qd = q * eg
    tot = gx[0:1] if backward else gx[c - 1:c]
    kd = k * jnp.exp(tot - gx)
    sb = s.astype(_BF16)
    ws_qs = _mm(jnp.concatenate([w, qd], axis=0), sb)
    delta = u - ws_qs[:c]
    o = ws_qs[c:] + _mm(qk, _block_diag(delta, left))
    upd = lax.dot_general(kd.astype(_BF16), delta.astype(_BF16), _TN, preferred_element_type=_F32)
    s = s * jnp.exp(tot) + jnp.where(diag_blocks, upd, 0.0)
    return o, s


def _gdn_scan_kernel(cq, ck, cv, cgf, cbf, cgb, cbb, fq, fk, fv, fg, fb, rq, rk, rv, rg, rb,
                     ocf_ref, ocb_ref, of_ref, ob_ref, sf_ref, sb_ref):
    step = pl.program_id(1)
    is_ctx = step == 0

    @pl.when(is_ctx)
    def _():
        sf_ref[...] = jnp.zeros_like(sf_ref)
        sb_ref[...] = jnp.zeros_like(sb_ref)

    c = GDN_CHUNK
    li = lax.broadcasted_iota(jnp.int32, (c, LANES), 1)
    ri = lax.broadcasted_iota(jnp.int32, (c, LANES), 0)
    lj = li & (c - 1)
    left = li < c
    eye2 = lj == ri
    ones64 = jnp.ones((c, c), _BF16)
    r2 = lax.broadcasted_iota(jnp.int32, (LANES, LANES), 0)
    c2 = lax.broadcasted_iota(jnp.int32, (LANES, LANES), 1)
    diag_blocks = (r2 // c) == (c2 // c)
    masks_f = (left, eye2, ri >= lj, ri > lj, ones64, diag_blocks)
    masks_b = (left, eye2, ri <= lj, ri < lj, ones64, diag_blocks)

    def run(refs, ctx_refs, s_ref, o_ref, oc_ref, backward, masks):
        vals = [jnp.where(is_ctx, cr[0], r[0]) for cr, r in zip(ctx_refs, refs)]
        order = range(TOK // c - 1, -1, -1) if backward else range(TOK // c)
        outs = {}
        for p in range(GDN_PAIRS):
            s = s_ref[p]
            for n in order:
                blk = [a[n * c:(n + 1) * c, p * LANES:(p + 1) * LANES] for a in vals]
                o, s = _gdn_chunk(*blk, s, backward, masks)
                outs[(n, p)] = o
            s_ref[p] = s
        o_all = jnp.concatenate(
            [jnp.concatenate([outs[(n, p)] for p in range(GDN_PAIRS)], axis=1) for n in range(TOK // c)], axis=0)

        @pl.when(is_ctx)
        def _():
            oc_ref[0] = o_all

        @pl.when(jnp.logical_not(is_ctx))
        def _():
            o_ref[0] = o_all

    run((fq, fk, fv, fg, fb), (cq, ck, cv, cgf, cbf), sf_ref, of_ref, ocf_ref, False, masks_f)
    run((rq, rk, rv, rg, rb), (cq, ck, cv, cgb, cbb), sb_ref, ob_ref, ocb_ref, True, masks_b)


def _gdn_intra(probs, masks):
    left, eye2, ones64, tri, same_blk = masks
    c = GDN_CHUNK
    qs, ks, vs, gxs, bxs, bws = zip(*probs)
    qk_kk = [lax.dot_general(jnp.concatenate([q, k], axis=0).astype(_BF16), _block_diag(k, left), _NT,
                             preferred_element_type=_F32) for q, k in zip(qs, ks)]
    rs = []
    for gx in gxs:
        d0 = jnp.where(eye2, gx, 0.0)
        t_hi = d0.astype(_BF16)
        r1 = d0 - t_hi.astype(_F32)
        t_mid = r1.astype(_BF16)
        t_lo = (r1 - t_mid.astype(_F32)).astype(_BF16)
        r3 = jnp.dot(ones64, jnp.concatenate([t_hi, t_mid, t_lo], axis=1), preferred_element_type=_F32)
        rs.append(r3[:, :LANES] + r3[:, LANES:2 * LANES] + r3[:, 2 * LANES:])
    a_s, qkm, egs = [], [], []
    for x, gx, bx, r, bw in zip(qk_kk, gxs, bxs, rs, bws):
        incl, strict = tri[bw]
        dec = jnp.where(incl, jnp.exp(jnp.where(incl, gx - r, 0.0)), 0.0)
        a_s.append(bx * x[c:] * jnp.where(strict, dec, 0.0))
        qkm.append(x[:c] * dec)
        egs.append(jnp.exp(gx))
    eye_f = jnp.where(eye2, 1.0, 0.0)
    base = same_blk[8]
    d1 = [jnp.where(base, a, 0.0) for a in a_s]
    ps = [eye_f - d for d in d1]
    d2 = [_mm(d, _block_diag(d, left)) for d in d1]
    d2_bd = [_block_diag(d, left) for d in d2]
    ps = [p + _mm(p, bd) for p, bd in zip(ps, d2_bd)]
    d4 = [_mm(d, bd) for d, bd in zip(d2, d2_bd)]
    ps = [p + _mm(p, _block_diag(d, left)) for p, d in zip(ps, d4)]
    for blk in (8, 16, 32):
        off = jnp.logical_and(same_blk[2 * blk], jnp.logical_not(same_blk[blk]))
        t1 = [_mm(p, _block_diag(jnp.where(off, a, 0.0), left)) for p, a in zip(ps, a_s)]
        ps = [p - _mm(t, _block_diag(p, left)) for p, t in zip(ps, t1)]
    out = []
    for p, q, k, v, gx, bx, eg, qk, bw in zip(ps, qs, ks, vs, gxs, bxs, egs, qkm, bws):
        tot = gx[0:1] if bw else gx[c - 1:c]
        lhs = jnp.concatenate([k * eg, q * eg], axis=0).astype(_BF16)
        out.append((p, lhs, bx, bx * v, qk, k * jnp.exp(tot - gx), tot))
    return out


def _gdn_state_step(chains, left, diag_blocks):
    c = GDN_CHUNK
    ys = [jnp.dot(x[1], s.astype(_BF16), preferred_element_type=_F32) for s, x in chains]
    resid = [x[3] - x[2] * y[:c] for (s, x), y in zip(chains, ys)]
    deltas = [_mm(x[0], _block_diag(r, left)) for (s, x), r in zip(chains, resid)]
    os_ = [y[c:] + _mm(x[4], _block_diag(d, left)) for (s, x), y, d in zip(chains, ys, deltas)]
    upds = [lax.dot_general(x[5].astype(_BF16), d.astype(_BF16), _TN, preferred_element_type=_F32)
            for (s, x), d in zip(chains, deltas)]
    new_s = [s * jnp.exp(x[6]) + jnp.where(diag_blocks, u, 0.0) for (s, x), u in zip(chains, upds)]
    return list(zip(os_, new_s))


def _gdn_scan_kernel_bf(cq, ck, cv, cgf, cbf, cgb, cbb, fq, fk, fv, fg, fb, rq, rk, rv, rg, rb,
                        ocf_ref, ocb_ref, of_ref, ob_ref, sf_ref, sb_ref):
    step = pl.program_id(1)
    is_ctx = step == 0

    @pl.when(is_ctx)
    def _():
        sf_ref[...] = jnp.zeros_like(sf_ref)
        sb_ref[...] = jnp.zeros_like(sb_ref)

    c = GDN_CHUNK
    nc = TOK // c
    li = lax.broadcasted_iota(jnp.int32, (c, LANES), 1)
    ri = lax.broadcasted_iota(jnp.int32, (c, LANES), 0)
    lj = li & (c - 1)
    left = li < c
    eye2 = lj == ri
    ones64 = jnp.ones((c, c), _BF16)
    r2 = lax.broadcasted_iota(jnp.int32, (LANES, LANES), 0)
    c2 = lax.broadcasted_iota(jnp.int32, (LANES, LANES), 1)
    diag_blocks = (r2 // c) == (c2 // c)
    tri = {False: (ri >= lj, ri > lj), True: (ri <= lj, ri < lj)}
    same_blk = {b: (ri // b) == (lj // b) for b in (8, 16, 32, 64)}
    masks = (left, eye2, ones64, tri, same_blk)

    fvals = [jnp.where(is_ctx, cr[0], r[0]) for cr, r in zip((cq, ck, cv, cgf, cbf), (fq, fk, fv, fg, fb))]
    bvals = [jnp.where(is_ctx, cr[0], r[0]) for cr, r in zip((cq, ck, cv, cgb, cbb), (rq, rk, rv, rg, rb))]
    keys, probs = [], []
    for bw, vals in ((False, fvals), (True, bvals)):
        for n in range(nc):
            for p in range(GDN_PAIRS):
                keys.append((bw, n, p))
                probs.append(tuple(a[n * c:(n + 1) * c, p * LANES:(p + 1) * LANES] for a in vals) + (bw,))
    intra = dict(zip(keys, _gdn_intra(probs, masks)))

    chain_keys = [(bw, p) for bw in (False, True) for p in range(GDN_PAIRS)]
    states = {(bw, p): (sb_ref if bw else sf_ref)[p] for bw, p in chain_keys}
    outs = {}
    for t in range(nc):
        ns = {(bw, p): (nc - 1 - t if bw else t) for bw, p in chain_keys}
        res = _gdn_state_step([(states[kk], intra[(kk[0], ns[kk], kk[1])]) for kk in chain_keys], left, diag_blocks)
        for kk, (o, s) in zip(chain_keys, res):
            outs[(kk[0], ns[kk], kk[1])] = o
            states[kk] = s
    for bw, p in chain_keys:
        (sb_ref if bw else sf_ref)[p] = states[(bw, p)]

    for bw, o_ref, oc_ref in ((False, of_ref, ocf_ref), (True, ob_ref, ocb_ref)):
        o_all = jnp.concatenate(
            [jnp.concatenate([outs[(bw, n, p)] for p in range(GDN_PAIRS)], axis=1) for n in range(nc)], axis=0)

        @pl.when(is_ctx)
        def _(o_all=o_all, oc_ref=oc_ref):
            oc_ref[0] = o_all

        @pl.when(jnp.logical_not(is_ctx))
        def _(o_all=o_all, o_ref=o_ref):
            o_ref[0] = o_all


def _gdn_scan(q, k, v, gf, gb, bf, bb):
    b, lt, w = q.shape
    nl = lt // TOK - 1
    ctx = pl.BlockSpec((1, TOK, w), lambda i, s: (i, 0, 0))
    fwd = pl.BlockSpec((1, TOK, w), lambda i, s: (i, jnp.maximum(s, 1), 0))
    bwd = pl.BlockSpec((1, TOK, w), lambda i, s: (i, nl + 1 - jnp.maximum(s, 1), 0))
    fwd_o = pl.BlockSpec((1, TOK, w), lambda i, s: (i, jnp.maximum(s, 1) - 1, 0))
    bwd_o = pl.BlockSpec((1, TOK, w), lambda i, s: (i, nl - jnp.maximum(s, 1), 0))
    return pl.pallas_call(
        _gdn_scan_kernel_bf,
        grid=(b, nl + 1),
        in_specs=[ctx] * 7 + [fwd] * 5 + [bwd] * 5,
        out_specs=[ctx, ctx, fwd_o, bwd_o],
        out_shape=[jax.ShapeDtypeStruct((b, TOK, w), _F32)] * 2 + [jax.ShapeDtypeStruct((b, nl * TOK, w), _F32)] * 2,
        scratch_shapes=[pltpu.VMEM((GDN_PAIRS, LANES, LANES), _F32)] * 2,
        compiler_params=_cparams(2, ("parallel", "arbitrary")),
        name="gdn_scan",
    )(q, k, v, gf, bf, gb, bb, q, k, v, gf, bf, q, k, v, gb, bb)


MLA_HEAD_PAD = LANES
MLA_WIDE = MLA_HEADS * MLA_HEAD_PAD
ATTN_TQ = 512
ATTN_TK = 512


def _rope_tables(n_ctx, seq):
    rows = seq // GRID_W
    row = np.repeat(np.arange(rows, dtype=np.float64), GRID_W)
    col = np.tile(np.arange(GRID_W, dtype=np.float64), rows)
    inv = ROPE_BASE ** (-np.arange(0, ROPE_AXIS, 2, dtype=np.float64) / ROPE_AXIS)
    ang = np.concatenate([row[:, None] * inv, col[:, None] * inv], axis=-1)
    cos, sin = np.cos(ang), np.sin(ang)
    half = MLA_ROPE // 2
    c = np.ones((n_ctx + seq, MLA_HEAD_PAD))
    s = np.zeros((n_ctx + seq, MLA_HEAD_PAD))
    c[n_ctx:, MLA_NOPE:MLA_NOPE + half] = cos
    c[n_ctx:, MLA_NOPE + half:MLA_QK] = cos
    s[n_ctx:, MLA_NOPE:MLA_NOPE + half] = -sin
    s[n_ctx:, MLA_NOPE + half:MLA_QK] = sin
    return c.astype(np.float32), s.astype(np.float32)


def _mla_prep_kernel(p_ref, qn_ref, kvn_ref, wq_ref, wk_ref, wv_ref, sel_ref, qg_ref, kg_ref, cos_ref, sin_ref,
                     qc_ref, ql_ref, k_ref, v_ref):
    p = p_ref[0]
    cq = p[:, :MLA_Q_RANK]
    ckv = p[:, MLA_Q_RANK:MLA_Q_RANK + MLA_KV_RANK]
    kr = p[:, MLA_Q_RANK + MLA_KV_RANK:]
    cq = (cq * lax.rsqrt(jnp.mean(cq * cq, axis=-1, keepdims=True) + EPS) * qn_ref[...]).astype(_BF16)
    ckv = (ckv * lax.rsqrt(jnp.mean(ckv * ckv, axis=-1, keepdims=True) + EPS) * kvn_ref[...]).astype(_BF16)
    q = jnp.dot(cq, wq_ref[...], preferred_element_type=_F32)
    k = (jnp.dot(ckv, wk_ref[...], preferred_element_type=_F32)
         + jnp.dot(kr, sel_ref[...], precision=_HI, preferred_element_type=_F32))
    lane = lax.broadcasted_iota(jnp.int32, (TOK, MLA_WIDE), 1) & (MLA_HEAD_PAD - 1)
    v = jnp.dot(ckv, wv_ref[...], preferred_element_type=_F32) + jnp.where(lane == MLA_V, 1.0, 0.0)
    cos = jnp.concatenate([cos_ref[...]] * MLA_HEADS, axis=1)
    sin = jnp.concatenate([sin_ref[...]] * MLA_HEADS, axis=1)
    half = MLA_ROPE // 2
    first = jnp.logical_and(lane >= MLA_NOPE, lane < MLA_NOPE + half)
    second = jnp.logical_and(lane >= MLA_NOPE + half, lane < MLA_QK)

    def head_norm_rope(x, g):
        parts = []
        for h in range(MLA_HEADS):
            xh = x[:, h * MLA_HEAD_PAD:(h + 1) * MLA_HEAD_PAD]
            ms = jnp.sum(xh * xh, axis=-1, keepdims=True) * (1.0 / MLA_QK)
            parts.append(xh * lax.rsqrt(ms + EPS))
        xn = jnp.concatenate(parts, axis=1) * g
        up = pltpu.roll(xn, half, 1)
        down = pltpu.roll(xn, MLA_WIDE - half, 1)
        swapped = jnp.where(first, down, jnp.where(second, up, 0.0))
        return xn * cos + swapped * sin

    qf = head_norm_rope(q, qg_ref[...]) * (MLA_QK ** -0.5 * math.log2(math.e))
    kf = head_norm_rope(k, kg_ref[...])
    is_ctx = pl.program_id(1) == 0
    for h in range(MLA_HEADS):
        sl = slice(h * MLA_HEAD_PAD, (h + 1) * MLA_HEAD_PAD)
        k_ref[0, h] = kf[:, sl].astype(_BF16)
        v_ref[0, h] = v[:, sl].astype(_BF16)

    @pl.when(is_ctx)
    def _():
        for h in range(MLA_HEADS):
            qc_ref[0, h] = qf[:, h * MLA_HEAD_PAD:(h + 1) * MLA_HEAD_PAD].astype(_BF16)

    @pl.when(jnp.logical_not(is_ctx))
    def _():
        for h in range(MLA_HEADS):
            ql_ref[0, h] = qf[:, h * MLA_HEAD_PAD:(h + 1) * MLA_HEAD_PAD].astype(_BF16)


def _mla_weights(w_uq, w_ukv, q_head_g, k_head_g):
    pad = MLA_HEAD_PAD
    wq = jnp.pad(w_uq.reshape(MLA_Q_RANK, MLA_HEADS, MLA_QK), ((0, 0), (0, 0), (0, pad - MLA_QK)))
    wkv = w_ukv.reshape(MLA_KV_RANK, MLA_HEADS, MLA_NOPE + MLA_V)
    wk = jnp.pad(wkv[:, :, :MLA_NOPE], ((0, 0), (0, 0), (0, pad - MLA_NOPE)))
    wv = jnp.pad(wkv[:, :, MLA_NOPE:], ((0, 0), (0, 0), (0, pad - MLA_V)))
    sel = np.zeros((MLA_PAD - MLA_Q_RANK - MLA_KV_RANK, MLA_WIDE), np.float32)
    for h in range(MLA_HEADS):
        for r in range(MLA_ROPE):
            sel[r, h * pad + MLA_NOPE + r] = 1.0
    tile_g = lambda g: jnp.tile(jnp.pad(g, (0, pad - MLA_QK)), MLA_HEADS)[None, :]
    flat = lambda w: w.reshape(w.shape[0], MLA_WIDE).astype(_BF16)
    return flat(wq), flat(wk), flat(wv), jnp.asarray(sel), tile_g(q_head_g), tile_g(k_head_g)


def _mla_prep(mla_in, q_norm_g, kv_norm_g, weights, seq):
    b, lt, _ = mla_in.shape
    nt = lt // TOK
    wq, wk, wv, sel, qg, kg = weights
    cos, sin = (jnp.asarray(t) for t in _rope_tables(lt - seq, seq))
    whole = lambda a: pl.BlockSpec(a.shape, lambda i, j: (0,) * a.ndim)
    tab = pl.BlockSpec((TOK, MLA_HEAD_PAD), lambda i, j: (j, 0))
    hd = lambda f: pl.BlockSpec((1, MLA_HEADS, TOK, MLA_HEAD_PAD), f)
    qn, kvn = q_norm_g[None, :], kv_norm_g[None, :]
    shp = lambda t: jax.ShapeDtypeStruct((b, MLA_HEADS, t, MLA_HEAD_PAD), _BF16)
    return pl.pallas_call(
        _mla_prep_kernel,
        grid=(b, nt),
        in_specs=[pl.BlockSpec((1, TOK, MLA_PAD), lambda i, j: (i, j, 0)), whole(qn), whole(kvn),
                  whole(wq), whole(wk), whole(wv), whole(sel), whole(qg), whole(kg), tab, tab],
        out_specs=[hd(lambda i, j: (i, 0, 0, 0)), hd(lambda i, j: (i, 0, jnp.maximum(j, 1) - 1, 0)),
                   hd(lambda i, j: (i, 0, j, 0)), hd(lambda i, j: (i, 0, j, 0))],
        out_shape=[shp(TOK), shp(lt - TOK), shp(lt), shp(lt)],
        compiler_params=_cparams(2, ("parallel", "arbitrary")),
        name="mla_prep",
    )(mla_in, qn, kvn, wq, wk, wv, sel, qg, kg, cos, sin)


def _attn_kernel(q_ref, k_ref, v_ref, o_ref):
    n_keys = k_ref.shape[2]
    starts = list(range(0, n_keys, ATTN_TK))
    heads = range(2)

    def scores(h, lo):
        hi = min(lo + ATTN_TK, n_keys)
        return lax.dot_general(q_ref[0, h], k_ref[0, h, lo:hi, :], _NT, preferred_element_type=_F32)

    nxt = [scores(h, starts[0]) for h in heads]
    m = [None, None]
    acc = [None, None]
    for n, lo in enumerate(starts):
        cur = nxt
        if n + 1 < len(starts):
            nxt = [scores(h, starts[n + 1]) for h in heads]
        hi = min(lo + ATTN_TK, n_keys)
        for h in heads:
            s = cur[h]
            m_blk = jnp.max(s, axis=-1, keepdims=True)
            m_new = m_blk if n == 0 else jnp.maximum(m[h], m_blk)
            p = jnp.exp2(s - m_new).astype(_BF16)
            pv = jnp.dot(p, v_ref[0, h, lo:hi, :], preferred_element_type=_F32)
            acc[h] = pv if n == 0 else acc[h] * jnp.exp2(m[h] - m_new) + pv
            m[h] = m_new
    o_ref[0] = jnp.concatenate([a[:, :MLA_V] / a[:, MLA_V:MLA_V + 1] for a in acc], axis=1)


def _attention(q, k, v, n_keys, tq):
    b, h, t, w = q.shape
    return pl.pallas_call(
        _attn_kernel,
        grid=(b, h // 2, t // tq),
        in_specs=[pl.BlockSpec((1, 2, tq, w), lambda i, j, l: (i, j, l, 0)),
                  pl.BlockSpec((1, 2, n_keys, w), lambda i, j, l: (i, j, 0, 0)),
                  pl.BlockSpec((1, 2, n_keys, w), lambda i, j, l: (i, j, 0, 0))],
        out_specs=pl.BlockSpec((1, tq, 2 * MLA_V), lambda i, j, l: (i, l, j)),
        out_shape=jax.ShapeDtypeStruct((b, t, h * MLA_V), _F32),
        compiler_params=_cparams(3),
        name="mla_attention",
    )(q, k, v)


HY_BLK = 256
HY_LO = 128
HY_CB = 16


def _hy_prep_kernel(xp_ref, x_ref, xn_ref, cw_ref, cb_ref, x0_ref, z_ref, xe_ref):
    j = pl.program_id(1)
    nt = pl.num_programs(1)
    pv = (j >= 2).astype(_F32)
    nv = jnp.logical_and(j >= 1, j < nt - 1).astype(_F32)
    xe_ref[0:SUBLANES] = xp_ref[0] * pv
    xe_ref[SUBLANES:SUBLANES + TOK] = x_ref[0]
    xe_ref[SUBLANES + TOK:] = xn_ref[0] * nv
    cw = cw_ref[...]
    u = (xe_ref[SUBLANES - 1:SUBLANES - 1 + TOK] * cw[0:1] + x_ref[0] * cw[1:2]
         + xe_ref[SUBLANES + 1:SUBLANES + 1 + TOK] * cw[2:3] + cb_ref[...])
    x0_ref[0] = u[:, :HY_CH]
    z_ref[0] = u[:, 2 * HY_CH:] * u[:, HY_CH:2 * HY_CH]


def _hy_prep(hy_in, conv_w, conv_b):
    b, lt, _ = hy_in.shape
    nh = TOK // SUBLANES
    last = lt // SUBLANES - 1
    whole = lambda a: pl.BlockSpec(a.shape, lambda i, j: (0,) * a.ndim)
    row = lambda n: pl.BlockSpec((1, TOK, n), lambda i, j: (i, j, 0))
    cb = conv_b[None, :]
    return pl.pallas_call(
        _hy_prep_kernel,
        grid=(b, lt // TOK),
        in_specs=[pl.BlockSpec((1, SUBLANES, HY_IN), lambda i, j: (i, jnp.maximum(j * nh - 1, 0), 0)),
                  row(HY_IN),
                  pl.BlockSpec((1, SUBLANES, HY_IN), lambda i, j: (i, jnp.minimum((j + 1) * nh, last), 0)),
                  whole(conv_w), whole(cb)],
        out_specs=[row(HY_CH), row(HY_CH)],
        out_shape=[jax.ShapeDtypeStruct((b, lt, HY_CH), _F32)] * 2,
        scratch_shapes=[pltpu.VMEM((TOK + 2 * SUBLANES, HY_IN), _F32)],
        compiler_params=_cparams(2),
        name="hyena_prep",
    )(hy_in, hy_in, hy_in, conv_w, cb)


def _hy_filter_consts(l):
    def emb(t):
        t = t.astype(np.float64)
        t_norm = t / max(l - 1, 1)
        bands = np.linspace(1e-4, HY_BANDS - 1, HY_BANDS)
        ang = 2.0 * math.pi * t[:, None] * bands[None, :] / l
        z = np.concatenate([t_norm[:, None], np.cos(ang), np.sin(ang)], axis=-1)
        return np.pad(z, ((0, 0), (0, LANES - HY_EMB))), t_norm[:, None]
    r = np.arange(l)
    e_rev, tn_rev = emb(l - 1 - r)
    e_sh, tn_sh = emb(r + 1)
    deltas = np.abs(np.linspace(HY_MIN_DECAY, HY_MAX_DECAY, HY_CH))[None, :]
    f = lambda a: np.asarray(a, np.float32)
    return f(e_rev), f(e_sh), f(tn_rev), f(tn_sh), f(deltas)


def _hy_filter_kernel(er_ref, es_ref, tr_ref, ts_ref, dl_ref, w1_ref, b1_ref, w2_ref, b2_ref, w3_ref, b3_ref, o_ref):
    l = er_ref.shape[0]

    def mlp(e, col):
        h = jnp.sin(jnp.dot(e, w1_ref[...], precision=_HI, preferred_element_type=_F32) + b1_ref[...])
        h = jnp.sin(jnp.dot(h, w2_ref[...], precision=_HI, preferred_element_type=_F32) + b2_ref[...])
        return (jnp.dot(h, w3_ref[:, col * HY_CH:(col + 1) * HY_CH], precision=_HI, preferred_element_type=_F32)
                + b3_ref[:, col * HY_CH:(col + 1) * HY_CH])

    hf = mlp(er_ref[...], 0) * jnp.exp(-tr_ref[...] * dl_ref[...])
    hb = mlp(es_ref[...], 1) * jnp.exp(-ts_ref[...] * dl_ref[...])
    row = lax.broadcasted_iota(jnp.int32, hb.shape, 0)
    hb = jnp.where(row < l - 1, hb, 0.0)
    norm = jnp.sum(jnp.abs(hf), axis=0, keepdims=True) + jnp.sum(jnp.abs(hb), axis=0, keepdims=True)
    o_ref[...] = jnp.transpose(jnp.concatenate([hf, hb], axis=0) / norm)


def _hy_filter(l, w1, b1, w2, b2, w3, b3):
    consts = [jnp.asarray(a) for a in _hy_filter_consts(l)]
    w1p = jnp.pad(w1, ((0, LANES - HY_EMB), (0, 0)))
    args = consts + [w1p, b1[None, :], w2, b2[None, :], w3, b3[None, :]]
    return pl.pallas_call(
        _hy_filter_kernel,
        out_shape=jax.ShapeDtypeStruct((HY_CH, 2 * l), _F32),
        compiler_params=pltpu.CompilerParams(vmem_limit_bytes=VMEM_LIMIT_BYTES),
        name="hyena_filter",
    )(*args)


def _hy_conv_kernel(f_ref, z_ref, y_ref, g_ref, *, nblk, nb):
    cols = z_ref.shape[2]
    lane = lax.broadcasted_iota(jnp.int32, (HY_BLK, cols), 1)

    def channel(ch, carry):
        base = pltpu.roll(jnp.broadcast_to(f_ref[ch], (SUBLANES, f_ref.shape[2])), 1, 1, stride=1, stride_axis=0)
        for a in range(0, HY_LO // SUBLANES, 2):
            lo = base if a == 0 else pltpu.roll(base, SUBLANES * a, 1)
            hi = pltpu.roll(base, SUBLANES * (a + 1), 1)
            g_ref[a * SUBLANES:(a + 2) * SUBLANES, :] = jnp.concatenate([lo, hi], axis=0).astype(_BF16)
        z = z_ref[ch]
        y = jnp.zeros((HY_BLK, cols), _F32)
        for d in range(-(nblk - 1), nblk):
            o = HY_BLK * (nblk - d)
            t_d = jnp.concatenate([g_ref[:, o:o + HY_BLK], g_ref[:, o - HY_LO:o - HY_LO + HY_BLK]], axis=0)
            r = jnp.dot(t_d, z, preferred_element_type=_F32)
            if d > 0:
                r = jnp.where(lane >= nb * d, pltpu.roll(r, nb * d, 1), 0.0)
            elif d < 0:
                r = jnp.where(lane < nb * (nblk + d), pltpu.roll(r, cols + nb * d, 1), 0.0)
            y = y + r
        y_ref[ch] = y
        return carry

    lax.fori_loop(0, f_ref.shape[0], channel, 0)


def _hy_conv(fline, z):
    b, l, c = z.shape
    nblk = l // HY_BLK
    cols = max(nblk * b, LANES)
    zall = z.reshape(b, nblk, HY_BLK, c).transpose(3, 2, 1, 0).reshape(c, HY_BLK, nblk * b)
    zall = jnp.pad(zall, ((0, 0), (0, 0), (0, cols - nblk * b))).astype(_BF16)
    y = pl.pallas_call(
        functools.partial(_hy_conv_kernel, nblk=nblk, nb=b),
        grid=(c // HY_CB,),
        in_specs=[pl.BlockSpec((HY_CB, 1, 2 * l), lambda i: (i, 0, 0)),
                  pl.BlockSpec((HY_CB, HY_BLK, cols), lambda i: (i, 0, 0))],
        out_specs=pl.BlockSpec((HY_CB, HY_BLK, cols), lambda i: (i, 0, 0)),
        out_shape=jax.ShapeDtypeStruct((c, HY_BLK, cols), _F32),
        scratch_shapes=[pltpu.VMEM((HY_LO, 2 * l), _BF16)],
        compiler_params=_cparams(1),
        name="hyena_conv",
    )(fline.reshape(c, 1, 2 * l), zall)
    return y[:, :, :nblk * b].reshape(c, HY_BLK, nblk, b).transpose(3, 2, 1, 0).reshape(b, l, c)


def _out_proj_kernel(x_ref, of_ref, ob_ref, z_ref, gn_ref, hones_ref, mla_ref, hx_ref, hz_ref, hy_ref, hd_ref,
                     ga_ref, w_ref, o_ref):
    o = of_ref[0] + ob_ref[0]
    ms = jnp.dot(o * o, hones_ref[...], precision=_HI, preferred_element_type=_F32) * (1.0 / GDN_DV)
    gdn = o * lax.rsqrt(ms + EPS) * gn_ref[...] * _silu(z_ref[0])
    hy = hx_ref[0] * (hy_ref[0] + hz_ref[0] * hd_ref[...])
    mix = jnp.concatenate([gdn, mla_ref[0], hy], axis=-1).astype(_BF16)
    y = jnp.dot(mix, w_ref[...], preferred_element_type=_F32)
    o_ref[0] = x_ref[0] + ga_ref[0] * y


def _out_proj(x, o_f, o_b, z, gn_row, mla, hx0, hz, hy, hd_row, ga, w_out, toff):
    b, l, d = x.shape
    hones = jnp.asarray(_gdn_consts()[2])
    row = lambda n: pl.BlockSpec((1, TOK, n), lambda i, j: (i, j, 0))
    rowc = lambda n: pl.BlockSpec((1, TOK, n), lambda i, j: (i, j + toff, 0))
    whole = lambda a: pl.BlockSpec(a.shape, lambda i, j: (0,) * a.ndim)
    return pl.pallas_call(
        _out_proj_kernel,
        grid=(b, l // TOK),
        in_specs=[row(d), row(GDN_WIDTH), row(GDN_WIDTH), rowc(GDN_WIDTH), whole(gn_row), whole(hones),
                  row(MLA_WIDTH), rowc(HY_WIDTH), rowc(HY_WIDTH), row(HY_WIDTH), whole(hd_row),
                  pl.BlockSpec((1, 1, d), lambda i, j: (i, 0, 0)),
                  pl.BlockSpec((d, d), lambda i, j: (0, 0))],
        out_specs=row(d),
        out_shape=jax.ShapeDtypeStruct((b, l, d), _F32),
        compiler_params=_cparams(2),
        name="out_proj",
    )(x, o_f, o_b, z, gn_row, hones, mla, hx0, hz, hy, hd_row, ga, w_out)


def _ffn_kernel(xp_ref, x_ref, xn_ref, g_ref, sf_ref, cf_ref, gf_ref, wup_ref, cw_ref, cb_ref, wdn_ref,
                o_ref, h_ref, up_ref, uv_ref, act_ref, *, tl):
    i = pl.program_id(1)
    nt = pl.num_programs(1)
    g, sf, cf = g_ref[...], sf_ref[0], cf_ref[0]
    pv = (i > 0).astype(_F32)
    nv = (i < nt - 1).astype(_F32)
    h_ref[0:HALO] = (_norm_mod(xp_ref[0], g, sf, cf) * pv).astype(_BF16)
    h_ref[HALO:HALO + tl] = _norm_mod(x_ref[0], g, sf, cf).astype(_BF16)
    h_ref[HALO + tl:] = (_norm_mod(xn_ref[0], g, sf, cf) * nv).astype(_BF16)

    def up_proj(c, slot):
        lo = pl.multiple_of(c * FFN_CHUNK, FFN_CHUNK)
        up_ref[slot] = jnp.dot(h_ref[...], wup_ref[:, pl.ds(lo, FFN_CHUNK)], preferred_element_type=_F32)
        uv_ref[slot] = jnp.dot(h_ref[HALO:HALO + tl], wup_ref[:, pl.ds(D_FF + lo, FFN_CHUNK)],
                               preferred_element_type=_F32)

    def gate(c, slot):
        lo = pl.multiple_of(c * FFN_CHUNK, FFN_CHUNK)
        cw = cw_ref[:, pl.ds(lo, FFN_CHUNK)]
        cb = cb_ref[:, pl.ds(lo, FFN_CHUNK)]
        gt = (up_ref[slot, HALO - 1:HALO - 1 + tl] * cw[0:1] + up_ref[slot, HALO:HALO + tl] * cw[1:2]
              + up_ref[slot, HALO + 1:HALO + 1 + tl] * cw[2:3] + cb)
        act_ref[:, pl.ds(lo, FFN_CHUNK)] = (_silu(gt) * uv_ref[slot]).astype(_BF16)

    n_chunks = D_FF // FFN_CHUNK
    up_proj(0, 0)

    def pair(it, carry):
        c = 2 * it
        up_proj(c + 1, 1)
        gate(c, 0)
        up_proj(c + 2, 0)
        gate(c + 1, 1)
        return carry

    lax.fori_loop(0, (n_chunks - 1) // 2, pair, 0)
    assert n_chunks % 2 == 1
    gate(n_chunks - 1, 0)
    y = jnp.dot(act_ref[...], wdn_ref[...], preferred_element_type=_F32)
    o_ref[0] = x_ref[0] + gf_ref[0] * y


def _ffn(x, g, sf, cf, gf, w_up, conv_w, conv_b, w_down, tl):
    b, l, d = x.shape
    nh = tl // HALO
    last = l // HALO - 1
    vec = pl.BlockSpec((1, 1, d), lambda i, j: (i, 0, 0))
    whole = lambda a: pl.BlockSpec(a.shape, lambda i, j: (0,) * a.ndim)
    return pl.pallas_call(
        functools.partial(_ffn_kernel, tl=tl),
        grid=(b, l // tl),
        in_specs=[
            pl.BlockSpec((1, HALO, d), lambda i, j: (i, jnp.maximum(j * nh - 1, 0), 0)),
            pl.BlockSpec((1, tl, d), lambda i, j: (i, j, 0)),
            pl.BlockSpec((1, HALO, d), lambda i, j: (i, jnp.minimum((j + 1) * nh, last), 0)),
            pl.BlockSpec((1, d), lambda i, j: (0, 0)),
            vec, vec, vec,
            whole(w_up), whole(conv_w), whole(conv_b), whole(w_down),
        ],
        out_specs=pl.BlockSpec((1, tl, d), lambda i, j: (i, j, 0)),
        out_shape=jax.ShapeDtypeStruct((b, l, d), _F32),
        scratch_shapes=[
            pltpu.VMEM((tl + 2 * HALO, d), _BF16),
            pltpu.VMEM((2, tl + 2 * HALO, FFN_CHUNK), _F32),
            pltpu.VMEM((2, tl, FFN_CHUNK), _F32),
            pltpu.VMEM((tl, D_FF), _BF16),
        ],
        compiler_params=_cparams(2),
        name="conv_ffn",
    )(x, x, x, g, sf, cf, gf, w_up, conv_w, conv_b, w_down)


def rms_norm(x, g):
    xf = x.astype(jnp.float32)
    y = xf * lax.rsqrt(jnp.mean(xf * xf, axis=-1, keepdims=True) + EPS)
    return (y * g.astype(jnp.float32)).astype(x.dtype)


def dwconv3(x, w, b=None):
    xp = jnp.pad(x, ((0, 0), (1, 1), (0, 0)))
    y = xp[:, :-2] * w[0] + xp[:, 1:-1] * w[1] + xp[:, 2:] * w[2]
    return y if b is None else y + b


def axial_rope(l):
    rows = l // GRID_W
    row = jnp.repeat(jnp.arange(rows, dtype=jnp.float32), GRID_W)
    col = jnp.tile(jnp.arange(GRID_W, dtype=jnp.float32), rows)
    inv = ROPE_BASE ** (-jnp.arange(0, ROPE_AXIS, 2, dtype=jnp.float32) / ROPE_AXIS)
    ang = jnp.concatenate([row[:, None] * inv, col[:, None] * inv], axis=-1)
    return jnp.cos(ang), jnp.sin(ang)


def apply_rope(x, cos, sin):
    half = x.shape[-1] // 2
    x1, x2 = x[..., :half], x[..., half:]
    cos, sin = cos[None, :, None, :], sin[None, :, None, :]
    return jnp.concatenate([x1 * cos - x2 * sin, x1 * sin + x2 * cos], axis=-1)


def mla_heads(p, q_norm_g, w_uq, kv_norm_g, w_ukv, q_head_g, k_head_g, rope):
    b, l, _ = p.shape
    c_q = rms_norm(p[..., :MLA_Q_RANK], q_norm_g)
    c_kv = rms_norm(p[..., MLA_Q_RANK:MLA_Q_RANK + MLA_KV_RANK], kv_norm_g)
    k_rope = p[..., MLA_Q_RANK + MLA_KV_RANK:MLA_IN]
    q = (c_q @ w_uq).reshape(b, l, MLA_HEADS, MLA_QK)
    kv = (c_kv @ w_ukv).reshape(b, l, MLA_HEADS, MLA_NOPE + MLA_V)
    k = jnp.concatenate([kv[..., :MLA_NOPE],
                         jnp.broadcast_to(k_rope[:, :, None, :], (b, l, MLA_HEADS, MLA_ROPE))], axis=-1)
    v = kv[..., MLA_NOPE:]
    q = rms_norm(q, q_head_g)
    k = rms_norm(k, k_head_g)
    if rope is not None:
        cos, sin = rope
        q = jnp.concatenate([q[..., :MLA_NOPE], apply_rope(q[..., MLA_NOPE:], cos, sin)], axis=-1)
        k = jnp.concatenate([k[..., :MLA_NOPE], apply_rope(k[..., MLA_NOPE:], cos, sin)], axis=-1)
    return q, k, v


def softmax_attend(q, k, v, scale):
    s = jnp.einsum('bqhd,bkhd->bhqk', q, k).astype(jnp.float32) * scale
    p = jax.nn.softmax(s, axis=-1).astype(v.dtype)
    return jnp.einsum('bhqk,bkhd->bqhd', p, v)


def mla_mixer(p_ctx, p_lat, q_norm_g, w_uq, kv_norm_g, w_ukv, q_head_g, k_head_g, rope, with_ctx):
    qc, kc, vc = mla_heads(p_ctx, q_norm_g, w_uq, kv_norm_g, w_ukv, q_head_g, k_head_g, None)
    ql, kl, vl = mla_heads(p_lat, q_norm_g, w_uq, kv_norm_g, w_ukv, q_head_g, k_head_g, rope)
    scale = MLA_QK ** -0.5
    k_all = jnp.concatenate([kl, kc], axis=1)
    v_all = jnp.concatenate([vl, vc], axis=1)
    b, l = ql.shape[0], ql.shape[1]
    nb = l // ATTN_BLOCK
    q_blocks = jnp.moveaxis(ql.reshape(b, nb, ATTN_BLOCK, MLA_HEADS, MLA_QK), 1, 0)
    o_lat = lax.map(lambda qb: softmax_attend(qb, k_all, v_all, scale), q_blocks)
    o_lat = jnp.moveaxis(o_lat, 0, 1).reshape(b, l, MLA_WIDTH)
    if not with_ctx:
        return jnp.zeros((b, qc.shape[1], MLA_WIDTH), o_lat.dtype), o_lat
    o_ctx = softmax_attend(qc, kc, vc, scale).reshape(b, qc.shape[1], MLA_WIDTH)
    return o_ctx, o_lat


def hyena_filter(l, w1, b1, w2, b2, w3, b3):
    t = jnp.arange(l, dtype=jnp.float32)
    t_norm = t / max(l - 1, 1)
    bands = jnp.linspace(1e-4, HY_BANDS - 1, HY_BANDS, dtype=jnp.float32)
    ang = 2.0 * math.pi * t[:, None] * bands[None, :] / l
    z = jnp.concatenate([t_norm[:, None], jnp.cos(ang), jnp.sin(ang)], axis=-1)
    h = jnp.sin(z @ w1 + b1)
    h = jnp.sin(h @ w2 + b2)
    h = (h @ w3 + b3).reshape(l, 2, HY_CH).astype(jnp.float32)
    deltas = jnp.abs(jnp.linspace(HY_MIN_DECAY, HY_MAX_DECAY, HY_CH, dtype=jnp.float32))
    h = h * jnp.exp(-t_norm[:, None, None] * deltas)
    buf = jnp.concatenate([h[:, 0], jnp.zeros((1, HY_CH), jnp.float32), h[:0:-1, 1]], axis=0)
    return buf / jnp.sum(jnp.abs(buf), axis=0, keepdims=True)


def hyena_mixer(p, conv_w, conv_b, w1, b1, w2, b2, w3, b3, d_skip):
    b, l, _ = p.shape
    u = dwconv3(p, conv_w, conv_b)
    x0, x1, v = u[..., :HY_CH], u[..., HY_CH:2 * HY_CH], u[..., 2 * HY_CH:]
    z = (v * x1).astype(jnp.float32)
    buf = hyena_filter(l, w1, b1, w2, b2, w3, b3)
    zf = jnp.fft.rfft(z, n=2 * l, axis=1)
    hf = jnp.fft.rfft(buf, n=2 * l, axis=0)
    y = jnp.fft.irfft(zf * hf[None], n=2 * l, axis=1)[:, :l] + z * d_skip
    return x0 * y.astype(x0.dtype)


def _pad_row(v, n=LANES):
    v = v.reshape(1, -1)
    return jnp.pad(v, ((0, 0), (0, n - v.shape[1])))


def kernel(x, c, ctx, c_ctx, ada_w, ada_b, mix_norm_g, w_in, gdn_conv_w, gdn_a_log, gdn_dt_bias, gdn_norm_g, mla_q_norm_g, mla_w_uq, mla_kv_norm_g, mla_w_ukv, mla_q_head_g, mla_k_head_g, hy_conv_w, hy_conv_b, hy_w1, hy_b1, hy_w2, hy_b2, hy_w3, hy_b3, hy_d, w_out, ffn_norm_g, ffn_w_up, ffn_conv_w, ffn_conv_b, ffn_w_down):
    bsz, seq, d = x.shape
    n_ctx = ctx.shape[1]
    assert n_ctx == TOK and seq % TOK == 0 and bsz < ADA_ROWS
    cond = jnp.concatenate([c, c_ctx[None, :], jnp.zeros((ADA_ROWS - bsz - 1, d), c.dtype)], axis=0)
    for i in range(DEPTH):
        last = i == DEPTH - 1
        mod = _ada_mod(cond, ada_w[i], ada_b[i])
        mod_lat = mod[:bsz, None, :]
        mod_ctx = mod[bsz][None, None, :]
        sa_l, ca_l, ga_l, sf_l, cf_l, gf_l = jnp.split(mod_lat, 6, axis=-1)
        sa_c, ca_c, ga_c, sf_c, cf_c, gf_c = (jnp.broadcast_to(t, (bsz, 1, d)) for t in jnp.split(mod_ctx, 6, axis=-1))

        w_in_p = _pad_w_in(w_in[i])
        g_mix = mix_norm_g[i][None, :]
        qkv, z, ab, mla_in, hy_in = _in_proj(x, ctx, g_mix, sa_l, ca_l, sa_c[:1], ca_c[:1], w_in_p)

        q, k, v, gf, gb, bf, bb = _gdn_prep(qkv, ab, gdn_conv_w[i], _pad_row(gdn_a_log[i]), _pad_row(gdn_dt_bias[i]))
        oc_f, oc_b, ol_f, ol_b = _gdn_scan(q, k, v, gf, gb, bf, bb)

        mla_w = _mla_weights(mla_w_uq[i], mla_w_ukv[i], mla_q_head_g[i], mla_k_head_g[i])
        q_ctx, q_lat, k_all, v_all = _mla_prep(mla_in, mla_q_norm_g[i], mla_kv_norm_g[i], mla_w, seq)
        mla_l = _attention(q_lat, k_all, v_all, n_ctx + seq, ATTN_TQ)
        hy_x0, hy_z = _hy_prep(hy_in, hy_conv_w[i], hy_conv_b[i])
        hy_mlp = (hy_w1[i], hy_b1[i], hy_w2[i], hy_b2[i], hy_w3[i], hy_b3[i])
        hy_l = _hy_conv(_hy_filter(seq, *hy_mlp), hy_z[:, TOK:])
        hd_row = hy_d[i][None, :]

        w_out_b = w_out[i].astype(_BF16)
        w_up_b = ffn_w_up[i].astype(_BF16)
        w_dn_b = ffn_w_down[i].astype(_BF16)
        g_ffn = ffn_norm_g[i][None, :]
        cb = ffn_conv_b[i][None, :]
        gn_row = jnp.tile(gdn_norm_g[i], GDN_HEADS)[None, :]

        x = _out_proj(x, ol_f, ol_b, z, gn_row, mla_l, hy_x0, hy_z, hy_l, hd_row, ga_l, w_out_b, 1)
        x = _ffn(x, g_ffn, sf_l, cf_l, gf_l, w_up_b, ffn_conv_w[i], cb, w_dn_b, 512)

        if not last:
            mla_c = _attention(q_ctx, k_all, v_all, n_ctx, TOK)
            hy_c = _hy_conv(_hy_filter(n_ctx, *hy_mlp), hy_z[:, :TOK])
            ctx = _out_proj(ctx, oc_f, oc_b, z, gn_row, mla_c, hy_x0, hy_z, hy_c, hd_row, ga_c, w_out_b, 0)
            ctx = _ffn(ctx, g_ffn, sf_c, cf_c, gf_c, w_up_b, ffn_conv_w[i], cb, w_dn_b, TOK)
    return x
```

```python
import functools
import math

import jax
import jax.numpy as jnp
import numpy as np
from jax import lax
from jax.experimental import pallas as pl
from jax.experimental.pallas import tpu as pltpu

D_MODEL = 1024
DEPTH = 2
GRID_W = 64
EPS = 1e-6

GDN_HEADS = 6
GDN_DK = 64
GDN_DV = 64
GDN_CHUNK = 64

MLA_HEADS = 6
MLA_Q_RANK = 256
MLA_KV_RANK = 128
MLA_NOPE = 64
MLA_ROPE = 32
MLA_V = 64
MLA_QK = MLA_NOPE + MLA_ROPE
ATTN_BLOCK = 128
ROPE_BASE = 10000.0
ROPE_AXIS = MLA_ROPE // 2

HY_CH = 256
HY_BANDS = 16
HY_EMB = 1 + 2 * HY_BANDS
HY_HIDDEN = 64
HY_TARGET = 1e-2
HY_FAST_DECAY_PCT = 0.3
HY_SLOW_DECAY_PCT = 1.5
HY_MAX_DECAY = math.log(HY_TARGET) / HY_FAST_DECAY_PCT
HY_MIN_DECAY = math.log(HY_TARGET) / HY_SLOW_DECAY_PCT

D_FF = 2816

GDN_WIDTH = GDN_HEADS * GDN_DV
MLA_WIDTH = MLA_HEADS * MLA_V
HY_WIDTH = HY_CH
GDN_QKV = GDN_HEADS * (2 * GDN_DK + GDN_DV)
GDN_IN = GDN_QKV + GDN_WIDTH + 4 * GDN_HEADS
MLA_IN = MLA_Q_RANK + MLA_KV_RANK + MLA_ROPE
HY_IN = 3 * HY_CH

LANES = 128
SUBLANES = 8
SUBLANES_BF16 = 16
VMEM_LIMIT_BYTES = 56 * 1024 * 1024

TOK = 256
GDN_PAIRS = GDN_HEADS // 2
GDN_BB = 2
N_GATE = 4 * GDN_HEADS

AB_PAD = LANES
MLA_PAD = 512
IN_GROUPS = (GDN_QKV, GDN_WIDTH, AB_PAD, MLA_PAD, HY_IN)
IN_TOTAL = sum(IN_GROUPS)

FFN_CHUNK = 256
HALO = SUBLANES_BF16

_BF16 = jnp.bfloat16
_F32 = jnp.float32
_HI = lax.Precision.HIGHEST
_NT = (((1,), (1,)), ((), ()))
_TN = (((0,), (0,)), ((), ()))


def _cparams(n_axes, sem=None):
    return pltpu.CompilerParams(
        dimension_semantics=sem or ("parallel",) * n_axes, vmem_limit_bytes=VMEM_LIMIT_BYTES)


def _norm_mod(x, g, shift, scale):
    ms = jnp.mean(x * x, axis=-1, keepdims=True)
    y = x * lax.rsqrt(ms + EPS) * g
    return y * (1.0 + scale) + shift


def _silu(x):
    return x * jax.nn.sigmoid(x)


ADA_ROWS = 16
ADA_TN = 1024


def _ada_kernel(c_ref, w_ref, b_ref, o_ref):
    o_ref[...] = jnp.dot(_silu(c_ref[...]), w_ref[...], precision=_HI, preferred_element_type=_F32) + b_ref[...]


def _ada_mod(cond, w, b):
    d, n = w.shape
    return pl.pallas_call(
        _ada_kernel,
        grid=(n // ADA_TN,),
        in_specs=[pl.BlockSpec((ADA_ROWS, d), lambda j: (0, 0)),
                  pl.BlockSpec((d, ADA_TN), lambda j: (0, j)),
                  pl.BlockSpec((1, ADA_TN), lambda j: (0, j))],
        out_specs=pl.BlockSpec((ADA_ROWS, ADA_TN), lambda j: (0, j)),
        out_shape=jax.ShapeDtypeStruct((ADA_ROWS, n), _F32),
        compiler_params=_cparams(1),
        name="ada_mod",
    )(cond, w, b[None, :])


def _in_proj_kernel(x_ref, c_ref, g_ref, sl_ref, cl_ref, sc_ref, cc_ref, w_ref, *out_refs):
    is_ctx = pl.program_id(1) == 0
    x = jnp.where(is_ctx, c_ref[0], x_ref[0])
    shift = jnp.where(is_ctx, sc_ref[0], sl_ref[0])
    scale = jnp.where(is_ctx, cc_ref[0], cl_ref[0])
    h = _norm_mod(x, g_ref[...], shift, scale)
    p = jnp.dot(h.astype(_BF16), w_ref[...], preferred_element_type=_F32)
    off = 0
    for o_ref, n in zip(out_refs, IN_GROUPS):
        o_ref[0] = p[:, off:off + n]
        off += n


def _in_proj(x, ctx, g, shift_l, scale_l, shift_c, scale_c, w_pad):
    b, l, d = x.shape
    nt = 1 + l // TOK
    vec_l = pl.BlockSpec((1, 1, d), lambda i, j: (i, 0, 0))
    vec_c = pl.BlockSpec((1, 1, d), lambda i, j: (0, 0, 0))
    return pl.pallas_call(
        _in_proj_kernel,
        grid=(b, nt),
        in_specs=[
            pl.BlockSpec((1, TOK, d), lambda i, j: (i, jnp.maximum(j - 1, 0), 0)),
            pl.BlockSpec((1, TOK, d), lambda i, j: (i, 0, 0)),
            pl.BlockSpec((1, d), lambda i, j: (0, 0)),
            vec_l, vec_l, vec_c, vec_c,
            pl.BlockSpec((d, IN_TOTAL), lambda i, j: (0, 0)),
        ],
        out_specs=[pl.BlockSpec((1, TOK, n), lambda i, j: (i, j, 0)) for n in IN_GROUPS],
        out_shape=[jax.ShapeDtypeStruct((b, nt * TOK, n), _F32) for n in IN_GROUPS],
        compiler_params=_cparams(2),
        name="in_proj",
    )(x, ctx, g, shift_l, scale_l, shift_c, scale_c, w_pad)


def _pad_w_in(w_in):
    s1 = GDN_QKV + GDN_WIDTH
    s2 = GDN_IN
    s3 = GDN_IN + MLA_IN
    d = w_in.shape[0]
    z = lambda n: jnp.zeros((d, n), w_in.dtype)
    parts = [w_in[:, :s1], w_in[:, s1:s2], z(AB_PAD - N_GATE),
             w_in[:, s2:s3], z(MLA_PAD - MLA_IN), w_in[:, s3:]]
    return jnp.concatenate(parts, axis=1).astype(_BF16)


def _gdn_consts():
    r = np.arange(TOK)
    same = (r[:, None] // GDN_CHUNK) == (r[None, :] // GDN_CHUNK)
    tril = (same & (r[None, :] <= r[:, None])).astype(np.float32)
    triu = (same & (r[None, :] >= r[:, None])).astype(np.float32)
    c = np.arange(GDN_WIDTH)
    head_ones = (c[:, None] // GDN_DK == c[None, :] // GDN_DK).astype(np.float32)
    expand = np.zeros((LANES, 4 * GDN_WIDTH), np.float32)
    for k in range(4):
        for h in range(GDN_HEADS):
            expand[k * GDN_HEADS + h, k * GDN_WIDTH + h * GDN_DK:k * GDN_WIDTH + (h + 1) * GDN_DK] = 1.0
    return tril, triu, head_ones, expand


def _gdn_prep_consts():
    tril, triu, head_ones, expand = _gdn_consts()
    b = lambda a: jnp.asarray(a, _BF16)
    return b(tril), b(triu), b(np.concatenate([head_ones] * 2, axis=0)), b(np.concatenate([expand] * 3, axis=0))


def _split_bf16(x, n):
    terms = []
    for _ in range(n):
        t = x.astype(_BF16)
        terms.append(t)
        x = x - t.astype(_F32)
    return terms


def _sum_terms(y, n):
    w = y.shape[1] // n
    out = y[:, :w]
    for t in range(1, n):
        out = out + y[:, t * w:(t + 1) * w]
    return out


def _gdn_prep_kernel(xp_ref, x_ref, xn_ref, ab_ref, cw_ref, alog_ref, dt_ref, tril_ref, triu_ref, hones_ref,
                     exp_ref, q_ref, k_ref, v_ref, gf_ref, gb_ref, bf_ref, bb_ref, xe_ref):
    j = pl.program_id(1)
    nt = pl.num_programs(1)
    pv = (j >= 2).astype(_F32)
    nv = jnp.logical_and(j >= 1, j < nt - 1).astype(_F32)
    xe_ref[0:SUBLANES] = xp_ref[0] * pv
    xe_ref[SUBLANES:SUBLANES + TOK] = x_ref[0]
    xe_ref[SUBLANES + TOK:] = xn_ref[0] * nv
    cw = cw_ref[...]
    y = (xe_ref[SUBLANES - 1:SUBLANES - 1 + TOK] * cw[0:1] + x_ref[0] * cw[1:2]
         + xe_ref[SUBLANES + 1:SUBLANES + 1 + TOK] * cw[2:3])
    y = _silu(y)
    hk = GDN_HEADS * GDN_DK
    q, k, v = y[:, :hk], y[:, hk:2 * hk], y[:, 2 * hk:]
    sq = jnp.concatenate([q * q, k * k], axis=0)
    ss = jnp.dot(jnp.concatenate(_split_bf16(sq, 2), axis=1), hones_ref[...], preferred_element_type=_F32)
    q_ref[0] = q * lax.rsqrt(ss[:TOK] + EPS) * (GDN_DK ** -0.5)
    k_ref[0] = k * lax.rsqrt(ss[TOK:] + EPS)
    v_ref[0] = v

    ab = ab_ref[0]
    lane = lax.broadcasted_iota(jnp.int32, ab.shape, 1)
    a_in = ab + dt_ref[...]
    softplus = jnp.maximum(a_in, 0.0) + jnp.log(1.0 + jnp.exp(-jnp.abs(a_in)))
    g = jnp.where(lane < 2 * GDN_HEADS, -jnp.exp(alog_ref[...]) * softplus, 0.0)
    g3 = jnp.concatenate(_split_bf16(g, 3), axis=1)
    gc_f = _sum_terms(jnp.dot(tril_ref[...], g3, preferred_element_type=_F32), 3)
    gc_b = _sum_terms(jnp.dot(triu_ref[...], g3, preferred_element_type=_F32), 3)
    cols = jnp.where(lane < GDN_HEADS, gc_f, jnp.where(lane < 2 * GDN_HEADS, gc_b, jax.nn.sigmoid(ab)))
    wide = jnp.dot(jnp.concatenate(_split_bf16(cols, 3), axis=1), exp_ref[...],
                   preferred_element_type=_F32)
    gf_ref[0] = wide[:, 0:GDN_WIDTH]
    gb_ref[0] = wide[:, GDN_WIDTH:2 * GDN_WIDTH]
    bf_ref[0] = wide[:, 2 * GDN_WIDTH:3 * GDN_WIDTH]
    bb_ref[0] = wide[:, 3 * GDN_WIDTH:]


def _gdn_prep(qkv, ab, conv_w, a_log_row, dt_row):
    b, lt, _ = qkv.shape
    nt = lt // TOK
    nh = TOK // SUBLANES
    last = lt // SUBLANES - 1
    consts = list(_gdn_prep_consts())
    whole = lambda a: pl.BlockSpec(a.shape, lambda i, j: (0,) * a.ndim)
    row = lambda n: pl.BlockSpec((1, TOK, n), lambda i, j: (i, j, 0))
    return pl.pallas_call(
        _gdn_prep_kernel,
        grid=(b, nt),
        in_specs=[
            pl.BlockSpec((1, SUBLANES, GDN_QKV), lambda i, j: (i, jnp.maximum(j * nh - 1, 0), 0)),
            row(GDN_QKV),
            pl.BlockSpec((1, SUBLANES, GDN_QKV), lambda i, j: (i, jnp.minimum((j + 1) * nh, last), 0)),
            row(AB_PAD), whole(conv_w), whole(a_log_row), whole(dt_row),
        ] + [whole(a) for a in consts],
        out_specs=[row(GDN_WIDTH)] * 7,
        out_shape=[jax.ShapeDtypeStruct((b, lt, GDN_WIDTH), _F32)] * 7,
        scratch_shapes=[pltpu.VMEM((TOK + 2 * SUBLANES, GDN_QKV), _F32)],
        compiler_params=_cparams(2),
        name="gdn_prep",
    )(qkv, qkv, qkv, ab, conv_w, a_log_row, dt_row, *consts)


def _block_diag(z, left):
    return jnp.concatenate([jnp.where(left, z, 0.0), jnp.where(left, 0.0, z)], axis=0).astype(_BF16)


def _mm(a, b):
    return jnp.dot(a.astype(_BF16), b, preferred_element_type=_F32)


def _gdn_chunk(q, k, v, gx, bx, s, backward, masks):
    left, eye2, incl, strict, ones64, diag_blocks = masks
    c = GDN_CHUNK
    yk = _block_diag(k, left)
    qk_kk = lax.dot_general(jnp.concatenate([q, k], axis=0).astype(_BF16), yk, _NT, preferred_element_type=_F32)
    qk, kk = qk_kk[:c], qk_kk[c:]
    d0 = jnp.where(eye2, gx, 0.0)
    t_hi = d0.astype(_BF16)
    r1 = d0 - t_hi.astype(_F32)
    t_mid = r1.astype(_BF16)
    t_lo = (r1 - t_mid.astype(_F32)).astype(_BF16)
    r3 = jnp.dot(ones64, jnp.concatenate([t_hi, t_mid, t_lo], axis=1), preferred_element_type=_F32)
    r = r3[:, :LANES] + r3[:, LANES:2 * LANES] + r3[:, 2 * LANES:]
    dec = jnp.where(incl, jnp.exp(jnp.where(incl, gx - r, 0.0)), 0.0)
    a = bx * kk * jnp.where(strict, dec, 0.0)
    qk = qk * dec
    eg = jnp.exp(gx)
    p = jnp.where(eye2, 1.0, 0.0) - a
    pw = a
    pw_bd = _block_diag(pw, left)
    for _ in range(5):
        pw = _mm(pw, pw_bd)
        pw_bd = _block_diag(pw, left)
        p = p + _mm(p, pw_bd)
    rhs = jnp.concatenate([_block_diag(bx * v, left), _block_diag(bx * k * eg, left)], axis=1)
    uw = _mm(p, rhs)
    u, w = uw[:, :LANES], uw[:, LANES:]
    qd = q * eg
    tot = gx[0:1] if backward else gx[c - 1:c]
    kd = k * jnp.exp(tot - gx)
    sb = s.astype(_BF16)
    ws_qs = _mm(jnp.concatenate([w, qd], axis=0), sb)
    delta = u - ws_qs[:c]
    o = ws_qs[c:] + _mm(qk, _block_diag(delta, left))
    upd = lax.dot_general(kd.astype(_BF16), delta.astype(_BF16), _TN, preferred_element_type=_F32)
    s = s * jnp.exp(tot) + jnp.where(diag_blocks, upd, 0.0)
    return o, s


def _gdn_scan_kernel(cq, ck, cv, cgf, cbf, cgb, cbb, fq, fk, fv, fg, fb, rq, rk, rv, rg, rb,
                     ocf_ref, ocb_ref, of_ref, ob_ref, sf_ref, sb_ref):
    step = pl.program_id(1)
    is_ctx = step == 0

    @pl.when(is_ctx)
    def _():
        sf_ref[...] = jnp.zeros_like(sf_ref)
        sb_ref[...] = jnp.zeros_like(sb_ref)

    c = GDN_CHUNK
    li = lax.broadcasted_iota(jnp.int32, (c, LANES), 1)
    ri = lax.broadcasted_iota(jnp.int32, (c, LANES), 0)
    lj = li & (c - 1)
    left = li < c
    eye2 = lj == ri
    ones64 = jnp.ones((c, c), _BF16)
    r2 = lax.broadcasted_iota(jnp.int32, (LANES, LANES), 0)
    c2 = lax.broadcasted_iota(jnp.int32, (LANES, LANES), 1)
    diag_blocks = (r2 // c) == (c2 // c)
    masks_f = (left, eye2, ri >= lj, ri > lj, ones64, diag_blocks)
    masks_b = (left, eye2, ri <= lj, ri < lj, ones64, diag_blocks)

    def run(refs, ctx_refs, s_ref, o_ref, oc_ref, backward, masks):
        vals = [jnp.where(is_ctx, cr[0], r[0]) for cr, r in zip(ctx_refs, refs)]
        order = range(TOK // c - 1, -1, -1) if backward else range(TOK // c)
        outs = {}
        for p in range(GDN_PAIRS):
            s = s_ref[p]
            for n in order:
                blk = [a[n * c:(n + 1) * c, p * LANES:(p + 1) * LANES] for a in vals]
                o, s = _gdn_chunk(*blk, s, backward, masks)
                outs[(n, p)] = o
            s_ref[p] = s
        o_all = jnp.concatenate(
            [jnp.concatenate([outs[(n, p)] for p in range(GDN_PAIRS)], axis=1) for n in range(TOK // c)], axis=0)

        @pl.when(is_ctx)
        def _():
            oc_ref[0] = o_all

        @pl.when(jnp.logical_not(is_ctx))
        def _():
            o_ref[0] = o_all

    run((fq, fk, fv, fg, fb), (cq, ck, cv, cgf, cbf), sf_ref, of_ref, ocf_ref, False, masks_f)
    run((rq, rk, rv, rg, rb), (cq, ck, cv, cgb, cbb), sb_ref, ob_ref, ocb_ref, True, masks_b)


def _gdn_intra(probs, masks):
    left, eye2, ones64, tri, same_blk = masks
    c = GDN_CHUNK
    qs, ks, vs, gxs, bxs, bws = zip(*probs)
    qk_kk = [lax.dot_general(jnp.concatenate([q, k], axis=0).astype(_BF16), _block_diag(k, left), _NT,
                             preferred_element_type=_F32) for q, k in zip(qs, ks)]
    rs = []
    for gx in gxs:
        d0 = jnp.where(eye2, gx, 0.0)
        t_hi = d0.astype(_BF16)
        r1 = d0 - t_hi.astype(_F32)
        t_mid = r1.astype(_BF16)
        t_lo = (r1 - t_mid.astype(_F32)).astype(_BF16)
        r3 = jnp.dot(ones64, jnp.concatenate([t_hi, t_mid, t_lo], axis=1), preferred_element_type=_F32)
        rs.append(r3[:, :LANES] + r3[:, LANES:2 * LANES] + r3[:, 2 * LANES:])
    a_s, qkm, egs = [], [], []
    for x, gx, bx, r, bw in zip(qk_kk, gxs, bxs, rs, bws):
        incl, strict = tri[bw]
        dec = jnp.where(incl, jnp.exp(jnp.where(incl, gx - r, 0.0)), 0.0)
        a_s.append(bx * x[c:] * jnp.where(strict, dec, 0.0))
        qkm.append(x[:c] * dec)
        egs.append(jnp.exp(gx))
    eye_f = jnp.where(eye2, 1.0, 0.0)
    base = same_blk[8]
    d1 = [jnp.where(base, a, 0.0) for a in a_s]
    ps = [eye_f - d for d in d1]
    d2 = [_mm(d, _block_diag(d, left)) for d in d1]
    d2_bd = [_block_diag(d, left) for d in d2]
    ps = [p + _mm(p, bd) for p, bd in zip(ps, d2_bd)]
    d4 = [_mm(d, bd) for d, bd in zip(d2, d2_bd)]
    ps = [p + _mm(p, _block_diag(d, left)) for p, d in zip(ps, d4)]
    for blk in (8, 16, 32):
        off = jnp.logical_and(same_blk[2 * blk], jnp.logical_not(same_blk[blk]))
        t1 = [_mm(p, _block_diag(jnp.where(off, a, 0.0), left)) for p, a in zip(ps, a_s)]
        ps = [p - _mm(t, _block_diag(p, left)) for p, t in zip(ps, t1)]
    out = []
    for p, q, k, v, gx, bx, eg, qk, bw in zip(ps, qs, ks, vs, gxs, bxs, egs, qkm, bws):
        tot = gx[0:1] if bw else gx[c - 1:c]
        lhs = jnp.concatenate([k * eg, q * eg], axis=0).astype(_BF16)
        out.append((p, lhs, bx, bx * v, qk, k * jnp.exp(tot - gx), tot))
    return out


def _gdn_state_step(chains, left, diag_blocks):
    c = GDN_CHUNK
    ys = [jnp.dot(x[1], s.astype(_BF16), preferred_element_type=_F32) for s, x in chains]
    resid = [x[3] - x[2] * y[:c] for (s, x), y in zip(chains, ys)]
    deltas = [_mm(x[0], _block_diag(r, left)) for (s, x), r in zip(chains, resid)]
    os_ = [y[c:] + _mm(x[4], _block_diag(d, left)) for (s, x), y, d in zip(chains, ys, deltas)]
    upds = [lax.dot_general(x[5].astype(_BF16), d.astype(_BF16), _TN, preferred_element_type=_F32)
            for (s, x), d in zip(chains, deltas)]
    new_s = [s * jnp.exp(x[6]) + jnp.where(diag_blocks, u, 0.0) for (s, x), u in zip(chains, upds)]
    return list(zip(os_, new_s))


def _gdn_scan_kernel_bf(cq, ck, cv, cgf, cbf, cgb, cbb, fq, fk, fv, fg, fb, rq, rk, rv, rg, rb,
                        ocf_ref, ocb_ref, of_ref, ob_ref, sf_ref, sb_ref):
    step = pl.program_id(1)
    is_ctx = step == 0

    @pl.when(is_ctx)
    def _():
        sf_ref[...] = jnp.zeros_like(sf_ref)
        sb_ref[...] = jnp.zeros_like(sb_ref)

    c = GDN_CHUNK
    nc = TOK // c
    li = lax.broadcasted_iota(jnp.int32, (c, LANES), 1)
    ri = lax.broadcasted_iota(jnp.int32, (c, LANES), 0)
    lj = li & (c - 1)
    left = li < c
    eye2 = lj == ri
    ones64 = jnp.ones((c, c), _BF16)
    r2 = lax.broadcasted_iota(jnp.int32, (LANES, LANES), 0)
    c2 = lax.broadcasted_iota(jnp.int32, (LANES, LANES), 1)
    diag_blocks = (r2 // c) == (c2 // c)
    tri = {False: (ri >= lj, ri > lj), True: (ri <= lj, ri < lj)}
    same_blk = {b: (ri // b) == (lj // b) for b in (8, 16, 32, 64)}
    masks = (left, eye2, ones64, tri, same_blk)

    nbb = fq.shape[0]
    ctx_refs = {False: (cq, ck, cv, cgf, cbf), True: (cq, ck, cv, cgb, cbb)}
    lat_refs = {False: (fq, fk, fv, fg, fb), True: (rq, rk, rv, rg, rb)}
    vals = {(bw, e): [jnp.where(is_ctx, cr[e], r[e]) for cr, r in zip(ctx_refs[bw], lat_refs[bw])]
            for bw in (False, True) for e in range(nbb)}
    chain_keys = [(bw, e, p) for bw in (False, True) for e in range(nbb) for p in range(GDN_PAIRS)]
    prob_keys = [(bw, e, p, n) for bw, e, p in chain_keys for n in range(nc)]
    probs = [tuple(a[n * c:(n + 1) * c, p * LANES:(p + 1) * LANES] for a in vals[(bw, e)]) + (bw,)
             for bw, e, p, n in prob_keys]
    intra = dict(zip(prob_keys, _gdn_intra(probs, masks)))

    state_ref = lambda bw: sb_ref if bw else sf_ref
    states = {(bw, e, p): state_ref(bw)[e, p] for bw, e, p in chain_keys}
    outs = {}
    for t in range(nc):
        ns = {kk: (nc - 1 - t if kk[0] else t) for kk in chain_keys}
        res = _gdn_state_step([(states[kk], intra[kk + (ns[kk],)]) for kk in chain_keys], left, diag_blocks)
        for kk, (o, st) in zip(chain_keys, res):
            outs[kk + (ns[kk],)] = o
            states[kk] = st
    for bw, e, p in chain_keys:
        state_ref(bw)[e, p] = states[(bw, e, p)]

    for bw, o_ref, oc_ref in ((False, of_ref, ocf_ref), (True, ob_ref, ocb_ref)):
        for e in range(nbb):
            o_all = jnp.concatenate(
                [jnp.concatenate([outs[(bw, e, p, n)] for p in range(GDN_PAIRS)], axis=1) for n in range(nc)], axis=0)

            @pl.when(is_ctx)
            def _(o_all=o_all, oc_ref=oc_ref, e=e):
                oc_ref[e] = o_all

            @pl.when(jnp.logical_not(is_ctx))
            def _(o_all=o_all, o_ref=o_ref, e=e):
                o_ref[e] = o_all


def _gdn_scan(q, k, v, gf, gb, bf, bb):
    b, lt, w = q.shape
    nl = lt // TOK - 1
    nbb = GDN_BB
    ctx = pl.BlockSpec((nbb, TOK, w), lambda i, s: (i, 0, 0))
    fwd = pl.BlockSpec((nbb, TOK, w), lambda i, s: (i, jnp.maximum(s, 1), 0))
    bwd = pl.BlockSpec((nbb, TOK, w), lambda i, s: (i, nl + 1 - jnp.maximum(s, 1), 0))
    fwd_o = pl.BlockSpec((nbb, TOK, w), lambda i, s: (i, jnp.maximum(s, 1) - 1, 0))
    bwd_o = pl.BlockSpec((nbb, TOK, w), lambda i, s: (i, nl - jnp.maximum(s, 1), 0))
    return pl.pallas_call(
        _gdn_scan_kernel_bf,
        grid=(b // nbb, nl + 1),
        in_specs=[ctx] * 7 + [fwd] * 5 + [bwd] * 5,
        out_specs=[ctx, ctx, fwd_o, bwd_o],
        out_shape=[jax.ShapeDtypeStruct((b, TOK, w), _F32)] * 2 + [jax.ShapeDtypeStruct((b, nl * TOK, w), _F32)] * 2,
        scratch_shapes=[pltpu.VMEM((nbb, GDN_PAIRS, LANES, LANES), _F32)] * 2,
        compiler_params=_cparams(2, ("parallel", "arbitrary")),
        name="gdn_scan",
    )(q, k, v, gf, bf, gb, bb, q, k, v, gf, bf, q, k, v, gb, bb)


MLA_HEAD_PAD = LANES
MLA_WIDE = MLA_HEADS * MLA_HEAD_PAD
ATTN_TQ = 512
ATTN_TK = 512


def _rope_tables(n_ctx, seq):
    rows = seq // GRID_W
    row = np.repeat(np.arange(rows, dtype=np.float64), GRID_W)
    col = np.tile(np.arange(GRID_W, dtype=np.float64), rows)
    inv = ROPE_BASE ** (-np.arange(0, ROPE_AXIS, 2, dtype=np.float64) / ROPE_AXIS)
    ang = np.concatenate([row[:, None] * inv, col[:, None] * inv], axis=-1)
    cos, sin = np.cos(ang), np.sin(ang)
    half = MLA_ROPE // 2
    c = np.ones((n_ctx + seq, MLA_HEAD_PAD))
    s = np.zeros((n_ctx + seq, MLA_HEAD_PAD))
    c[n_ctx:, MLA_NOPE:MLA_NOPE + half] = cos
    c[n_ctx:, MLA_NOPE + half:MLA_QK] = cos
    s[n_ctx:, MLA_NOPE:MLA_NOPE + half] = -sin
    s[n_ctx:, MLA_NOPE + half:MLA_QK] = sin
    return c.astype(np.float32), s.astype(np.float32)


def _mla_prep_kernel(p_ref, qn_ref, kvn_ref, wq_ref, wk_ref, wv_ref, sel_ref, qg_ref, kg_ref, cos_ref, sin_ref,
                     qc_ref, ql_ref, k_ref, v_ref):
    p = p_ref[0]
    cq = p[:, :MLA_Q_RANK]
    ckv = p[:, MLA_Q_RANK:MLA_Q_RANK + MLA_KV_RANK]
    kr = p[:, MLA_Q_RANK + MLA_KV_RANK:]
    cq = (cq * lax.rsqrt(jnp.mean(cq * cq, axis=-1, keepdims=True) + EPS) * qn_ref[...]).astype(_BF16)
    ckv = (ckv * lax.rsqrt(jnp.mean(ckv * ckv, axis=-1, keepdims=True) + EPS) * kvn_ref[...]).astype(_BF16)
    q = jnp.dot(cq, wq_ref[...], preferred_element_type=_F32)
    k = (jnp.dot(ckv, wk_ref[...], preferred_element_type=_F32)
         + jnp.dot(kr, sel_ref[...], precision=_HI, preferred_element_type=_F32))
    lane = lax.broadcasted_iota(jnp.int32, (TOK, MLA_WIDE), 1) & (MLA_HEAD_PAD - 1)
    v = jnp.dot(ckv, wv_ref[...], preferred_element_type=_F32) + jnp.where(lane == MLA_V, 1.0, 0.0)
    cos = jnp.concatenate([cos_ref[...]] * MLA_HEADS, axis=1)
    sin = jnp.concatenate([sin_ref[...]] * MLA_HEADS, axis=1)
    half = MLA_ROPE // 2
    first = jnp.logical_and(lane >= MLA_NOPE, lane < MLA_NOPE + half)
    second = jnp.logical_and(lane >= MLA_NOPE + half, lane < MLA_QK)

    def head_norm_rope(x, g):
        parts = []
        for h in range(MLA_HEADS):
            xh = x[:, h * MLA_HEAD_PAD:(h + 1) * MLA_HEAD_PAD]
            ms = jnp.sum(xh * xh, axis=-1, keepdims=True) * (1.0 / MLA_QK)
            parts.append(xh * lax.rsqrt(ms + EPS))
        xn = jnp.concatenate(parts, axis=1) * g
        up = pltpu.roll(xn, half, 1)
        down = pltpu.roll(xn, MLA_WIDE - half, 1)
        swapped = jnp.where(first, down, jnp.where(second, up, 0.0))
        return xn * cos + swapped * sin

    qf = head_norm_rope(q, qg_ref[...]) * (MLA_QK ** -0.5 * math.log2(math.e))
    kf = head_norm_rope(k, kg_ref[...])
    is_ctx = pl.program_id(1) == 0
    for h in range(MLA_HEADS):
        sl = slice(h * MLA_HEAD_PAD, (h + 1) * MLA_HEAD_PAD)
        k_ref[0, h] = kf[:, sl].astype(_BF16)
        v_ref[0, h] = v[:, sl].astype(_BF16)

    @pl.when(is_ctx)
    def _():
        for h in range(MLA_HEADS):
            qc_ref[0, h] = qf[:, h * MLA_HEAD_PAD:(h + 1) * MLA_HEAD_PAD].astype(_BF16)

    @pl.when(jnp.logical_not(is_ctx))
    def _():
        for h in range(MLA_HEADS):
            ql_ref[0, h] = qf[:, h * MLA_HEAD_PAD:(h + 1) * MLA_HEAD_PAD].astype(_BF16)


def _mla_weights(w_uq, w_ukv, q_head_g, k_head_g):
    pad = MLA_HEAD_PAD
    wq = jnp.pad(w_uq.reshape(MLA_Q_RANK, MLA_HEADS, MLA_QK), ((0, 0), (0, 0), (0, pad - MLA_QK)))
    wkv = w_ukv.reshape(MLA_KV_RANK, MLA_HEADS, MLA_NOPE + MLA_V)
    wk = jnp.pad(wkv[:, :, :MLA_NOPE], ((0, 0), (0, 0), (0, pad - MLA_NOPE)))
    wv = jnp.pad(wkv[:, :, MLA_NOPE:], ((0, 0), (0, 0), (0, pad - MLA_V)))
    sel = np.zeros((MLA_PAD - MLA_Q_RANK - MLA_KV_RANK, MLA_WIDE), np.float32)
    for h in range(MLA_HEADS):
        for r in range(MLA_ROPE):
            sel[r, h * pad + MLA_NOPE + r] = 1.0
    tile_g = lambda g: jnp.tile(jnp.pad(g, (0, pad - MLA_QK)), MLA_HEADS)[None, :]
    flat = lambda w: w.reshape(w.shape[0], MLA_WIDE).astype(_BF16)
    return flat(wq), flat(wk), flat(wv), jnp.asarray(sel), tile_g(q_head_g), tile_g(k_head_g)


def _mla_prep(mla_in, q_norm_g, kv_norm_g, weights, seq):
    b, lt, _ = mla_in.shape
    nt = lt // TOK
    wq, wk, wv, sel, qg, kg = weights
    cos, sin = (jnp.asarray(t) for t in _rope_tables(lt - seq, seq))
    whole = lambda a: pl.BlockSpec(a.shape, lambda i, j: (0,) * a.ndim)
    tab = pl.BlockSpec((TOK, MLA_HEAD_PAD), lambda i, j: (j, 0))
    hd = lambda f: pl.BlockSpec((1, MLA_HEADS, TOK, MLA_HEAD_PAD), f)
    qn, kvn = q_norm_g[None, :], kv_norm_g[None, :]
    shp = lambda t: jax.ShapeDtypeStruct((b, MLA_HEADS, t, MLA_HEAD_PAD), _BF16)
    return pl.pallas_call(
        _mla_prep_kernel,
        grid=(b, nt),
        in_specs=[pl.BlockSpec((1, TOK, MLA_PAD), lambda i, j: (i, j, 0)), whole(qn), whole(kvn),
                  whole(wq), whole(wk), whole(wv), whole(sel), whole(qg), whole(kg), tab, tab],
        out_specs=[hd(lambda i, j: (i, 0, 0, 0)), hd(lambda i, j: (i, 0, jnp.maximum(j, 1) - 1, 0)),
                   hd(lambda i, j: (i, 0, j, 0)), hd(lambda i, j: (i, 0, j, 0))],
        out_shape=[shp(TOK), shp(lt - TOK), shp(lt), shp(lt)],
        compiler_params=_cparams(2, ("parallel", "arbitrary")),
        name="mla_prep",
    )(mla_in, qn, kvn, wq, wk, wv, sel, qg, kg, cos, sin)


def _attn_kernel(q_ref, k_ref, v_ref, o_ref):
    n_keys = k_ref.shape[2]
    starts = list(range(0, n_keys, ATTN_TK))
    heads = range(2)

    def scores(h, lo):
        hi = min(lo + ATTN_TK, n_keys)
        return lax.dot_general(q_ref[0, h], k_ref[0, h, lo:hi, :], _NT, preferred_element_type=_F32)

    nxt = [scores(h, starts[0]) for h in heads]
    m = [None, None]
    acc = [None, None]
    for n, lo in enumerate(starts):
        cur = nxt
        if n + 1 < len(starts):
            nxt = [scores(h, starts[n + 1]) for h in heads]
        hi = min(lo + ATTN_TK, n_keys)
        for h in heads:
            s = cur[h]
            m_blk = jnp.max(s, axis=-1, keepdims=True)
            m_new = m_blk if n == 0 else jnp.maximum(m[h], m_blk)
            p = jnp.exp2(s - m_new).astype(_BF16)
            pv = jnp.dot(p, v_ref[0, h, lo:hi, :], preferred_element_type=_F32)
            acc[h] = pv if n == 0 else acc[h] * jnp.exp2(m[h] - m_new) + pv
            m[h] = m_new
    o_ref[0] = jnp.concatenate([a[:, :MLA_V] / a[:, MLA_V:MLA_V + 1] for a in acc], axis=1)


def _attention(q, k, v, n_keys, tq):
    b, h, t, w = q.shape
    return pl.pallas_call(
        _attn_kernel,
        grid=(b, h // 2, t // tq),
        in_specs=[pl.BlockSpec((1, 2, tq, w), lambda i, j, l: (i, j, l, 0)),
                  pl.BlockSpec((1, 2, n_keys, w), lambda i, j, l: (i, j, 0, 0)),
                  pl.BlockSpec((1, 2, n_keys, w), lambda i, j, l: (i, j, 0, 0))],
        out_specs=pl.BlockSpec((1, tq, 2 * MLA_V), lambda i, j, l: (i, l, j)),
        out_shape=jax.ShapeDtypeStruct((b, t, h * MLA_V), _F32),
        compiler_params=_cparams(3),
        name="mla_attention",
    )(q, k, v)


HY_BLK = 256
HY_LO = 128
HY_CB = 16


def _hy_prep_kernel(xp_ref, x_ref, xn_ref, cw_ref, cb_ref, x0_ref, z_ref, zc_ref, zl_ref, xe_ref):
    j = pl.program_id(1)
    nt = pl.num_programs(1)
    pv = (j >= 2).astype(_F32)
    nv = jnp.logical_and(j >= 1, j < nt - 1).astype(_F32)
    xe_ref[0:SUBLANES] = xp_ref[0] * pv
    xe_ref[SUBLANES:SUBLANES + TOK] = x_ref[0]
    xe_ref[SUBLANES + TOK:] = xn_ref[0] * nv
    cw = cw_ref[...]
    u = (xe_ref[SUBLANES - 1:SUBLANES - 1 + TOK] * cw[0:1] + x_ref[0] * cw[1:2]
         + xe_ref[SUBLANES + 1:SUBLANES + 1 + TOK] * cw[2:3] + cb_ref[...])
    x0_ref[0] = u[:, :HY_CH]
    z = u[:, 2 * HY_CH:] * u[:, HY_CH:2 * HY_CH]
    z_ref[0] = z

    @pl.when(j == 0)
    def _():
        zc_ref[0] = z.astype(_BF16)

    @pl.when(j > 0)
    def _():
        zl_ref[0] = z.astype(_BF16)


def _hy_prep(hy_in, conv_w, conv_b):
    b, lt, _ = hy_in.shape
    nh = TOK // SUBLANES
    last = lt // SUBLANES - 1
    whole = lambda a: pl.BlockSpec(a.shape, lambda i, j: (0,) * a.ndim)
    row = lambda n: pl.BlockSpec((1, TOK, n), lambda i, j: (i, j, 0))
    cb = conv_b[None, :]
    return pl.pallas_call(
        _hy_prep_kernel,
        grid=(b, lt // TOK),
        in_specs=[pl.BlockSpec((1, SUBLANES, HY_IN), lambda i, j: (i, jnp.maximum(j * nh - 1, 0), 0)),
                  row(HY_IN),
                  pl.BlockSpec((1, SUBLANES, HY_IN), lambda i, j: (i, jnp.minimum((j + 1) * nh, last), 0)),
                  whole(conv_w), whole(cb)],
        out_specs=[row(HY_CH), row(HY_CH),
                   pl.BlockSpec((1, TOK, HY_CH), lambda i, j: (i, 0, 0)),
                   pl.BlockSpec((1, TOK, HY_CH), lambda i, j: (i, jnp.maximum(j, 1) - 1, 0))],
        out_shape=[jax.ShapeDtypeStruct((b, lt, HY_CH), _F32)] * 2
        + [jax.ShapeDtypeStruct((b, TOK, HY_CH), _BF16), jax.ShapeDtypeStruct((b, lt - TOK, HY_CH), _BF16)],
        scratch_shapes=[pltpu.VMEM((TOK + 2 * SUBLANES, HY_IN), _F32)],
        compiler_params=_cparams(2, ("parallel", "arbitrary")),
        name="hyena_prep",
    )(hy_in, hy_in, hy_in, conv_w, cb)


def _hy_filter_consts(l):
    def emb(t):
        t = t.astype(np.float64)
        t_norm = t / max(l - 1, 1)
        bands = np.linspace(1e-4, HY_BANDS - 1, HY_BANDS)
        ang = 2.0 * math.pi * t[:, None] * bands[None, :] / l
        z = np.concatenate([t_norm[:, None], np.cos(ang), np.sin(ang)], axis=-1)
        return np.pad(z, ((0, 0), (0, LANES - HY_EMB))), t_norm[:, None]
    r = np.arange(l)
    e_rev, tn_rev = emb(l - 1 - r)
    e_sh, tn_sh = emb(r + 1)
    deltas = np.abs(np.linspace(HY_MIN_DECAY, HY_MAX_DECAY, HY_CH))[None, :]
    f = lambda a: np.asarray(a, np.float32)
    return f(e_rev), f(e_sh), f(tn_rev), f(tn_sh), f(deltas)


def _hy_filter_kernel(er_ref, es_ref, tr_ref, ts_ref, dl_ref, w1_ref, b1_ref, w2_ref, b2_ref, w3_ref, b3_ref, o_ref):
    l = er_ref.shape[0]

    def mlp(e, col):
        h = jnp.sin(jnp.dot(e, w1_ref[...], precision=_HI, preferred_element_type=_F32) + b1_ref[...])
        h = jnp.sin(jnp.dot(h, w2_ref[...], precision=_HI, preferred_element_type=_F32) + b2_ref[...])
        return (jnp.dot(h, w3_ref[:, col * HY_CH:(col + 1) * HY_CH], precision=_HI, preferred_element_type=_F32)
                + b3_ref[:, col * HY_CH:(col + 1) * HY_CH])

    hf = mlp(er_ref[...], 0) * jnp.exp(-tr_ref[...] * dl_ref[...])
    hb = mlp(es_ref[...], 1) * jnp.exp(-ts_ref[...] * dl_ref[...])
    row = lax.broadcasted_iota(jnp.int32, hb.shape, 0)
    hb = jnp.where(row < l - 1, hb, 0.0)
    norm = jnp.sum(jnp.abs(hf), axis=0, keepdims=True) + jnp.sum(jnp.abs(hb), axis=0, keepdims=True)
    o_ref[...] = jnp.transpose(jnp.concatenate([hf, hb], axis=0) / norm)


def _hy_filter(l, w1, b1, w2, b2, w3, b3):
    consts = [jnp.asarray(a) for a in _hy_filter_consts(l)]
    w1p = jnp.pad(w1, ((0, LANES - HY_EMB), (0, 0)))
    args = consts + [w1p, b1[None, :], w2, b2[None, :], w3, b3[None, :]]
    return pl.pallas_call(
        _hy_filter_kernel,
        out_shape=jax.ShapeDtypeStruct((HY_CH, 2 * l), _F32),
        compiler_params=pltpu.CompilerParams(vmem_limit_bytes=VMEM_LIMIT_BYTES),
        name="hyena_filter",
    )(*args)


def _hy_conv_kernel(f_ref, z_ref, y_ref, g_ref, *, nblk, nb):
    cols = z_ref.shape[2]
    lane = lax.broadcasted_iota(jnp.int32, (HY_BLK, cols), 1)

    def channel(ch, carry):
        base = pltpu.roll(jnp.broadcast_to(f_ref[ch], (SUBLANES, f_ref.shape[2])), 1, 1, stride=1, stride_axis=0)
        for a in range(0, HY_LO // SUBLANES, 2):
            lo = base if a == 0 else pltpu.roll(base, SUBLANES * a, 1)
            hi = pltpu.roll(base, SUBLANES * (a + 1), 1)
            g_ref[a * SUBLANES:(a + 2) * SUBLANES, :] = jnp.concatenate([lo, hi], axis=0).astype(_BF16)
        z = z_ref[ch]
        y = jnp.zeros((HY_BLK, cols), _F32)
        for d in range(-(nblk - 1), nblk):
            o = HY_BLK * (nblk - d)
            t_d = jnp.concatenate([g_ref[:, o:o + HY_BLK], g_ref[:, o - HY_LO:o - HY_LO + HY_BLK]], axis=0)
            r = jnp.dot(t_d, z, preferred_element_type=_F32)
            if d > 0:
                r = jnp.where(lane >= nb * d, pltpu.roll(r, nb * d, 1), 0.0)
            elif d < 0:
                r = jnp.where(lane < nb * (nblk + d), pltpu.roll(r, cols + nb * d, 1), 0.0)
            y = y + r
        y_ref[ch] = y
        return carry

    lax.fori_loop(0, f_ref.shape[0], channel, 0)


def _hy_conv(fline, z):
    b, l, c = z.shape
    nblk = l // HY_BLK
    cols = max(nblk * b, LANES)
    zall = z.reshape(b, nblk, HY_BLK, c).transpose(3, 2, 1, 0).reshape(c, HY_BLK, nblk * b)
    zall = jnp.pad(zall.astype(_BF16), ((0, 0), (0, 0), (0, cols - nblk * b)))
    y = pl.pallas_call(
        functools.partial(_hy_conv_kernel, nblk=nblk, nb=b),
        grid=(c // HY_CB,),
        in_specs=[pl.BlockSpec((HY_CB, 1, 2 * l), lambda i: (i, 0, 0)),
                  pl.BlockSpec((HY_CB, HY_BLK, cols), lambda i: (i, 0, 0))],
        out_specs=pl.BlockSpec((HY_CB, HY_BLK, cols), lambda i: (i, 0, 0)),
        out_shape=jax.ShapeDtypeStruct((c, HY_BLK, cols), _F32),
        scratch_shapes=[pltpu.VMEM((HY_LO, 2 * l), _BF16)],
        compiler_params=_cparams(1),
        name="hyena_conv",
    )(fline.reshape(c, 1, 2 * l), zall)
    return y[:, :, :nblk * b].reshape(c, HY_BLK, nblk, b).transpose(3, 2, 1, 0).reshape(b, l, c)


def _out_proj_kernel(x_ref, of_ref, ob_ref, z_ref, gn_ref, hones_ref, mla_ref, hx_ref, hz_ref, hy_ref, hd_ref,
                     ga_ref, w_ref, o_ref):
    o = of_ref[0] + ob_ref[0]
    ms = jnp.dot(jnp.concatenate(_split_bf16(o * o, 2), axis=1), hones_ref[...],
                 preferred_element_type=_F32) * (1.0 / GDN_DV)
    gdn = o * lax.rsqrt(ms + EPS) * gn_ref[...] * _silu(z_ref[0])
    hy = hx_ref[0] * (hy_ref[0] + hz_ref[0] * hd_ref[...])
    mix = jnp.concatenate([gdn, mla_ref[0], hy], axis=-1).astype(_BF16)
    y = jnp.dot(mix, w_ref[...], preferred_element_type=_F32)
    o_ref[0] = x_ref[0] + ga_ref[0] * y


def _out_proj(x, o_f, o_b, z, gn_row, mla, hx0, hz, hy, hd_row, ga, w_out, toff):
    b, l, d = x.shape
    hones = _gdn_prep_consts()[2]
    row = lambda n: pl.BlockSpec((1, TOK, n), lambda i, j: (i, j, 0))
    rowc = lambda n: pl.BlockSpec((1, TOK, n), lambda i, j: (i, j + toff, 0))
    whole = lambda a: pl.BlockSpec(a.shape, lambda i, j: (0,) * a.ndim)
    return pl.pallas_call(
        _out_proj_kernel,
        grid=(b, l // TOK),
        in_specs=[row(d), row(GDN_WIDTH), row(GDN_WIDTH), rowc(GDN_WIDTH), whole(gn_row), whole(hones),
                  row(MLA_WIDTH), rowc(HY_WIDTH), rowc(HY_WIDTH), row(HY_WIDTH), whole(hd_row),
                  pl.BlockSpec((1, 1, d), lambda i, j: (i, 0, 0)),
                  pl.BlockSpec((d, d), lambda i, j: (0, 0))],
        out_specs=row(d),
        out_shape=jax.ShapeDtypeStruct((b, l, d), _F32),
        compiler_params=_cparams(2),
        name="out_proj",
    )(x, o_f, o_b, z, gn_row, hones, mla, hx0, hz, hy, hd_row, ga, w_out)


def _ffn_kernel(xp_ref, x_ref, xn_ref, g_ref, sf_ref, cf_ref, gf_ref, wup_ref, cw_ref, cb_ref, wdn_ref,
                o_ref, h_ref, up_ref, uv_ref, act_ref, *, tl):
    i = pl.program_id(1)
    nt = pl.num_programs(1)
    g, sf, cf = g_ref[...], sf_ref[0], cf_ref[0]
    pv = (i > 0).astype(_F32)
    nv = (i < nt - 1).astype(_F32)
    h_ref[0:HALO] = (_norm_mod(xp_ref[0], g, sf, cf) * pv).astype(_BF16)
    h_ref[HALO:HALO + tl] = _norm_mod(x_ref[0], g, sf, cf).astype(_BF16)
    h_ref[HALO + tl:] = (_norm_mod(xn_ref[0], g, sf, cf) * nv).astype(_BF16)

    def up_proj(c, slot):
        lo = pl.multiple_of(c * FFN_CHUNK, FFN_CHUNK)
        up_ref[slot] = jnp.dot(h_ref[...], wup_ref[:, pl.ds(lo, FFN_CHUNK)], preferred_element_type=_F32)
        uv_ref[slot] = jnp.dot(h_ref[HALO:HALO + tl], wup_ref[:, pl.ds(D_FF + lo, FFN_CHUNK)],
                               preferred_element_type=_F32)

    def gate(c, slot):
        lo = pl.multiple_of(c * FFN_CHUNK, FFN_CHUNK)
        cw = cw_ref[:, pl.ds(lo, FFN_CHUNK)]
        cb = cb_ref[:, pl.ds(lo, FFN_CHUNK)]
        gt = (up_ref[slot, HALO - 1:HALO - 1 + tl] * cw[0:1] + up_ref[slot, HALO:HALO + tl] * cw[1:2]
              + up_ref[slot, HALO + 1:HALO + 1 + tl] * cw[2:3] + cb)
        act_ref[:, pl.ds(lo, FFN_CHUNK)] = (_silu(gt) * uv_ref[slot]).astype(_BF16)

    n_chunks = D_FF // FFN_CHUNK
    up_proj(0, 0)

    def pair(it, carry):
        c = 2 * it
        up_proj(c + 1, 1)
        gate(c, 0)
        up_proj(c + 2, 0)
        gate(c + 1, 1)
        return carry

    lax.fori_loop(0, (n_chunks - 1) // 2, pair, 0)
    assert n_chunks % 2 == 1
    gate(n_chunks - 1, 0)
    y = jnp.dot(act_ref[...], wdn_ref[...], preferred_element_type=_F32)
    o_ref[0] = x_ref[0] + gf_ref[0] * y


def _ffn(x, g, sf, cf, gf, w_up, conv_w, conv_b, w_down, tl):
    b, l, d = x.shape
    nh = tl // HALO
    last = l // HALO - 1
    vec = pl.BlockSpec((1, 1, d), lambda i, j: (i, 0, 0))
    whole = lambda a: pl.BlockSpec(a.shape, lambda i, j: (0,) * a.ndim)
    return pl.pallas_call(
        functools.partial(_ffn_kernel, tl=tl),
        grid=(b, l // tl),
        in_specs=[
            pl.BlockSpec((1, HALO, d), lambda i, j: (i, jnp.maximum(j * nh - 1, 0), 0)),
            pl.BlockSpec((1, tl, d), lambda i, j: (i, j, 0)),
            pl.BlockSpec((1, HALO, d), lambda i, j: (i, jnp.minimum((j + 1) * nh, last), 0)),
            pl.BlockSpec((1, d), lambda i, j: (0, 0)),
            vec, vec, vec,
            whole(w_up), whole(conv_w), whole(conv_b), whole(w_down),
        ],
        out_specs=pl.BlockSpec((1, tl, d), lambda i, j: (i, j, 0)),
        out_shape=jax.ShapeDtypeStruct((b, l, d), _F32),
        scratch_shapes=[
            pltpu.VMEM((tl + 2 * HALO, d), _BF16),
            pltpu.VMEM((2, tl + 2 * HALO, FFN_CHUNK), _F32),
            pltpu.VMEM((2, tl, FFN_CHUNK), _F32),
            pltpu.VMEM((tl, D_FF), _BF16),
        ],
        compiler_params=_cparams(2),
        name="conv_ffn",
    )(x, x, x, g, sf, cf, gf, w_up, conv_w, conv_b, w_down)


def rms_norm(x, g):
    xf = x.astype(jnp.float32)
    y = xf * lax.rsqrt(jnp.mean(xf * xf, axis=-1, keepdims=True) + EPS)
    return (y * g.astype(jnp.float32)).astype(x.dtype)


def dwconv3(x, w, b=None):
    xp = jnp.pad(x, ((0, 0), (1, 1), (0, 0)))
    y = xp[:, :-2] * w[0] + xp[:, 1:-1] * w[1] + xp[:, 2:] * w[2]
    return y if b is None else y + b


def axial_rope(l):
    rows = l // GRID_W
    row = jnp.repeat(jnp.arange(rows, dtype=jnp.float32), GRID_W)
    col = jnp.tile(jnp.arange(GRID_W, dtype=jnp.float32), rows)
    inv = ROPE_BASE ** (-jnp.arange(0, ROPE_AXIS, 2, dtype=jnp.float32) / ROPE_AXIS)
    ang = jnp.concatenate([row[:, None] * inv, col[:, None] * inv], axis=-1)
    return jnp.cos(ang), jnp.sin(ang)


def apply_rope(x, cos, sin):
    half = x.shape[-1] // 2
    x1, x2 = x[..., :half], x[..., half:]
    cos, sin = cos[None, :, None, :], sin[None, :, None, :]
    return jnp.concatenate([x1 * cos - x2 * sin, x1 * sin + x2 * cos], axis=-1)


def mla_heads(p, q_norm_g, w_uq, kv_norm_g, w_ukv, q_head_g, k_head_g, rope):
    b, l, _ = p.shape
    c_q = rms_norm(p[..., :MLA_Q_RANK], q_norm_g)
    c_kv = rms_norm(p[..., MLA_Q_RANK:MLA_Q_RANK + MLA_KV_RANK], kv_norm_g)
    k_rope = p[..., MLA_Q_RANK + MLA_KV_RANK:MLA_IN]
    q = (c_q @ w_uq).reshape(b, l, MLA_HEADS, MLA_QK)
    kv = (c_kv @ w_ukv).reshape(b, l, MLA_HEADS, MLA_NOPE + MLA_V)
    k = jnp.concatenate([kv[..., :MLA_NOPE],
                         jnp.broadcast_to(k_rope[:, :, None, :], (b, l, MLA_HEADS, MLA_ROPE))], axis=-1)
    v = kv[..., MLA_NOPE:]
    q = rms_norm(q, q_head_g)
    k = rms_norm(k, k_head_g)
    if rope is not None:
        cos, sin = rope
        q = jnp.concatenate([q[..., :MLA_NOPE], apply_rope(q[..., MLA_NOPE:], cos, sin)], axis=-1)
        k = jnp.concatenate([k[..., :MLA_NOPE], apply_rope(k[..., MLA_NOPE:], cos, sin)], axis=-1)
    return q, k, v


def softmax_attend(q, k, v, scale):
    s = jnp.einsum('bqhd,bkhd->bhqk', q, k).astype(jnp.float32) * scale
    p = jax.nn.softmax(s, axis=-1).astype(v.dtype)
    return jnp.einsum('bhqk,bkhd->bqhd', p, v)


def mla_mixer(p_ctx, p_lat, q_norm_g, w_uq, kv_norm_g, w_ukv, q_head_g, k_head_g, rope, with_ctx):
    qc, kc, vc = mla_heads(p_ctx, q_norm_g, w_uq, kv_norm_g, w_ukv, q_head_g, k_head_g, None)
    ql, kl, vl = mla_heads(p_lat, q_norm_g, w_uq, kv_norm_g, w_ukv, q_head_g, k_head_g, rope)
    scale = MLA_QK ** -0.5
    k_all = jnp.concatenate([kl, kc], axis=1)
    v_all = jnp.concatenate([vl, vc], axis=1)
    b, l = ql.shape[0], ql.shape[1]
    nb = l // ATTN_BLOCK
    q_blocks = jnp.moveaxis(ql.reshape(b, nb, ATTN_BLOCK, MLA_HEADS, MLA_QK), 1, 0)
    o_lat = lax.map(lambda qb: softmax_attend(qb, k_all, v_all, scale), q_blocks)
    o_lat = jnp.moveaxis(o_lat, 0, 1).reshape(b, l, MLA_WIDTH)
    if not with_ctx:
        return jnp.zeros((b, qc.shape[1], MLA_WIDTH), o_lat.dtype), o_lat
    o_ctx = softmax_attend(qc, kc, vc, scale).reshape(b, qc.shape[1], MLA_WIDTH)
    return o_ctx, o_lat


def hyena_filter(l, w1, b1, w2, b2, w3, b3):
    t = jnp.arange(l, dtype=jnp.float32)
    t_norm = t / max(l - 1, 1)
    bands = jnp.linspace(1e-4, HY_BANDS - 1, HY_BANDS, dtype=jnp.float32)
    ang = 2.0 * math.pi * t[:, None] * bands[None, :] / l
    z = jnp.concatenate([t_norm[:, None], jnp.cos(ang), jnp.sin(ang)], axis=-1)
    h = jnp.sin(z @ w1 + b1)
    h = jnp.sin(h @ w2 + b2)
    h = (h @ w3 + b3).reshape(l, 2, HY_CH).astype(jnp.float32)
    deltas = jnp.abs(jnp.linspace(HY_MIN_DECAY, HY_MAX_DECAY, HY_CH, dtype=jnp.float32))
    h = h * jnp.exp(-t_norm[:, None, None] * deltas)
    buf = jnp.concatenate([h[:, 0], jnp.zeros((1, HY_CH), jnp.float32), h[:0:-1, 1]], axis=0)
    return buf / jnp.sum(jnp.abs(buf), axis=0, keepdims=True)


def hyena_mixer(p, conv_w, conv_b, w1, b1, w2, b2, w3, b3, d_skip):
    b, l, _ = p.shape
    u = dwconv3(p, conv_w, conv_b)
    x0, x1, v = u[..., :HY_CH], u[..., HY_CH:2 * HY_CH], u[..., 2 * HY_CH:]
    z = (v * x1).astype(jnp.float32)
    buf = hyena_filter(l, w1, b1, w2, b2, w3, b3)
    zf = jnp.fft.rfft(z, n=2 * l, axis=1)
    hf = jnp.fft.rfft(buf, n=2 * l, axis=0)
    y = jnp.fft.irfft(zf * hf[None], n=2 * l, axis=1)[:, :l] + z * d_skip
    return x0 * y.astype(x0.dtype)


def _pad_row(v, n=LANES):
    v = v.reshape(1, -1)
    return jnp.pad(v, ((0, 0), (0, n - v.shape[1])))


def kernel(x, c, ctx, c_ctx, ada_w, ada_b, mix_norm_g, w_in, gdn_conv_w, gdn_a_log, gdn_dt_bias, gdn_norm_g, mla_q_norm_g, mla_w_uq, mla_kv_norm_g, mla_w_ukv, mla_q_head_g, mla_k_head_g, hy_conv_w, hy_conv_b, hy_w1, hy_b1, hy_w2, hy_b2, hy_w3, hy_b3, hy_d, w_out, ffn_norm_g, ffn_w_up, ffn_conv_w, ffn_conv_b, ffn_w_down):
    bsz, seq, d = x.shape
    n_ctx = ctx.shape[1]
    assert n_ctx == TOK and seq % TOK == 0 and bsz < ADA_ROWS
    cond = jnp.concatenate([c, c_ctx[None, :], jnp.zeros((ADA_ROWS - bsz - 1, d), c.dtype)], axis=0)
    for i in range(DEPTH):
        last = i == DEPTH - 1
        mod = _ada_mod(cond, ada_w[i], ada_b[i])
        mod_lat = mod[:bsz, None, :]
        mod_ctx = mod[bsz][None, None, :]
        sa_l, ca_l, ga_l, sf_l, cf_l, gf_l = jnp.split(mod_lat, 6, axis=-1)
        sa_c, ca_c, ga_c, sf_c, cf_c, gf_c = (jnp.broadcast_to(t, (bsz, 1, d)) for t in jnp.split(mod_ctx, 6, axis=-1))

        w_in_p = _pad_w_in(w_in[i])
        g_mix = mix_norm_g[i][None, :]
        qkv, z, ab, mla_in, hy_in = _in_proj(x, ctx, g_mix, sa_l, ca_l, sa_c[:1], ca_c[:1], w_in_p)

        q, k, v, gf, gb, bf, bb = _gdn_prep(qkv, ab, gdn_conv_w[i], _pad_row(gdn_a_log[i]), _pad_row(gdn_dt_bias[i]))
        oc_f, oc_b, ol_f, ol_b = _gdn_scan(q, k, v, gf, gb, bf, bb)

        mla_w = _mla_weights(mla_w_uq[i], mla_w_ukv[i], mla_q_head_g[i], mla_k_head_g[i])
        q_ctx, q_lat, k_all, v_all = _mla_prep(mla_in, mla_q_norm_g[i], mla_kv_norm_g[i], mla_w, seq)
        mla_l = _attention(q_lat, k_all, v_all, n_ctx + seq, ATTN_TQ)
        hy_x0, hy_z, hy_zc, hy_zl = _hy_prep(hy_in, hy_conv_w[i], hy_conv_b[i])
        hy_mlp = (hy_w1[i], hy_b1[i], hy_w2[i], hy_b2[i], hy_w3[i], hy_b3[i])
        hy_l = _hy_conv(_hy_filter(seq, *hy_mlp), hy_zl)
        hd_row = hy_d[i][None, :]

        w_out_b = w_out[i].astype(_BF16)
        w_up_b = ffn_w_up[i].astype(_BF16)
        w_dn_b = ffn_w_down[i].astype(_BF16)
        g_ffn = ffn_norm_g[i][None, :]
        cb = ffn_conv_b[i][None, :]
        gn_row = jnp.tile(gdn_norm_g[i], GDN_HEADS)[None, :]

        x = _out_proj(x, ol_f, ol_b, z, gn_row, mla_l, hy_x0, hy_z, hy_l, hd_row, ga_l, w_out_b, 1)
        x = _ffn(x, g_ffn, sf_l, cf_l, gf_l, w_up_b, ffn_conv_w[i], cb, w_dn_b, 512)

        if not last:
            mla_c = _attention(q_ctx, k_all, v_all, n_ctx, TOK)
            hy_c = _hy_conv(_hy_filter(n_ctx, *hy_mlp), hy_zc)
            ctx = _out_proj(ctx, oc_f, oc_b, z, gn_row, mla_c, hy_x0, hy_z, hy_c, hd_row, ga_c, w_out_b, 0)
            ctx = _ffn(ctx, g_ffn, sf_c, cf_c, gf_c, w_up_b, ffn_conv_w[i], cb, w_dn_b, TOK)
    return x
```

```python
import functools
import math

import jax
import jax.numpy as jnp
import numpy as np
from jax import lax
from jax.experimental import pallas as pl
from jax.experimental.pallas import tpu as pltpu

D_MODEL = 1024
DEPTH = 2
GRID_W = 64
EPS = 1e-6

GDN_HEADS = 6
GDN_DK = 64
GDN_DV = 64
GDN_CHUNK = 64

MLA_HEADS = 6
MLA_Q_RANK = 256
MLA_KV_RANK = 128
MLA_NOPE = 64
MLA_ROPE = 32
MLA_V = 64
MLA_QK = MLA_NOPE + MLA_ROPE
ATTN_BLOCK = 128
ROPE_BASE = 10000.0
ROPE_AXIS = MLA_ROPE // 2

HY_CH = 256
HY_BANDS = 16
HY_EMB = 1 + 2 * HY_BANDS
HY_HIDDEN = 64
HY_TARGET = 1e-2
HY_FAST_DECAY_PCT = 0.3
HY_SLOW_DECAY_PCT = 1.5
HY_MAX_DECAY = math.log(HY_TARGET) / HY_FAST_DECAY_PCT
HY_MIN_DECAY = math.log(HY_TARGET) / HY_SLOW_DECAY_PCT

D_FF = 2816

GDN_WIDTH = GDN_HEADS * GDN_DV
MLA_WIDTH = MLA_HEADS * MLA_V
HY_WIDTH = HY_CH
GDN_QKV = GDN_HEADS * (2 * GDN_DK + GDN_DV)
GDN_IN = GDN_QKV + GDN_WIDTH + 4 * GDN_HEADS
MLA_IN = MLA_Q_RANK + MLA_KV_RANK + MLA_ROPE
HY_IN = 3 * HY_CH

LANES = 128
SUBLANES = 8
SUBLANES_BF16 = 16
VMEM_LIMIT_BYTES = 56 * 1024 * 1024

TOK = 256
GDN_PAIRS = GDN_HEADS // 2
GDN_BB = 2
N_GATE = 4 * GDN_HEADS

AB_PAD = LANES
MLA_PAD = 512
IN_GROUPS = (GDN_QKV, GDN_WIDTH, AB_PAD, MLA_PAD, HY_IN)
IN_TOTAL = sum(IN_GROUPS)

FFN_CHUNK = 256
FFN_TL = 1024
HALO = SUBLANES_BF16

_BF16 = jnp.bfloat16
_F32 = jnp.float32
_HI = lax.Precision.HIGHEST
_NT = (((1,), (1,)), ((), ()))
_TN = (((0,), (0,)), ((), ()))


def _cparams(n_axes, sem=None):
    return pltpu.CompilerParams(
        dimension_semantics=sem or ("parallel",) * n_axes, vmem_limit_bytes=VMEM_LIMIT_BYTES)


def _norm_mod(x, g, shift, scale):
    ms = jnp.mean(x * x, axis=-1, keepdims=True)
    y = x * lax.rsqrt(ms + EPS) * g
    return y * (1.0 + scale) + shift


def _silu(x):
    return x * jax.nn.sigmoid(x)


ADA_ROWS = 16
ADA_TN = 1024


def _ada_kernel(c_ref, w_ref, b_ref, o_ref):
    o_ref[...] = jnp.dot(_silu(c_ref[...]), w_ref[...], precision=_HI, preferred_element_type=_F32) + b_ref[...]


def _ada_mod(cond, w, b):
    d, n = w.shape
    return pl.pallas_call(
        _ada_kernel,
        grid=(n // ADA_TN,),
        in_specs=[pl.BlockSpec((ADA_ROWS, d), lambda j: (0, 0)),
                  pl.BlockSpec((d, ADA_TN), lambda j: (0, j)),
                  pl.BlockSpec((1, ADA_TN), lambda j: (0, j))],
        out_specs=pl.BlockSpec((ADA_ROWS, ADA_TN), lambda j: (0, j)),
        out_shape=jax.ShapeDtypeStruct((ADA_ROWS, n), _F32),
        compiler_params=_cparams(1),
        name="ada_mod",
    )(cond, w, b[None, :])


def _in_proj_kernel(x_ref, c_ref, g_ref, sl_ref, cl_ref, sc_ref, cc_ref, w_ref, *out_refs):
    is_ctx = pl.program_id(1) == 0
    x = jnp.where(is_ctx, c_ref[0], x_ref[0])
    shift = jnp.where(is_ctx, sc_ref[0], sl_ref[0])
    scale = jnp.where(is_ctx, cc_ref[0], cl_ref[0])
    h = _norm_mod(x, g_ref[...], shift, scale)
    p = jnp.dot(h.astype(_BF16), w_ref[...], preferred_element_type=_F32)
    off = 0
    for o_ref, n in zip(out_refs, IN_GROUPS):
        o_ref[0] = p[:, off:off + n]
        off += n


def _in_proj(x, ctx, g, shift_l, scale_l, shift_c, scale_c, w_pad):
    b, l, d = x.shape
    nt = 1 + l // TOK
    vec_l = pl.BlockSpec((1, 1, d), lambda i, j: (i, 0, 0))
    vec_c = pl.BlockSpec((1, 1, d), lambda i, j: (0, 0, 0))
    return pl.pallas_call(
        _in_proj_kernel,
        grid=(b, nt),
        in_specs=[
            pl.BlockSpec((1, TOK, d), lambda i, j: (i, jnp.maximum(j - 1, 0), 0)),
            pl.BlockSpec((1, TOK, d), lambda i, j: (i, 0, 0)),
            pl.BlockSpec((1, d), lambda i, j: (0, 0)),
            vec_l, vec_l, vec_c, vec_c,
            pl.BlockSpec((d, IN_TOTAL), lambda i, j: (0, 0)),
        ],
        out_specs=[pl.BlockSpec((1, TOK, n), lambda i, j: (i, j, 0)) for n in IN_GROUPS],
        out_shape=[jax.ShapeDtypeStruct((b, nt * TOK, n), _F32) for n in IN_GROUPS],
        compiler_params=_cparams(2),
        name="in_proj",
    )(x, ctx, g, shift_l, scale_l, shift_c, scale_c, w_pad)


def _pad_w_in(w_in):
    s1 = GDN_QKV + GDN_WIDTH
    s2 = GDN_IN
    s3 = GDN_IN + MLA_IN
    d = w_in.shape[0]
    z = lambda n: jnp.zeros((d, n), w_in.dtype)
    parts = [w_in[:, :s1], w_in[:, s1:s2], z(AB_PAD - N_GATE),
             w_in[:, s2:s3], z(MLA_PAD - MLA_IN), w_in[:, s3:]]
    return jnp.concatenate(parts, axis=1).astype(_BF16)


def _gdn_consts():
    r = np.arange(TOK)
    same = (r[:, None] // GDN_CHUNK) == (r[None, :] // GDN_CHUNK)
    tril = (same & (r[None, :] <= r[:, None])).astype(np.float32)
    triu = (same & (r[None, :] >= r[:, None])).astype(np.float32)
    c = np.arange(GDN_WIDTH)
    head_ones = (c[:, None] // GDN_DK == c[None, :] // GDN_DK).astype(np.float32)
    expand = np.zeros((LANES, 4 * GDN_WIDTH), np.float32)
    for k in range(4):
        for h in range(GDN_HEADS):
            expand[k * GDN_HEADS + h, k * GDN_WIDTH + h * GDN_DK:k * GDN_WIDTH + (h + 1) * GDN_DK] = 1.0
    return tril, triu, head_ones, expand


def _gdn_prep_consts():
    tril, triu, head_ones, expand = _gdn_consts()
    b = lambda a: jnp.asarray(a, _BF16)
    return b(tril), b(triu), b(np.concatenate([head_ones] * 2, axis=0)), b(np.concatenate([expand] * 3, axis=0))


def _split_bf16(x, n):
    terms = []
    for _ in range(n):
        t = x.astype(_BF16)
        terms.append(t)
        x = x - t.astype(_F32)
    return terms


def _sum_terms(y, n):
    w = y.shape[1] // n
    out = y[:, :w]
    for t in range(1, n):
        out = out + y[:, t * w:(t + 1) * w]
    return out


def _gdn_prep_kernel(xp_ref, x_ref, xn_ref, ab_ref, cw_ref, alog_ref, dt_ref, tril_ref, triu_ref, hones_ref,
                     exp_ref, q_ref, k_ref, v_ref, gf_ref, gb_ref, bf_ref, bb_ref, xe_ref):
    j = pl.program_id(1)
    nt = pl.num_programs(1)
    pv = (j >= 2).astype(_F32)
    nv = jnp.logical_and(j >= 1, j < nt - 1).astype(_F32)
    xe_ref[0:SUBLANES] = xp_ref[0] * pv
    xe_ref[SUBLANES:SUBLANES + TOK] = x_ref[0]
    xe_ref[SUBLANES + TOK:] = xn_ref[0] * nv
    cw = cw_ref[...]
    y = (xe_ref[SUBLANES - 1:SUBLANES - 1 + TOK] * cw[0:1] + x_ref[0] * cw[1:2]
         + xe_ref[SUBLANES + 1:SUBLANES + 1 + TOK] * cw[2:3])
    y = _silu(y)
    hk = GDN_HEADS * GDN_DK
    q, k, v = y[:, :hk], y[:, hk:2 * hk], y[:, 2 * hk:]
    sq = jnp.concatenate([q * q, k * k], axis=0)
    ss = jnp.dot(jnp.concatenate(_split_bf16(sq, 2), axis=1), hones_ref[...], preferred_element_type=_F32)
    q_ref[0] = q * lax.rsqrt(ss[:TOK] + EPS) * (GDN_DK ** -0.5)
    k_ref[0] = k * lax.rsqrt(ss[TOK:] + EPS)
    v_ref[0] = v

    ab = ab_ref[0]
    lane = lax.broadcasted_iota(jnp.int32, ab.shape, 1)
    a_in = ab + dt_ref[...]
    softplus = jnp.maximum(a_in, 0.0) + jnp.log(1.0 + jnp.exp(-jnp.abs(a_in)))
    g = jnp.where(lane < 2 * GDN_HEADS, -jnp.exp(alog_ref[...]) * softplus, 0.0)
    g3 = jnp.concatenate(_split_bf16(g, 3), axis=1)
    gc_f = _sum_terms(jnp.dot(tril_ref[...], g3, preferred_element_type=_F32), 3)
    gc_b = _sum_terms(jnp.dot(triu_ref[...], g3, preferred_element_type=_F32), 3)
    cols = jnp.where(lane < GDN_HEADS, gc_f, jnp.where(lane < 2 * GDN_HEADS, gc_b, jax.nn.sigmoid(ab)))
    wide = jnp.dot(jnp.concatenate(_split_bf16(cols, 3), axis=1), exp_ref[...],
                   preferred_element_type=_F32)
    gf_ref[0] = wide[:, 0:GDN_WIDTH]
    gb_ref[0] = wide[:, GDN_WIDTH:2 * GDN_WIDTH]
    bf_ref[0] = wide[:, 2 * GDN_WIDTH:3 * GDN_WIDTH]
    bb_ref[0] = wide[:, 3 * GDN_WIDTH:]


def _gdn_prep(qkv, ab, conv_w, a_log_row, dt_row):
    b, lt, _ = qkv.shape
    nt = lt // TOK
    nh = TOK // SUBLANES
    last = lt // SUBLANES - 1
    consts = list(_gdn_prep_consts())
    whole = lambda a: pl.BlockSpec(a.shape, lambda i, j: (0,) * a.ndim)
    row = lambda n: pl.BlockSpec((1, TOK, n), lambda i, j: (i, j, 0))
    return pl.pallas_call(
        _gdn_prep_kernel,
        grid=(b, nt),
        in_specs=[
            pl.BlockSpec((1, SUBLANES, GDN_QKV), lambda i, j: (i, jnp.maximum(j * nh - 1, 0), 0)),
            row(GDN_QKV),
            pl.BlockSpec((1, SUBLANES, GDN_QKV), lambda i, j: (i, jnp.minimum((j + 1) * nh, last), 0)),
            row(AB_PAD), whole(conv_w), whole(a_log_row), whole(dt_row),
        ] + [whole(a) for a in consts],
        out_specs=[row(GDN_WIDTH)] * 7,
        out_shape=[jax.ShapeDtypeStruct((b, lt, GDN_WIDTH), _F32)] * 7,
        scratch_shapes=[pltpu.VMEM((TOK + 2 * SUBLANES, GDN_QKV), _F32)],
        compiler_params=_cparams(2),
        name="gdn_prep",
    )(qkv, qkv, qkv, ab, conv_w, a_log_row, dt_row, *consts)


def _block_diag(z, left):
    return jnp.concatenate([jnp.where(left, z, 0.0), jnp.where(left, 0.0, z)], axis=0).astype(_BF16)


def _mm(a, b):
    return jnp.dot(a.astype(_BF16), b, preferred_element_type=_F32)


def _gdn_chunk(q, k, v, gx, bx, s, backward, masks):
    left, eye2, incl, strict, ones64, diag_blocks = masks
    c = GDN_CHUNK
    yk = _block_diag(k, left)
    qk_kk = lax.dot_general(jnp.concatenate([q, k], axis=0).astype(_BF16), yk, _NT, preferred_element_type=_F32)
    qk, kk = qk_kk[:c], qk_kk[c:]
    d0 = jnp.where(eye2, gx, 0.0)
    t_hi = d0.astype(_BF16)
    r1 = d0 - t_hi.astype(_F32)
    t_mid = r1.astype(_BF16)
    t_lo = (r1 - t_mid.astype(_F32)).astype(_BF16)
    r3 = jnp.dot(ones64, jnp.concatenate([t_hi, t_mid, t_lo], axis=1), preferred_element_type=_F32)
    r = r3[:, :LANES] + r3[:, LANES:2 * LANES] + r3[:, 2 * LANES:]
    dec = jnp.where(incl, jnp.exp(jnp.where(incl, gx - r, 0.0)), 0.0)
    a = bx * kk * jnp.where(strict, dec, 0.0)
    qk = qk * dec
    eg = jnp.exp(gx)
    p = jnp.where(eye2, 1.0, 0.0) - a
    pw = a
    pw_bd = _block_diag(pw, left)
    for _ in range(5):
        pw = _mm(pw, pw_bd)
        pw_bd = _block_diag(pw, left)
        p = p + _mm(p, pw_bd)
    rhs = jnp.concatenate([_block_diag(bx * v, left), _block_diag(bx * k * eg, left)], axis=1)
    uw = _mm(p, rhs)
    u, w = uw[:, :LANES], uw[:, LANES:]
    qd = q * eg
    tot = gx[0:1] if backward else gx[c - 1:c]
    kd = k * jnp.exp(tot - gx)
    sb = s.astype(_BF16)
    ws_qs = _mm(jnp.concatenate([w, qd], axis=0), sb)
    delta = u - ws_qs[:c]
    o = ws_qs[c:] + _mm(qk, _block_diag(delta, left))
    upd = lax.dot_general(kd.astype(_BF16), delta.astype(_BF16), _TN, preferred_element_type=_F32)
    s = s * jnp.exp(tot) + jnp.where(diag_blocks, upd, 0.0)
    return o, s


def _gdn_scan_kernel(cq, ck, cv, cgf, cbf, cgb, cbb, fq, fk, fv, fg, fb, rq, rk, rv, rg, rb,
                     ocf_ref, ocb_ref, of_ref, ob_ref, sf_ref, sb_ref):
    step = pl.program_id(1)
    is_ctx = step == 0

    @pl.when(is_ctx)
    def _():
        sf_ref[...] = jnp.zeros_like(sf_ref)
        sb_ref[...] = jnp.zeros_like(sb_ref)

    c = GDN_CHUNK
    li = lax.broadcasted_iota(jnp.int32, (c, LANES), 1)
    ri = lax.broadcasted_iota(jnp.int32, (c, LANES), 0)
    lj = li & (c - 1)
    left = li < c
    eye2 = lj == ri
    ones64 = jnp.ones((c, c), _BF16)
    r2 = lax.broadcasted_iota(jnp.int32, (LANES, LANES), 0)
    c2 = lax.broadcasted_iota(jnp.int32, (LANES, LANES), 1)
    diag_blocks = (r2 // c) == (c2 // c)
    masks_f = (left, eye2, ri >= lj, ri > lj, ones64, diag_blocks)
    masks_b = (left, eye2, ri <= lj, ri < lj, ones64, diag_blocks)

    def run(refs, ctx_refs, s_ref, o_ref, oc_ref, backward, masks):
        vals = [jnp.where(is_ctx, cr[0], r[0]) for cr, r in zip(ctx_refs, refs)]
        order = range(TOK // c - 1, -1, -1) if backward else range(TOK // c)
        outs = {}
        for p in range(GDN_PAIRS):
            s = s_ref[p]
            for n in order:
                blk = [a[n * c:(n + 1) * c, p * LANES:(p + 1) * LANES] for a in vals]
                o, s = _gdn_chunk(*blk, s, backward, masks)
                outs[(n, p)] = o
            s_ref[p] = s
        o_all = jnp.concatenate(
            [jnp.concatenate([outs[(n, p)] for p in range(GDN_PAIRS)], axis=1) for n in range(TOK // c)], axis=0)

        @pl.when(is_ctx)
        def _():
            oc_ref[0] = o_all

        @pl.when(jnp.logical_not(is_ctx))
        def _():
            o_ref[0] = o_all

    run((fq, fk, fv, fg, fb), (cq, ck, cv, cgf, cbf), sf_ref, of_ref, ocf_ref, False, masks_f)
    run((rq, rk, rv, rg, rb), (cq, ck, cv, cgb, cbb), sb_ref, ob_ref, ocb_ref, True, masks_b)


def _gdn_intra(probs, masks):
    left, eye2, ones64, tri, same_blk = masks
    c = GDN_CHUNK
    qs, ks, vs, gxs, bxs, bws = zip(*probs)
    qk_kk = [lax.dot_general(jnp.concatenate([q, k], axis=0).astype(_BF16), _block_diag(k, left), _NT,
                             preferred_element_type=_F32) for q, k in zip(qs, ks)]
    rs = []
    for gx in gxs:
        d0 = jnp.where(eye2, gx, 0.0)
        t_hi = d0.astype(_BF16)
        r1 = d0 - t_hi.astype(_F32)
        t_mid = r1.astype(_BF16)
        t_lo = (r1 - t_mid.astype(_F32)).astype(_BF16)
        r3 = jnp.dot(ones64, jnp.concatenate([t_hi, t_mid, t_lo], axis=1), preferred_element_type=_F32)
        rs.append(r3[:, :LANES] + r3[:, LANES:2 * LANES] + r3[:, 2 * LANES:])
    a_s, qkm, egs = [], [], []
    for x, gx, bx, r, bw in zip(qk_kk, gxs, bxs, rs, bws):
        incl, strict = tri[bw]
        dec = jnp.where(incl, jnp.exp(jnp.where(incl, gx - r, 0.0)), 0.0)
        a_s.append(bx * x[c:] * jnp.where(strict, dec, 0.0))
        qkm.append(x[:c] * dec)
        egs.append(jnp.exp(gx))
    eye_f = jnp.where(eye2, 1.0, 0.0)
    base = same_blk[8]
    d1 = [jnp.where(base, a, 0.0) for a in a_s]
    ps = [eye_f - d for d in d1]
    d2 = [_mm(d, _block_diag(d, left)) for d in d1]
    d2_bd = [_block_diag(d, left) for d in d2]
    ps = [p + _mm(p, bd) for p, bd in zip(ps, d2_bd)]
    d4 = [_mm(d, bd) for d, bd in zip(d2, d2_bd)]
    ps = [p + _mm(p, _block_diag(d, left)) for p, d in zip(ps, d4)]
    for blk in (8, 16, 32):
        off = jnp.logical_and(same_blk[2 * blk], jnp.logical_not(same_blk[blk]))
        t1 = [_mm(p, _block_diag(jnp.where(off, a, 0.0), left)) for p, a in zip(ps, a_s)]
        ps = [p - _mm(t, _block_diag(p, left)) for p, t in zip(ps, t1)]
    out = []
    for p, q, k, v, gx, bx, eg, qk, bw in zip(ps, qs, ks, vs, gxs, bxs, egs, qkm, bws):
        tot = gx[0:1] if bw else gx[c - 1:c]
        lhs = jnp.concatenate([k * eg, q * eg], axis=0).astype(_BF16)
        out.append((p, lhs, bx, bx * v, qk, k * jnp.exp(tot - gx), tot))
    return out


def _gdn_state_step(chains, left, diag_blocks):
    c = GDN_CHUNK
    ys = [jnp.dot(x[1], s.astype(_BF16), preferred_element_type=_F32) for s, x in chains]
    resid = [x[3] - x[2] * y[:c] for (s, x), y in zip(chains, ys)]
    deltas = [_mm(x[0], _block_diag(r, left)) for (s, x), r in zip(chains, resid)]
    os_ = [y[c:] + _mm(x[4], _block_diag(d, left)) for (s, x), y, d in zip(chains, ys, deltas)]
    upds = [lax.dot_general(x[5].astype(_BF16), d.astype(_BF16), _TN, preferred_element_type=_F32)
            for (s, x), d in zip(chains, deltas)]
    new_s = [s * jnp.exp(x[6]) + jnp.where(diag_blocks, u, 0.0) for (s, x), u in zip(chains, upds)]
    return list(zip(os_, new_s))


def _gdn_scan_kernel_bf(cq, ck, cv, cgf, cbf, cgb, cbb, fq, fk, fv, fg, fb, rq, rk, rv, rg, rb,
                        ocf_ref, ocb_ref, of_ref, ob_ref, sf_ref, sb_ref):
    step = pl.program_id(1)
    is_ctx = step == 0

    @pl.when(is_ctx)
    def _():
        sf_ref[...] = jnp.zeros_like(sf_ref)
        sb_ref[...] = jnp.zeros_like(sb_ref)

    c = GDN_CHUNK
    nc = TOK // c
    li = lax.broadcasted_iota(jnp.int32, (c, LANES), 1)
    ri = lax.broadcasted_iota(jnp.int32, (c, LANES), 0)
    lj = li & (c - 1)
    left = li < c
    eye2 = lj == ri
    ones64 = jnp.ones((c, c), _BF16)
    r2 = lax.broadcasted_iota(jnp.int32, (LANES, LANES), 0)
    c2 = lax.broadcasted_iota(jnp.int32, (LANES, LANES), 1)
    diag_blocks = (r2 // c) == (c2 // c)
    tri = {False: (ri >= lj, ri > lj), True: (ri <= lj, ri < lj)}
    same_blk = {b: (ri // b) == (lj // b) for b in (8, 16, 32, 64)}
    masks = (left, eye2, ones64, tri, same_blk)

    nbb = fq.shape[0]
    ctx_refs = {False: (cq, ck, cv, cgf, cbf), True: (cq, ck, cv, cgb, cbb)}
    lat_refs = {False: (fq, fk, fv, fg, fb), True: (rq, rk, rv, rg, rb)}
    vals = {(bw, e): [jnp.where(is_ctx, cr[e], r[e]) for cr, r in zip(ctx_refs[bw], lat_refs[bw])]
            for bw in (False, True) for e in range(nbb)}
    chain_keys = [(bw, e, p) for bw in (False, True) for e in range(nbb) for p in range(GDN_PAIRS)]
    prob_keys = [(bw, e, p, n) for bw, e, p in chain_keys for n in range(nc)]
    probs = [tuple(a[n * c:(n + 1) * c, p * LANES:(p + 1) * LANES] for a in vals[(bw, e)]) + (bw,)
             for bw, e, p, n in prob_keys]
    intra = dict(zip(prob_keys, _gdn_intra(probs, masks)))

    state_ref = lambda bw: sb_ref if bw else sf_ref
    states = {(bw, e, p): state_ref(bw)[e, p] for bw, e, p in chain_keys}
    outs = {}
    for t in range(nc):
        ns = {kk: (nc - 1 - t if kk[0] else t) for kk in chain_keys}
        res = _gdn_state_step([(states[kk], intra[kk + (ns[kk],)]) for kk in chain_keys], left, diag_blocks)
        for kk, (o, st) in zip(chain_keys, res):
            outs[kk + (ns[kk],)] = o
            states[kk] = st
    for bw, e, p in chain_keys:
        state_ref(bw)[e, p] = states[(bw, e, p)]

    for bw, o_ref, oc_ref in ((False, of_ref, ocf_ref), (True, ob_ref, ocb_ref)):
        for e in range(nbb):
            o_all = jnp.concatenate(
                [jnp.concatenate([outs[(bw, e, p, n)] for p in range(GDN_PAIRS)], axis=1) for n in range(nc)], axis=0)

            @pl.when(is_ctx)
            def _(o_all=o_all, oc_ref=oc_ref, e=e):
                oc_ref[e] = o_all

            @pl.when(jnp.logical_not(is_ctx))
            def _(o_all=o_all, o_ref=o_ref, e=e):
                o_ref[e] = o_all


def _gdn_scan(q, k, v, gf, gb, bf, bb):
    b, lt, w = q.shape
    nl = lt // TOK - 1
    nbb = GDN_BB
    ctx = pl.BlockSpec((nbb, TOK, w), lambda i, s: (i, 0, 0))
    fwd = pl.BlockSpec((nbb, TOK, w), lambda i, s: (i, jnp.maximum(s, 1), 0))
    bwd = pl.BlockSpec((nbb, TOK, w), lambda i, s: (i, nl + 1 - jnp.maximum(s, 1), 0))
    fwd_o = pl.BlockSpec((nbb, TOK, w), lambda i, s: (i, jnp.maximum(s, 1) - 1, 0))
    bwd_o = pl.BlockSpec((nbb, TOK, w), lambda i, s: (i, nl - jnp.maximum(s, 1), 0))
    return pl.pallas_call(
        _gdn_scan_kernel_bf,
        grid=(b // nbb, nl + 1),
        in_specs=[ctx] * 7 + [fwd] * 5 + [bwd] * 5,
        out_specs=[ctx, ctx, fwd_o, bwd_o],
        out_shape=[jax.ShapeDtypeStruct((b, TOK, w), _F32)] * 2 + [jax.ShapeDtypeStruct((b, nl * TOK, w), _F32)] * 2,
        scratch_shapes=[pltpu.VMEM((nbb, GDN_PAIRS, LANES, LANES), _F32)] * 2,
        compiler_params=_cparams(2, ("parallel", "arbitrary")),
        name="gdn_scan",
    )(q, k, v, gf, bf, gb, bb, q, k, v, gf, bf, q, k, v, gb, bb)


MLA_HEAD_PAD = LANES
MLA_WIDE = MLA_HEADS * MLA_HEAD_PAD
ATTN_TQ = 512
ATTN_TK = 512


def _rope_tables(n_ctx, seq):
    rows = seq // GRID_W
    row = np.repeat(np.arange(rows, dtype=np.float64), GRID_W)
    col = np.tile(np.arange(GRID_W, dtype=np.float64), rows)
    inv = ROPE_BASE ** (-np.arange(0, ROPE_AXIS, 2, dtype=np.float64) / ROPE_AXIS)
    ang = np.concatenate([row[:, None] * inv, col[:, None] * inv], axis=-1)
    cos, sin = np.cos(ang), np.sin(ang)
    half = MLA_ROPE // 2
    c = np.ones((n_ctx + seq, MLA_HEAD_PAD))
    s = np.zeros((n_ctx + seq, MLA_HEAD_PAD))
    c[n_ctx:, MLA_NOPE:MLA_NOPE + half] = cos
    c[n_ctx:, MLA_NOPE + half:MLA_QK] = cos
    s[n_ctx:, MLA_NOPE:MLA_NOPE + half] = -sin
    s[n_ctx:, MLA_NOPE + half:MLA_QK] = sin
    return c.astype(np.float32), s.astype(np.float32)


def _mla_prep_kernel(p_ref, qn_ref, kvn_ref, wq_ref, wk_ref, wv_ref, qg_ref, kg_ref, cos_ref, sin_ref,
                     qc_ref, ql_ref, k_ref, v_ref):
    p = p_ref[0]
    cq = p[:, :MLA_Q_RANK]
    ckv = p[:, MLA_Q_RANK:MLA_Q_RANK + MLA_KV_RANK]
    kr = p[:, MLA_Q_RANK + MLA_KV_RANK:]
    cq = (cq * lax.rsqrt(jnp.mean(cq * cq, axis=-1, keepdims=True) + EPS) * qn_ref[...]).astype(_BF16)
    ckv = (ckv * lax.rsqrt(jnp.mean(ckv * ckv, axis=-1, keepdims=True) + EPS) * kvn_ref[...]).astype(_BF16)
    q = jnp.dot(cq, wq_ref[...], preferred_element_type=_F32)
    k = jnp.dot(jnp.concatenate([ckv, kr.astype(_BF16)], axis=1), wk_ref[...], preferred_element_type=_F32)
    lane = lax.broadcasted_iota(jnp.int32, (TOK, MLA_WIDE), 1) & (MLA_HEAD_PAD - 1)
    v = jnp.dot(ckv, wv_ref[...], preferred_element_type=_F32) + jnp.where(lane == MLA_V, 1.0, 0.0)
    cos = jnp.concatenate([cos_ref[...]] * MLA_HEADS, axis=1)
    sin = jnp.concatenate([sin_ref[...]] * MLA_HEADS, axis=1)
    half = MLA_ROPE // 2
    first = jnp.logical_and(lane >= MLA_NOPE, lane < MLA_NOPE + half)
    second = jnp.logical_and(lane >= MLA_NOPE + half, lane < MLA_QK)

    def head_norm_rope(x, g):
        parts = []
        for h in range(MLA_HEADS):
            xh = x[:, h * MLA_HEAD_PAD:(h + 1) * MLA_HEAD_PAD]
            ms = jnp.sum(xh * xh, axis=-1, keepdims=True) * (1.0 / MLA_QK)
            parts.append(xh * lax.rsqrt(ms + EPS))
        xn = jnp.concatenate(parts, axis=1) * g
        up = pltpu.roll(xn, half, 1)
        down = pltpu.roll(xn, MLA_WIDE - half, 1)
        swapped = jnp.where(first, down, jnp.where(second, up, 0.0))
        return xn * cos + swapped * sin

    qf = head_norm_rope(q, qg_ref[...]) * (MLA_QK ** -0.5 * math.log2(math.e))
    kf = head_norm_rope(k, kg_ref[...])
    is_ctx = pl.program_id(1) == 0
    for h in range(MLA_HEADS):
        sl = slice(h * MLA_HEAD_PAD, (h + 1) * MLA_HEAD_PAD)
        k_ref[0, h] = kf[:, sl].astype(_BF16)
        v_ref[0, h] = v[:, sl].astype(_BF16)

    @pl.when(is_ctx)
    def _():
        for h in range(MLA_HEADS):
            qc_ref[0, h] = qf[:, h * MLA_HEAD_PAD:(h + 1) * MLA_HEAD_PAD].astype(_BF16)

    @pl.when(jnp.logical_not(is_ctx))
    def _():
        for h in range(MLA_HEADS):
            ql_ref[0, h] = qf[:, h * MLA_HEAD_PAD:(h + 1) * MLA_HEAD_PAD].astype(_BF16)


def _mla_weights(w_uq, w_ukv, q_head_g, k_head_g):
    pad = MLA_HEAD_PAD
    wq = jnp.pad(w_uq.reshape(MLA_Q_RANK, MLA_HEADS, MLA_QK), ((0, 0), (0, 0), (0, pad - MLA_QK)))
    wkv = w_ukv.reshape(MLA_KV_RANK, MLA_HEADS, MLA_NOPE + MLA_V)
    wk = jnp.pad(wkv[:, :, :MLA_NOPE], ((0, 0), (0, 0), (0, pad - MLA_NOPE)))
    wv = jnp.pad(wkv[:, :, MLA_NOPE:], ((0, 0), (0, 0), (0, pad - MLA_V)))
    sel = np.zeros((MLA_PAD - MLA_Q_RANK - MLA_KV_RANK, MLA_WIDE), np.float32)
    for h in range(MLA_HEADS):
        for r in range(MLA_ROPE):
            sel[r, h * pad + MLA_NOPE + r] = 1.0
    tile_g = lambda g: jnp.tile(jnp.pad(g, (0, pad - MLA_QK)), MLA_HEADS)[None, :]
    flat = lambda w: w.reshape(w.shape[0], MLA_WIDE).astype(_BF16)
    wk_sel = jnp.concatenate([flat(wk), jnp.asarray(sel, _BF16)], axis=0)
    return flat(wq), wk_sel, flat(wv), tile_g(q_head_g), tile_g(k_head_g)


def _mla_prep(mla_in, q_norm_g, kv_norm_g, weights, seq):
    b, lt, _ = mla_in.shape
    nt = lt // TOK
    wq, wk, wv, qg, kg = weights
    cos, sin = (jnp.asarray(t) for t in _rope_tables(lt - seq, seq))
    whole = lambda a: pl.BlockSpec(a.shape, lambda i, j: (0,) * a.ndim)
    tab = pl.BlockSpec((TOK, MLA_HEAD_PAD), lambda i, j: (j, 0))
    hd = lambda f: pl.BlockSpec((1, MLA_HEADS, TOK, MLA_HEAD_PAD), f)
    qn, kvn = q_norm_g[None, :], kv_norm_g[None, :]
    shp = lambda t: jax.ShapeDtypeStruct((b, MLA_HEADS, t, MLA_HEAD_PAD), _BF16)
    return pl.pallas_call(
        _mla_prep_kernel,
        grid=(b, nt),
        in_specs=[pl.BlockSpec((1, TOK, MLA_PAD), lambda i, j: (i, j, 0)), whole(qn), whole(kvn),
                  whole(wq), whole(wk), whole(wv), whole(qg), whole(kg), tab, tab],
        out_specs=[hd(lambda i, j: (i, 0, 0, 0)), hd(lambda i, j: (i, 0, jnp.maximum(j, 1) - 1, 0)),
                   hd(lambda i, j: (i, 0, j, 0)), hd(lambda i, j: (i, 0, j, 0))],
        out_shape=[shp(TOK), shp(lt - TOK), shp(lt), shp(lt)],
        compiler_params=_cparams(2, ("parallel", "arbitrary")),
        name="mla_prep",
    )(mla_in, qn, kvn, wq, wk, wv, qg, kg, cos, sin)


def _attn_kernel(q_ref, k_ref, v_ref, o_ref):
    n_keys = k_ref.shape[2]
    starts = list(range(0, n_keys, ATTN_TK))
    heads = range(2)

    def scores(h, lo):
        hi = min(lo + ATTN_TK, n_keys)
        return lax.dot_general(q_ref[0, h], k_ref[0, h, lo:hi, :], _NT, preferred_element_type=_F32)

    nxt = [scores(h, starts[0]) for h in heads]
    m = [None, None]
    acc = [None, None]
    for n, lo in enumerate(starts):
        cur = nxt
        if n + 1 < len(starts):
            nxt = [scores(h, starts[n + 1]) for h in heads]
        hi = min(lo + ATTN_TK, n_keys)
        for h in heads:
            s = cur[h]
            m_blk = jnp.max(s, axis=-1, keepdims=True)
            m_new = m_blk if n == 0 else jnp.maximum(m[h], m_blk)
            p = jnp.exp2(s - m_new).astype(_BF16)
            pv = jnp.dot(p, v_ref[0, h, lo:hi, :], preferred_element_type=_F32)
            acc[h] = pv if n == 0 else acc[h] * jnp.exp2(m[h] - m_new) + pv
            m[h] = m_new
    o_ref[0] = jnp.concatenate([a[:, :MLA_V] / a[:, MLA_V:MLA_V + 1] for a in acc], axis=1)


def _attention(q, k, v, n_keys, tq):
    b, h, t, w = q.shape
    return pl.pallas_call(
        _attn_kernel,
        grid=(b, h // 2, t // tq),
        in_specs=[pl.BlockSpec((1, 2, tq, w), lambda i, j, l: (i, j, l, 0)),
                  pl.BlockSpec((1, 2, n_keys, w), lambda i, j, l: (i, j, 0, 0)),
                  pl.BlockSpec((1, 2, n_keys, w), lambda i, j, l: (i, j, 0, 0))],
        out_specs=pl.BlockSpec((1, tq, 2 * MLA_V), lambda i, j, l: (i, l, j)),
        out_shape=jax.ShapeDtypeStruct((b, t, h * MLA_V), _F32),
        compiler_params=_cparams(3),
        name="mla_attention",
    )(q, k, v)


HY_BLK = 256
HY_LO = 128
HY_CB = 16


def _hy_prep_kernel(xp_ref, x_ref, xn_ref, cw_ref, cb_ref, x0_ref, z_ref, zc_ref, zl_ref, xe_ref):
    j = pl.program_id(1)
    nt = pl.num_programs(1)
    pv = (j >= 2).astype(_F32)
    nv = jnp.logical_and(j >= 1, j < nt - 1).astype(_F32)
    xe_ref[0:SUBLANES] = xp_ref[0] * pv
    xe_ref[SUBLANES:SUBLANES + TOK] = x_ref[0]
    xe_ref[SUBLANES + TOK:] = xn_ref[0] * nv
    cw = cw_ref[...]
    u = (xe_ref[SUBLANES - 1:SUBLANES - 1 + TOK] * cw[0:1] + x_ref[0] * cw[1:2]
         + xe_ref[SUBLANES + 1:SUBLANES + 1 + TOK] * cw[2:3] + cb_ref[...])
    x0_ref[0] = u[:, :HY_CH]
    z = u[:, 2 * HY_CH:] * u[:, HY_CH:2 * HY_CH]
    z_ref[0] = z

    @pl.when(j == 0)
    def _():
        zc_ref[0] = z.astype(_BF16)

    @pl.when(j > 0)
    def _():
        zl_ref[0] = z.astype(_BF16)


def _hy_prep(hy_in, conv_w, conv_b):
    b, lt, _ = hy_in.shape
    nh = TOK // SUBLANES
    last = lt // SUBLANES - 1
    whole = lambda a: pl.BlockSpec(a.shape, lambda i, j: (0,) * a.ndim)
    row = lambda n: pl.BlockSpec((1, TOK, n), lambda i, j: (i, j, 0))
    cb = conv_b[None, :]
    return pl.pallas_call(
        _hy_prep_kernel,
        grid=(b, lt // TOK),
        in_specs=[pl.BlockSpec((1, SUBLANES, HY_IN), lambda i, j: (i, jnp.maximum(j * nh - 1, 0), 0)),
                  row(HY_IN),
                  pl.BlockSpec((1, SUBLANES, HY_IN), lambda i, j: (i, jnp.minimum((j + 1) * nh, last), 0)),
                  whole(conv_w), whole(cb)],
        out_specs=[row(HY_CH), row(HY_CH),
                   pl.BlockSpec((1, TOK, HY_CH), lambda i, j: (i, 0, 0)),
                   pl.BlockSpec((1, TOK, HY_CH), lambda i, j: (i, jnp.maximum(j, 1) - 1, 0))],
        out_shape=[jax.ShapeDtypeStruct((b, lt, HY_CH), _F32)] * 2
        + [jax.ShapeDtypeStruct((b, TOK, HY_CH), _BF16), jax.ShapeDtypeStruct((b, lt - TOK, HY_CH), _BF16)],
        scratch_shapes=[pltpu.VMEM((TOK + 2 * SUBLANES, HY_IN), _F32)],
        compiler_params=_cparams(2, ("parallel", "arbitrary")),
        name="hyena_prep",
    )(hy_in, hy_in, hy_in, conv_w, cb)


def _hy_filter_consts(l):
    def emb(t):
        t = t.astype(np.float64)
        t_norm = t / max(l - 1, 1)
        bands = np.linspace(1e-4, HY_BANDS - 1, HY_BANDS)
        ang = 2.0 * math.pi * t[:, None] * bands[None, :] / l
        z = np.concatenate([t_norm[:, None], np.cos(ang), np.sin(ang)], axis=-1)
        return np.pad(z, ((0, 0), (0, LANES - HY_EMB))), t_norm[:, None]
    r = np.arange(l)
    e_rev, tn_rev = emb(l - 1 - r)
    e_sh, tn_sh = emb(r + 1)
    deltas = np.abs(np.linspace(HY_MIN_DECAY, HY_MAX_DECAY, HY_CH))[None, :]
    f = lambda a: np.asarray(a, np.float32)
    return f(e_rev), f(e_sh), f(tn_rev), f(tn_sh), f(deltas)


def _hy_filter_kernel(er_ref, es_ref, tr_ref, ts_ref, dl_ref, w1_ref, b1_ref, w2_ref, b2_ref, w3_ref, b3_ref, o_ref):
    l = er_ref.shape[0]

    def mlp(e, col):
        h = jnp.sin(jnp.dot(e, w1_ref[...], precision=_HI, preferred_element_type=_F32) + b1_ref[...])
        h = jnp.sin(jnp.dot(h, w2_ref[...], precision=_HI, preferred_element_type=_F32) + b2_ref[...])
        return (jnp.dot(h, w3_ref[:, col * HY_CH:(col + 1) * HY_CH], precision=_HI, preferred_element_type=_F32)
                + b3_ref[:, col * HY_CH:(col + 1) * HY_CH])

    hf = mlp(er_ref[...], 0) * jnp.exp(-tr_ref[...] * dl_ref[...])
    hb = mlp(es_ref[...], 1) * jnp.exp(-ts_ref[...] * dl_ref[...])
    row = lax.broadcasted_iota(jnp.int32, hb.shape, 0)
    hb = jnp.where(row < l - 1, hb, 0.0)
    norm = jnp.sum(jnp.abs(hf), axis=0, keepdims=True) + jnp.sum(jnp.abs(hb), axis=0, keepdims=True)
    o_ref[...] = jnp.transpose(jnp.concatenate([hf, hb], axis=0) / norm)


def _hy_filter(l, w1, b1, w2, b2, w3, b3):
    consts = [jnp.asarray(a) for a in _hy_filter_consts(l)]
    w1p = jnp.pad(w1, ((0, LANES - HY_EMB), (0, 0)))
    args = consts + [w1p, b1[None, :], w2, b2[None, :], w3, b3[None, :]]
    return pl.pallas_call(
        _hy_filter_kernel,
        out_shape=jax.ShapeDtypeStruct((HY_CH, 2 * l), _F32),
        compiler_params=pltpu.CompilerParams(vmem_limit_bytes=VMEM_LIMIT_BYTES),
        name="hyena_filter",
    )(*args)


def _hy_conv_kernel(f_ref, z_ref, y_ref, g_ref, *, nblk, nb):
    cols = z_ref.shape[2]
    lane = lax.broadcasted_iota(jnp.int32, (HY_BLK, cols), 1)
    lane_p = lax.broadcasted_iota(jnp.int32, (HY_BLK // 2, cols), 1)
    gs = 4 if nblk >= 4 else 1

    def build(ch, slot):
        rows = SUBLANES_BF16
        base = pltpu.roll(jnp.broadcast_to(f_ref[ch], (rows, f_ref.shape[2])), 1, 1, stride=1, stride_axis=0)
        base = base.astype(_BF16)
        g_ref[slot, 0:rows, :] = base
        packed = pltpu.bitcast(base, jnp.int32)
        for a in range(1, HY_LO // rows):
            g_ref[slot, a * rows:(a + 1) * rows, :] = pltpu.bitcast(pltpu.roll(packed, rows * a, 1), _BF16)

    def convolve(ch, slot):
        z = z_ref[ch]
        zp = pltpu.bitcast(z, jnp.int32)
        real = nb * nblk

        def shift(x, lanes, blocks, sign, fill):
            if blocks == 0:
                return x
            s = nb * blocks
            if sign > 0:
                return jnp.where(lanes >= s, pltpu.roll(x, s, 1), fill)
            return jnp.where(lanes < real - s, pltpu.roll(x, cols - s, 1), fill)

        acc = {}
        for sign in (1, -1):
            for a in range(-(-nblk // gs)):
                ds = [gs * a + r for r in range(gs) if gs * a + r < nblk and not (sign < 0 and gs * a + r == 0)]
                if not ds:
                    continue
                wins = []
                for d in ds:
                    o = HY_BLK * (nblk - sign * d)
                    wins += [g_ref[slot, :, o:o + HY_BLK], g_ref[slot, :, o - HY_LO:o - HY_LO + HY_BLK]]
                zs = pltpu.bitcast(shift(zp, lane_p, gs * a, sign, 0), _BF16)
                part = jnp.dot(jnp.concatenate(wins, axis=0), zs, preferred_element_type=_F32)
                for n, d in enumerate(ds):
                    key = (sign, d - gs * a)
                    blk = part[n * HY_BLK:(n + 1) * HY_BLK]
                    acc[key] = blk if key not in acc else acc[key] + blk
        y = None
        for (sign, r), v in acc.items():
            v = shift(v, lane, r, sign, 0.0)
            y = v if y is None else y + v
        y_ref[ch] = y

    n_ch = f_ref.shape[0]
    build(0, 0)

    def pair(it, carry):
        ch = 2 * it
        build(ch + 1, 1)
        convolve(ch, 0)
        build(jnp.minimum(ch + 2, n_ch - 1), 0)
        convolve(ch + 1, 1)
        return carry

    lax.fori_loop(0, n_ch // 2, pair, 0)


def _hy_conv(fline, z):
    b, l, c = z.shape
    nblk = l // HY_BLK
    cols = max(nblk * b, LANES)
    zall = z.reshape(b, nblk, HY_BLK, c).transpose(3, 2, 1, 0).reshape(c, HY_BLK, nblk * b)
    zall = jnp.pad(zall.astype(_BF16), ((0, 0), (0, 0), (0, cols - nblk * b)))
    y = pl.pallas_call(
        functools.partial(_hy_conv_kernel, nblk=nblk, nb=b),
        grid=(c // HY_CB,),
        in_specs=[pl.BlockSpec((HY_CB, 1, 2 * l), lambda i: (i, 0, 0)),
                  pl.BlockSpec((HY_CB, HY_BLK, cols), lambda i: (i, 0, 0))],
        out_specs=pl.BlockSpec((HY_CB, HY_BLK, cols), lambda i: (i, 0, 0)),
        out_shape=jax.ShapeDtypeStruct((c, HY_BLK, cols), _F32),
        scratch_shapes=[pltpu.VMEM((2, HY_LO, 2 * l), _BF16)],
        compiler_params=_cparams(1),
        name="hyena_conv",
    )(fline.reshape(c, 1, 2 * l), zall)
    return y[:, :, :nblk * b].reshape(c, HY_BLK, nblk, b).transpose(3, 2, 1, 0).reshape(b, l, c)


def _out_proj_kernel(x_ref, of_ref, ob_ref, z_ref, gn_ref, hones_ref, mla_ref, hx_ref, hz_ref, hy_ref, hd_ref,
                     ga_ref, w_ref, o_ref):
    o = of_ref[0] + ob_ref[0]
    ms = jnp.dot(jnp.concatenate(_split_bf16(o * o, 2), axis=1), hones_ref[...],
                 preferred_element_type=_F32) * (1.0 / GDN_DV)
    gdn = o * lax.rsqrt(ms + EPS) * gn_ref[...] * _silu(z_ref[0])
    hy = hx_ref[0] * (hy_ref[0] + hz_ref[0] * hd_ref[...])
    mix = jnp.concatenate([gdn, mla_ref[0], hy], axis=-1).astype(_BF16)
    y = jnp.dot(mix, w_ref[...], preferred_element_type=_F32)
    o_ref[0] = x_ref[0] + ga_ref[0] * y


def _out_proj(x, o_f, o_b, z, gn_row, mla, hx0, hz, hy, hd_row, ga, w_out, toff):
    b, l, d = x.shape
    hones = _gdn_prep_consts()[2]
    row = lambda n: pl.BlockSpec((1, TOK, n), lambda i, j: (i, j, 0))
    rowc = lambda n: pl.BlockSpec((1, TOK, n), lambda i, j: (i, j + toff, 0))
    whole = lambda a: pl.BlockSpec(a.shape, lambda i, j: (0,) * a.ndim)
    return pl.pallas_call(
        _out_proj_kernel,
        grid=(b, l // TOK),
        in_specs=[row(d), row(GDN_WIDTH), row(GDN_WIDTH), rowc(GDN_WIDTH), whole(gn_row), whole(hones),
                  row(MLA_WIDTH), rowc(HY_WIDTH), rowc(HY_WIDTH), row(HY_WIDTH), whole(hd_row),
                  pl.BlockSpec((1, 1, d), lambda i, j: (i, 0, 0)),
                  pl.BlockSpec((d, d), lambda i, j: (0, 0))],
        out_specs=row(d),
        out_shape=jax.ShapeDtypeStruct((b, l, d), _F32),
        compiler_params=_cparams(2),
        name="out_proj",
    )(x, o_f, o_b, z, gn_row, hones, mla, hx0, hz, hy, hd_row, ga, w_out)


def _ffn_kernel(xp_ref, x_ref, xn_ref, g_ref, sf_ref, cf_ref, gf_ref, wup_ref, cw_ref, cb_ref, wdn_ref,
                o_ref, h_ref, up_ref, uv_ref, act_ref, *, tl):
    i = pl.program_id(1)
    nt = pl.num_programs(1)
    g, sf, cf = g_ref[...], sf_ref[0], cf_ref[0]
    pv = (i > 0).astype(_F32)
    nv = (i < nt - 1).astype(_F32)
    h_ref[0:HALO] = (_norm_mod(xp_ref[0], g, sf, cf) * pv).astype(_BF16)
    h_ref[HALO:HALO + tl] = _norm_mod(x_ref[0], g, sf, cf).astype(_BF16)
    h_ref[HALO + tl:] = (_norm_mod(xn_ref[0], g, sf, cf) * nv).astype(_BF16)

    def up_proj(c, slot):
        lo = pl.multiple_of(c * FFN_CHUNK, FFN_CHUNK)
        up_ref[slot] = jnp.dot(h_ref[...], wup_ref[:, pl.ds(lo, FFN_CHUNK)], preferred_element_type=_F32)
        uv_ref[slot] = jnp.dot(h_ref[HALO:HALO + tl], wup_ref[:, pl.ds(D_FF + lo, FFN_CHUNK)],
                               preferred_element_type=_F32)

    def gate(c, slot):
        lo = pl.multiple_of(c * FFN_CHUNK, FFN_CHUNK)
        cw = cw_ref[:, pl.ds(lo, FFN_CHUNK)]
        cb = cb_ref[:, pl.ds(lo, FFN_CHUNK)]
        gt = (up_ref[slot, HALO - 1:HALO - 1 + tl] * cw[0:1] + up_ref[slot, HALO:HALO + tl] * cw[1:2]
              + up_ref[slot, HALO + 1:HALO + 1 + tl] * cw[2:3] + cb)
        act_ref[:, pl.ds(lo, FFN_CHUNK)] = (_silu(gt) * uv_ref[slot]).astype(_BF16)

    n_chunks = D_FF // FFN_CHUNK
    up_proj(0, 0)

    def pair(it, carry):
        c = 2 * it
        up_proj(c + 1, 1)
        gate(c, 0)
        up_proj(c + 2, 0)
        gate(c + 1, 1)
        return carry

    lax.fori_loop(0, (n_chunks - 1) // 2, pair, 0)
    assert n_chunks % 2 == 1
    gate(n_chunks - 1, 0)
    y = jnp.dot(act_ref[...], wdn_ref[...], preferred_element_type=_F32)
    o_ref[0] = x_ref[0] + gf_ref[0] * y


def _ffn(x, g, sf, cf, gf, w_up, conv_w, conv_b, w_down, tl):
    b, l, d = x.shape
    nh = tl // HALO
    last = l // HALO - 1
    vec = pl.BlockSpec((1, 1, d), lambda i, j: (i, 0, 0))
    whole = lambda a: pl.BlockSpec(a.shape, lambda i, j: (0,) * a.ndim)
    once = lambda a: pl.BlockSpec(a.shape, lambda i, j: (0,) * a.ndim, pipeline_mode=pl.Buffered(1))
    return pl.pallas_call(
        functools.partial(_ffn_kernel, tl=tl),
        grid=(b, l // tl),
        in_specs=[
            pl.BlockSpec((1, HALO, d), lambda i, j: (i, jnp.maximum(j * nh - 1, 0), 0)),
            pl.BlockSpec((1, tl, d), lambda i, j: (i, j, 0)),
            pl.BlockSpec((1, HALO, d), lambda i, j: (i, jnp.minimum((j + 1) * nh, last), 0)),
            pl.BlockSpec((1, d), lambda i, j: (0, 0)),
            vec, vec, vec,
            once(w_up), whole(conv_w), whole(conv_b), once(w_down),
        ],
        out_specs=pl.BlockSpec((1, tl, d), lambda i, j: (i, j, 0)),
        out_shape=jax.ShapeDtypeStruct((b, l, d), _F32),
        scratch_shapes=[
            pltpu.VMEM((tl + 2 * HALO, d), _BF16),
            pltpu.VMEM((2, tl + 2 * HALO, FFN_CHUNK), _F32),
            pltpu.VMEM((2, tl, FFN_CHUNK), _F32),
            pltpu.VMEM((tl, D_FF), _BF16),
        ],
        compiler_params=_cparams(2),
        name="conv_ffn",
    )(x, x, x, g, sf, cf, gf, w_up, conv_w, conv_b, w_down)


def rms_norm(x, g):
    xf = x.astype(jnp.float32)
    y = xf * lax.rsqrt(jnp.mean(xf * xf, axis=-1, keepdims=True) + EPS)
    return (y * g.astype(jnp.float32)).astype(x.dtype)


def dwconv3(x, w, b=None):
    xp = jnp.pad(x, ((0, 0), (1, 1), (0, 0)))
    y = xp[:, :-2] * w[0] + xp[:, 1:-1] * w[1] + xp[:, 2:] * w[2]
    return y if b is None else y + b


def axial_rope(l):
    rows = l // GRID_W
    row = jnp.repeat(jnp.arange(rows, dtype=jnp.float32), GRID_W)
    col = jnp.tile(jnp.arange(GRID_W, dtype=jnp.float32), rows)
    inv = ROPE_BASE ** (-jnp.arange(0, ROPE_AXIS, 2, dtype=jnp.float32) / ROPE_AXIS)
    ang = jnp.concatenate([row[:, None] * inv, col[:, None] * inv], axis=-1)
    return jnp.cos(ang), jnp.sin(ang)


def apply_rope(x, cos, sin):
    half = x.shape[-1] // 2
    x1, x2 = x[..., :half], x[..., half:]
    cos, sin = cos[None, :, None, :], sin[None, :, None, :]
    return jnp.concatenate([x1 * cos - x2 * sin, x1 * sin + x2 * cos], axis=-1)


def mla_heads(p, q_norm_g, w_uq, kv_norm_g, w_ukv, q_head_g, k_head_g, rope):
    b, l, _ = p.shape
    c_q = rms_norm(p[..., :MLA_Q_RANK], q_norm_g)
    c_kv = rms_norm(p[..., MLA_Q_RANK:MLA_Q_RANK + MLA_KV_RANK], kv_norm_g)
    k_rope = p[..., MLA_Q_RANK + MLA_KV_RANK:MLA_IN]
    q = (c_q @ w_uq).reshape(b, l, MLA_HEADS, MLA_QK)
    kv = (c_kv @ w_ukv).reshape(b, l, MLA_HEADS, MLA_NOPE + MLA_V)
    k = jnp.concatenate([kv[..., :MLA_NOPE],
                         jnp.broadcast_to(k_rope[:, :, None, :], (b, l, MLA_HEADS, MLA_ROPE))], axis=-1)
    v = kv[..., MLA_NOPE:]
    q = rms_norm(q, q_head_g)
    k = rms_norm(k, k_head_g)
    if rope is not None:
        cos, sin = rope
        q = jnp.concatenate([q[..., :MLA_NOPE], apply_rope(q[..., MLA_NOPE:], cos, sin)], axis=-1)
        k = jnp.concatenate([k[..., :MLA_NOPE], apply_rope(k[..., MLA_NOPE:], cos, sin)], axis=-1)
    return q, k, v


def softmax_attend(q, k, v, scale):
    s = jnp.einsum('bqhd,bkhd->bhqk', q, k).astype(jnp.float32) * scale
    p = jax.nn.softmax(s, axis=-1).astype(v.dtype)
    return jnp.einsum('bhqk,bkhd->bqhd', p, v)


def mla_mixer(p_ctx, p_lat, q_norm_g, w_uq, kv_norm_g, w_ukv, q_head_g, k_head_g, rope, with_ctx):
    qc, kc, vc = mla_heads(p_ctx, q_norm_g, w_uq, kv_norm_g, w_ukv, q_head_g, k_head_g, None)
    ql, kl, vl = mla_heads(p_lat, q_norm_g, w_uq, kv_norm_g, w_ukv, q_head_g, k_head_g, rope)
    scale = MLA_QK ** -0.5
    k_all = jnp.concatenate([kl, kc], axis=1)
    v_all = jnp.concatenate([vl, vc], axis=1)
    b, l = ql.shape[0], ql.shape[1]
    nb = l // ATTN_BLOCK
    q_blocks = jnp.moveaxis(ql.reshape(b, nb, ATTN_BLOCK, MLA_HEADS, MLA_QK), 1, 0)
    o_lat = lax.map(lambda qb: softmax_attend(qb, k_all, v_all, scale), q_blocks)
    o_lat = jnp.moveaxis(o_lat, 0, 1).reshape(b, l, MLA_WIDTH)
    if not with_ctx:
        return jnp.zeros((b, qc.shape[1], MLA_WIDTH), o_lat.dtype), o_lat
    o_ctx = softmax_attend(qc, kc, vc, scale).reshape(b, qc.shape[1], MLA_WIDTH)
    return o_ctx, o_lat


def hyena_filter(l, w1, b1, w2, b2, w3, b3):
    t = jnp.arange(l, dtype=jnp.float32)
    t_norm = t / max(l - 1, 1)
    bands = jnp.linspace(1e-4, HY_BANDS - 1, HY_BANDS, dtype=jnp.float32)
    ang = 2.0 * math.pi * t[:, None] * bands[None, :] / l
    z = jnp.concatenate([t_norm[:, None], jnp.cos(ang), jnp.sin(ang)], axis=-1)
    h = jnp.sin(z @ w1 + b1)
    h = jnp.sin(h @ w2 + b2)
    h = (h @ w3 + b3).reshape(l, 2, HY_CH).astype(jnp.float32)
    deltas = jnp.abs(jnp.linspace(HY_MIN_DECAY, HY_MAX_DECAY, HY_CH, dtype=jnp.float32))
    h = h * jnp.exp(-t_norm[:, None, None] * deltas)
    buf = jnp.concatenate([h[:, 0], jnp.zeros((1, HY_CH), jnp.float32), h[:0:-1, 1]], axis=0)
    return buf / jnp.sum(jnp.abs(buf), axis=0, keepdims=True)


def hyena_mixer(p, conv_w, conv_b, w1, b1, w2, b2, w3, b3, d_skip):
    b, l, _ = p.shape
    u = dwconv3(p, conv_w, conv_b)
    x0, x1, v = u[..., :HY_CH], u[..., HY_CH:2 * HY_CH], u[..., 2 * HY_CH:]
    z = (v * x1).astype(jnp.float32)
    buf = hyena_filter(l, w1, b1, w2, b2, w3, b3)
    zf = jnp.fft.rfft(z, n=2 * l, axis=1)
    hf = jnp.fft.rfft(buf, n=2 * l, axis=0)
    y = jnp.fft.irfft(zf * hf[None], n=2 * l, axis=1)[:, :l] + z * d_skip
    return x0 * y.astype(x0.dtype)


def _pad_row(v, n=LANES):
    v = v.reshape(1, -1)
    return jnp.pad(v, ((0, 0), (0, n - v.shape[1])))


def kernel(x, c, ctx, c_ctx, ada_w, ada_b, mix_norm_g, w_in, gdn_conv_w, gdn_a_log, gdn_dt_bias, gdn_norm_g, mla_q_norm_g, mla_w_uq, mla_kv_norm_g, mla_w_ukv, mla_q_head_g, mla_k_head_g, hy_conv_w, hy_conv_b, hy_w1, hy_b1, hy_w2, hy_b2, hy_w3, hy_b3, hy_d, w_out, ffn_norm_g, ffn_w_up, ffn_conv_w, ffn_conv_b, ffn_w_down):
    bsz, seq, d = x.shape
    n_ctx = ctx.shape[1]
    assert n_ctx == TOK and seq % TOK == 0 and bsz < ADA_ROWS
    cond = jnp.concatenate([c, c_ctx[None, :], jnp.zeros((ADA_ROWS - bsz - 1, d), c.dtype)], axis=0)
    for i in range(DEPTH):
        last = i == DEPTH - 1
        mod = _ada_mod(cond, ada_w[i], ada_b[i])
        mod_lat = mod[:bsz, None, :]
        mod_ctx = mod[bsz][None, None, :]
        sa_l, ca_l, ga_l, sf_l, cf_l, gf_l = jnp.split(mod_lat, 6, axis=-1)
        sa_c, ca_c, ga_c, sf_c, cf_c, gf_c = (jnp.broadcast_to(t, (bsz, 1, d)) for t in jnp.split(mod_ctx, 6, axis=-1))

        w_in_p = _pad_w_in(w_in[i])
        g_mix = mix_norm_g[i][None, :]
        qkv, z, ab, mla_in, hy_in = _in_proj(x, ctx, g_mix, sa_l, ca_l, sa_c[:1], ca_c[:1], w_in_p)

        q, k, v, gf, gb, bf, bb = _gdn_prep(qkv, ab, gdn_conv_w[i], _pad_row(gdn_a_log[i]), _pad_row(gdn_dt_bias[i]))
        oc_f, oc_b, ol_f, ol_b = _gdn_scan(q, k, v, gf, gb, bf, bb)

        mla_w = _mla_weights(mla_w_uq[i], mla_w_ukv[i], mla_q_head_g[i], mla_k_head_g[i])
        q_ctx, q_lat, k_all, v_all = _mla_prep(mla_in, mla_q_norm_g[i], mla_kv_norm_g[i], mla_w, seq)
        mla_l = _attention(q_lat, k_all, v_all, n_ctx + seq, ATTN_TQ)
        hy_x0, hy_z, hy_zc, hy_zl = _hy_prep(hy_in, hy_conv_w[i], hy_conv_b[i])
        hy_mlp = (hy_w1[i], hy_b1[i], hy_w2[i], hy_b2[i], hy_w3[i], hy_b3[i])
        hy_l = _hy_conv(_hy_filter(seq, *hy_mlp), hy_zl)
        hd_row = hy_d[i][None, :]

        w_out_b = w_out[i].astype(_BF16)
        w_up_b = ffn_w_up[i].astype(_BF16)
        w_dn_b = ffn_w_down[i].astype(_BF16)
        g_ffn = ffn_norm_g[i][None, :]
        cb = ffn_conv_b[i][None, :]
        gn_row = jnp.tile(gdn_norm_g[i], GDN_HEADS)[None, :]

        x = _out_proj(x, ol_f, ol_b, z, gn_row, mla_l, hy_x0, hy_z, hy_l, hd_row, ga_l, w_out_b, 1)
        x = _ffn(x, g_ffn, sf_l, cf_l, gf_l, w_up_b, ffn_conv_w[i], cb, w_dn_b, FFN_TL)

        if not last:
            mla_c = _attention(q_ctx, k_all, v_all, n_ctx, TOK)
            hy_c = _hy_conv(_hy_filter(n_ctx, *hy_mlp), hy_zc)
            ctx = _out_proj(ctx, oc_f, oc_b, z, gn_row, mla_c, hy_x0, hy_z, hy_c, hd_row, ga_c, w_out_b, 0)
            ctx = _ffn(ctx, g_ffn, sf_c, cf_c, gf_c, w_up_b, ffn_conv_w[i], cb, w_dn_b, TOK)
    return x
```

```python
import functools
import math

import jax
import jax.numpy as jnp
import numpy as np
from jax import lax
from jax.experimental import pallas as pl
from jax.experimental.pallas import tpu as pltpu

D_MODEL = 1024
DEPTH = 2
GRID_W = 64
EPS = 1e-6

GDN_HEADS = 6
GDN_DK = 64
GDN_DV = 64
GDN_CHUNK = 64

MLA_HEADS = 6
MLA_Q_RANK = 256
MLA_KV_RANK = 128
MLA_NOPE = 64
MLA_ROPE = 32
MLA_V = 64
MLA_QK = MLA_NOPE + MLA_ROPE
ATTN_BLOCK = 128
ROPE_BASE = 10000.0
ROPE_AXIS = MLA_ROPE // 2

HY_CH = 256
HY_BANDS = 16
HY_EMB = 1 + 2 * HY_BANDS
HY_HIDDEN = 64
HY_TARGET = 1e-2
HY_FAST_DECAY_PCT = 0.3
HY_SLOW_DECAY_PCT = 1.5
HY_MAX_DECAY = math.log(HY_TARGET) / HY_FAST_DECAY_PCT
HY_MIN_DECAY = math.log(HY_TARGET) / HY_SLOW_DECAY_PCT

D_FF = 2816

GDN_WIDTH = GDN_HEADS * GDN_DV
MLA_WIDTH = MLA_HEADS * MLA_V
HY_WIDTH = HY_CH
GDN_QKV = GDN_HEADS * (2 * GDN_DK + GDN_DV)
GDN_IN = GDN_QKV + GDN_WIDTH + 4 * GDN_HEADS
MLA_IN = MLA_Q_RANK + MLA_KV_RANK + MLA_ROPE
HY_IN = 3 * HY_CH

LANES = 128
SUBLANES = 8
SUBLANES_BF16 = 16
VMEM_LIMIT_BYTES = 56 * 1024 * 1024

TOK = 256
GDN_PAIRS = GDN_HEADS // 2
GDN_BB = 2
N_GATE = 4 * GDN_HEADS

AB_PAD = LANES
MLA_PAD = 512
IN_GROUPS = (GDN_QKV, GDN_WIDTH, AB_PAD, MLA_PAD, HY_IN)
IN_TOTAL = sum(IN_GROUPS)
IN_DTYPES = (jnp.bfloat16, jnp.bfloat16, jnp.float32, jnp.bfloat16, jnp.bfloat16)

FFN_CHUNK = 256
FFN_TL = 1024
HALO = SUBLANES_BF16

_BF16 = jnp.bfloat16
_F32 = jnp.float32
_HI = lax.Precision.HIGHEST
_NT = (((1,), (1,)), ((), ()))
_TN = (((0,), (0,)), ((), ()))


def _cparams(n_axes, sem=None):
    return pltpu.CompilerParams(
        dimension_semantics=sem or ("parallel",) * n_axes, vmem_limit_bytes=VMEM_LIMIT_BYTES)


def _norm_mod(x, g, shift, scale):
    ms = jnp.mean(x * x, axis=-1, keepdims=True)
    y = x * lax.rsqrt(ms + EPS) * g
    return y * (1.0 + scale) + shift


def _silu(x):
    return x * jax.nn.sigmoid(x)


ADA_ROWS = 16
ADA_TN = 1024


def _ada_kernel(c_ref, w_ref, b_ref, o_ref):
    o_ref[...] = jnp.dot(_silu(c_ref[...]), w_ref[...], precision=_HI, preferred_element_type=_F32) + b_ref[...]


def _ada_mod(cond, w, b):
    d, n = w.shape
    return pl.pallas_call(
        _ada_kernel,
        grid=(n // ADA_TN,),
        in_specs=[pl.BlockSpec((ADA_ROWS, d), lambda j: (0, 0)),
                  pl.BlockSpec((d, ADA_TN), lambda j: (0, j)),
                  pl.BlockSpec((1, ADA_TN), lambda j: (0, j))],
        out_specs=pl.BlockSpec((ADA_ROWS, ADA_TN), lambda j: (0, j)),
        out_shape=jax.ShapeDtypeStruct((ADA_ROWS, n), _F32),
        compiler_params=_cparams(1),
        name="ada_mod",
    )(cond, w, b[None, :])


def _in_proj_kernel(x_ref, c_ref, g_ref, sl_ref, cl_ref, sc_ref, cc_ref, w_ref, *out_refs):
    is_ctx = pl.program_id(1) == 0
    x = jnp.where(is_ctx, c_ref[0], x_ref[0])
    shift = jnp.where(is_ctx, sc_ref[0], sl_ref[0])
    scale = jnp.where(is_ctx, cc_ref[0], cl_ref[0])
    h = _norm_mod(x, g_ref[...], shift, scale)
    p = jnp.dot(h.astype(_BF16), w_ref[...], preferred_element_type=_F32)
    off = 0
    for o_ref, n in zip(out_refs, IN_GROUPS):
        o_ref[0] = p[:, off:off + n].astype(o_ref.dtype)
        off += n


def _in_proj(x, ctx, g, shift_l, scale_l, shift_c, scale_c, w_pad):
    b, l, d = x.shape
    nt = 1 + l // TOK
    vec_l = pl.BlockSpec((1, 1, d), lambda i, j: (i, 0, 0))
    vec_c = pl.BlockSpec((1, 1, d), lambda i, j: (0, 0, 0))
    return pl.pallas_call(
        _in_proj_kernel,
        grid=(b, nt),
        in_specs=[
            pl.BlockSpec((1, TOK, d), lambda i, j: (i, jnp.maximum(j - 1, 0), 0)),
            pl.BlockSpec((1, TOK, d), lambda i, j: (i, 0, 0)),
            pl.BlockSpec((1, d), lambda i, j: (0, 0)),
            vec_l, vec_l, vec_c, vec_c,
            pl.BlockSpec((d, IN_TOTAL), lambda i, j: (0, 0)),
        ],
        out_specs=[pl.BlockSpec((1, TOK, n), lambda i, j: (i, j, 0)) for n in IN_GROUPS],
        out_shape=[jax.ShapeDtypeStruct((b, nt * TOK, n), dt) for n, dt in zip(IN_GROUPS, IN_DTYPES)],
        compiler_params=_cparams(2),
        name="in_proj",
    )(x, ctx, g, shift_l, scale_l, shift_c, scale_c, w_pad)


def _pad_w_in(w_in):
    s1 = GDN_QKV + GDN_WIDTH
    s2 = GDN_IN
    s3 = GDN_IN + MLA_IN
    d = w_in.shape[0]
    z = lambda n: jnp.zeros((d, n), w_in.dtype)
    parts = [w_in[:, :s1], w_in[:, s1:s2], z(AB_PAD - N_GATE),
             w_in[:, s2:s3], z(MLA_PAD - MLA_IN), w_in[:, s3:]]
    return jnp.concatenate(parts, axis=1).astype(_BF16)


def _gdn_consts():
    r = np.arange(TOK)
    same = (r[:, None] // GDN_CHUNK) == (r[None, :] // GDN_CHUNK)
    tril = (same & (r[None, :] <= r[:, None])).astype(np.float32)
    triu = (same & (r[None, :] >= r[:, None])).astype(np.float32)
    c = np.arange(GDN_WIDTH)
    head_ones = (c[:, None] // GDN_DK == c[None, :] // GDN_DK).astype(np.float32)
    expand = np.zeros((LANES, 4 * GDN_WIDTH), np.float32)
    for k in range(4):
        for h in range(GDN_HEADS):
            expand[k * GDN_HEADS + h, k * GDN_WIDTH + h * GDN_DK:k * GDN_WIDTH + (h + 1) * GDN_DK] = 1.0
    return tril, triu, head_ones, expand


def _gdn_prep_consts():
    tril, triu, head_ones, expand = _gdn_consts()
    b = lambda a: jnp.asarray(a, _BF16)
    return b(tril), b(triu), b(np.concatenate([head_ones] * 2, axis=0)), b(np.concatenate([expand] * 3, axis=0))


def _split_bf16(x, n):
    terms = []
    for _ in range(n):
        t = x.astype(_BF16)
        terms.append(t)
        x = x - t.astype(_F32)
    return terms


def _sum_terms(y, n):
    w = y.shape[1] // n
    out = y[:, :w]
    for t in range(1, n):
        out = out + y[:, t * w:(t + 1) * w]
    return out


def _gdn_prep_kernel(xp_ref, x_ref, xn_ref, ab_ref, cw_ref, alog_ref, dt_ref, tril_ref, triu_ref, hones_ref,
                     exp_ref, q_ref, k_ref, v_ref, gf_ref, gb_ref, bf_ref, bb_ref, xe_ref):
    j = pl.program_id(1)
    nt = pl.num_programs(1)
    pv = (j >= 2).astype(_F32)
    nv = jnp.logical_and(j >= 1, j < nt - 1).astype(_F32)
    xe_ref[0:HALO] = xp_ref[0].astype(_F32) * pv
    xe_ref[HALO:HALO + TOK] = x_ref[0].astype(_F32)
    xe_ref[HALO + TOK:] = xn_ref[0].astype(_F32) * nv
    cw = cw_ref[...]
    y = (xe_ref[HALO - 1:HALO - 1 + TOK] * cw[0:1] + xe_ref[HALO:HALO + TOK] * cw[1:2]
         + xe_ref[HALO + 1:HALO + 1 + TOK] * cw[2:3])
    y = _silu(y)
    hk = GDN_HEADS * GDN_DK
    q, k, v = y[:, :hk], y[:, hk:2 * hk], y[:, 2 * hk:]
    sq = jnp.concatenate([q * q, k * k], axis=0)
    ss = jnp.dot(jnp.concatenate(_split_bf16(sq, 2), axis=1), hones_ref[...], preferred_element_type=_F32)
    q_ref[0] = q * lax.rsqrt(ss[:TOK] + EPS) * (GDN_DK ** -0.5)
    k_ref[0] = k * lax.rsqrt(ss[TOK:] + EPS)
    v_ref[0] = v

    ab = ab_ref[0]
    lane = lax.broadcasted_iota(jnp.int32, ab.shape, 1)
    a_in = ab + dt_ref[...]
    softplus = jnp.maximum(a_in, 0.0) + jnp.log(1.0 + jnp.exp(-jnp.abs(a_in)))
    g = jnp.where(lane < 2 * GDN_HEADS, -jnp.exp(alog_ref[...]) * softplus, 0.0)
    g3 = jnp.concatenate(_split_bf16(g, 3), axis=1)
    gc_f = _sum_terms(jnp.dot(tril_ref[...], g3, preferred_element_type=_F32), 3)
    gc_b = _sum_terms(jnp.dot(triu_ref[...], g3, preferred_element_type=_F32), 3)
    cols = jnp.where(lane < GDN_HEADS, gc_f, jnp.where(lane < 2 * GDN_HEADS, gc_b, jax.nn.sigmoid(ab)))
    wide = jnp.dot(jnp.concatenate(_split_bf16(cols, 3), axis=1), exp_ref[...],
                   preferred_element_type=_F32)
    gf_ref[0] = wide[:, 0:GDN_WIDTH]
    gb_ref[0] = wide[:, GDN_WIDTH:2 * GDN_WIDTH]
    bf_ref[0] = wide[:, 2 * GDN_WIDTH:3 * GDN_WIDTH]
    bb_ref[0] = wide[:, 3 * GDN_WIDTH:]


def _gdn_prep(qkv, ab, conv_w, a_log_row, dt_row):
    b, lt, _ = qkv.shape
    nt = lt // TOK
    nh = TOK // HALO
    last = lt // HALO - 1
    consts = list(_gdn_prep_consts())
    whole = lambda a: pl.BlockSpec(a.shape, lambda i, j: (0,) * a.ndim)
    row = lambda n: pl.BlockSpec((1, TOK, n), lambda i, j: (i, j, 0))
    return pl.pallas_call(
        _gdn_prep_kernel,
        grid=(b, nt),
        in_specs=[
            pl.BlockSpec((1, HALO, GDN_QKV), lambda i, j: (i, jnp.maximum(j * nh - 1, 0), 0)),
            row(GDN_QKV),
            pl.BlockSpec((1, HALO, GDN_QKV), lambda i, j: (i, jnp.minimum((j + 1) * nh, last), 0)),
            row(AB_PAD), whole(conv_w), whole(a_log_row), whole(dt_row),
        ] + [whole(a) for a in consts],
        out_specs=[row(GDN_WIDTH)] * 7,
        out_shape=[jax.ShapeDtypeStruct((b, lt, GDN_WIDTH), _F32)] * 7,
        scratch_shapes=[pltpu.VMEM((TOK + 2 * HALO, GDN_QKV), _F32)],
        compiler_params=_cparams(2),
        name="gdn_prep",
    )(qkv, qkv, qkv, ab, conv_w, a_log_row, dt_row, *consts)


def _block_diag(z, left):
    return jnp.concatenate([jnp.where(left, z, 0.0), jnp.where(left, 0.0, z)], axis=0).astype(_BF16)


def _mm(a, b):
    return jnp.dot(a.astype(_BF16), b, preferred_element_type=_F32)


def _gdn_intra(probs, masks):
    left, eye2, ones64, tri, same_blk = masks
    c = GDN_CHUNK
    qs, ks, vs, gxs, bxs, bws = zip(*probs)
    qk_kk = [lax.dot_general(jnp.concatenate([q, k], axis=0).astype(_BF16), _block_diag(k, left), _NT,
                             preferred_element_type=_F32) for q, k in zip(qs, ks)]
    rs = []
    for gx in gxs:
        d0 = jnp.where(eye2, gx, 0.0)
        t_hi = d0.astype(_BF16)
        r1 = d0 - t_hi.astype(_F32)
        t_mid = r1.astype(_BF16)
        t_lo = (r1 - t_mid.astype(_F32)).astype(_BF16)
        r3 = jnp.dot(ones64, jnp.concatenate([t_hi, t_mid, t_lo], axis=1), preferred_element_type=_F32)
        rs.append(r3[:, :LANES] + r3[:, LANES:2 * LANES] + r3[:, 2 * LANES:])
    a_s, qkm, egs = [], [], []
    for x, gx, bx, r, bw in zip(qk_kk, gxs, bxs, rs, bws):
        incl, strict = tri[bw]
        dec = jnp.where(incl, jnp.exp(jnp.where(incl, gx - r, 0.0)), 0.0)
        a_s.append(bx * x[c:] * jnp.where(strict, dec, 0.0))
        qkm.append(x[:c] * dec)
        egs.append(jnp.exp(gx))
    eye_f = jnp.where(eye2, 1.0, 0.0)
    base = same_blk[8]
    d1 = [jnp.where(base, a, 0.0) for a in a_s]
    ps = [eye_f - d for d in d1]
    d2 = [_mm(d, _block_diag(d, left)) for d in d1]
    d2_bd = [_block_diag(d, left) for d in d2]
    ps = [p + _mm(p, bd) for p, bd in zip(ps, d2_bd)]
    d4 = [_mm(d, bd) for d, bd in zip(d2, d2_bd)]
    ps = [p + _mm(p, _block_diag(d, left)) for p, d in zip(ps, d4)]
    for blk in (8, 16, 32):
        off = jnp.logical_and(same_blk[2 * blk], jnp.logical_not(same_blk[blk]))
        t1 = [_mm(p, _block_diag(jnp.where(off, a, 0.0), left)) for p, a in zip(ps, a_s)]
        ps = [p - _mm(t, _block_diag(p, left)) for p, t in zip(ps, t1)]
    out = []
    for p, q, k, v, gx, bx, eg, qk, bw in zip(ps, qs, ks, vs, gxs, bxs, egs, qkm, bws):
        tot = gx[0:1] if bw else gx[c - 1:c]
        lhs = jnp.concatenate([k * eg, q * eg], axis=0).astype(_BF16)
        out.append((p, lhs, bx, bx * v, qk, k * jnp.exp(tot - gx), tot))
    return out


def _gdn_state_step(chains, left, diag_blocks):
    c = GDN_CHUNK
    ys = [jnp.dot(x[1], s.astype(_BF16), preferred_element_type=_F32) for s, x in chains]
    resid = [x[3] - x[2] * y[:c] for (s, x), y in zip(chains, ys)]
    deltas = [_mm(x[0], _block_diag(r, left)) for (s, x), r in zip(chains, resid)]
    os_ = [y[c:] + _mm(x[4], _block_diag(d, left)) for (s, x), y, d in zip(chains, ys, deltas)]
    upds = [lax.dot_general(x[5].astype(_BF16), d.astype(_BF16), _TN, preferred_element_type=_F32)
            for (s, x), d in zip(chains, deltas)]
    new_s = [s * jnp.exp(x[6]) + jnp.where(diag_blocks, u, 0.0) for (s, x), u in zip(chains, upds)]
    return list(zip(os_, new_s))


def _gdn_scan_kernel(cq, ck, cv, cgf, cbf, cgb, cbb, fq, fk, fv, fg, fb, rq, rk, rv, rg, rb,
                        ocf_ref, ocb_ref, of_ref, ob_ref, sf_ref, sb_ref):
    step = pl.program_id(1)
    is_ctx = step == 0

    @pl.when(is_ctx)
    def _():
        sf_ref[...] = jnp.zeros_like(sf_ref)
        sb_ref[...] = jnp.zeros_like(sb_ref)

    c = GDN_CHUNK
    nc = TOK // c
    li = lax.broadcasted_iota(jnp.int32, (c, LANES), 1)
    ri = lax.broadcasted_iota(jnp.int32, (c, LANES), 0)
    lj = li & (c - 1)
    left = li < c
    eye2 = lj == ri
    ones64 = jnp.ones((c, c), _BF16)
    r2 = lax.broadcasted_iota(jnp.int32, (LANES, LANES), 0)
    c2 = lax.broadcasted_iota(jnp.int32, (LANES, LANES), 1)
    diag_blocks = (r2 // c) == (c2 // c)
    tri = {False: (ri >= lj, ri > lj), True: (ri <= lj, ri < lj)}
    same_blk = {b: (ri // b) == (lj // b) for b in (8, 16, 32, 64)}
    masks = (left, eye2, ones64, tri, same_blk)

    nbb = fq.shape[0]
    ctx_refs = {False: (cq, ck, cv, cgf, cbf), True: (cq, ck, cv, cgb, cbb)}
    lat_refs = {False: (fq, fk, fv, fg, fb), True: (rq, rk, rv, rg, rb)}
    vals = {(bw, e): [jnp.where(is_ctx, cr[e], r[e]) for cr, r in zip(ctx_refs[bw], lat_refs[bw])]
            for bw in (False, True) for e in range(nbb)}
    chain_keys = [(bw, e, p) for bw in (False, True) for e in range(nbb) for p in range(GDN_PAIRS)]
    prob_keys = [(bw, e, p, n) for bw, e, p in chain_keys for n in range(nc)]
    probs = [tuple(a[n * c:(n + 1) * c, p * LANES:(p + 1) * LANES] for a in vals[(bw, e)]) + (bw,)
             for bw, e, p, n in prob_keys]
    intra = dict(zip(prob_keys, _gdn_intra(probs, masks)))

    state_ref = lambda bw: sb_ref if bw else sf_ref
    states = {(bw, e, p): state_ref(bw)[e, p] for bw, e, p in chain_keys}
    outs = {}
    for t in range(nc):
        ns = {kk: (nc - 1 - t if kk[0] else t) for kk in chain_keys}
        res = _gdn_state_step([(states[kk], intra[kk + (ns[kk],)]) for kk in chain_keys], left, diag_blocks)
        for kk, (o, st) in zip(chain_keys, res):
            outs[kk + (ns[kk],)] = o
            states[kk] = st
    for bw, e, p in chain_keys:
        state_ref(bw)[e, p] = states[(bw, e, p)]

    for bw, o_ref, oc_ref in ((False, of_ref, ocf_ref), (True, ob_ref, ocb_ref)):
        for e in range(nbb):
            o_all = jnp.concatenate(
                [jnp.concatenate([outs[(bw, e, p, n)] for p in range(GDN_PAIRS)], axis=1) for n in range(nc)], axis=0)

            @pl.when(is_ctx)
            def _(o_all=o_all, oc_ref=oc_ref, e=e):
                oc_ref[e] = o_all

            @pl.when(jnp.logical_not(is_ctx))
            def _(o_all=o_all, o_ref=o_ref, e=e):
                o_ref[e] = o_all


def _gdn_scan(q, k, v, gf, gb, bf, bb):
    b, lt, w = q.shape
    nl = lt // TOK - 1
    nbb = GDN_BB
    ctx = pl.BlockSpec((nbb, TOK, w), lambda i, s: (i, 0, 0))
    fwd = pl.BlockSpec((nbb, TOK, w), lambda i, s: (i, jnp.maximum(s, 1), 0))
    bwd = pl.BlockSpec((nbb, TOK, w), lambda i, s: (i, nl + 1 - jnp.maximum(s, 1), 0))
    fwd_o = pl.BlockSpec((nbb, TOK, w), lambda i, s: (i, jnp.maximum(s, 1) - 1, 0))
    bwd_o = pl.BlockSpec((nbb, TOK, w), lambda i, s: (i, nl - jnp.maximum(s, 1), 0))
    return pl.pallas_call(
        _gdn_scan_kernel,
        grid=(b // nbb, nl + 1),
        in_specs=[ctx] * 7 + [fwd] * 5 + [bwd] * 5,
        out_specs=[ctx, ctx, fwd_o, bwd_o],
        out_shape=[jax.ShapeDtypeStruct((b, TOK, w), _F32)] * 2 + [jax.ShapeDtypeStruct((b, nl * TOK, w), _F32)] * 2,
        scratch_shapes=[pltpu.VMEM((nbb, GDN_PAIRS, LANES, LANES), _F32)] * 2,
        compiler_params=_cparams(2, ("parallel", "arbitrary")),
        name="gdn_scan",
    )(q, k, v, gf, bf, gb, bb, q, k, v, gf, bf, q, k, v, gb, bb)


MLA_HEAD_PAD = LANES
MLA_WIDE = MLA_HEADS * MLA_HEAD_PAD
ATTN_TQ = 512
ATTN_TK = 512


def _rope_tables(n_ctx, seq):
    rows = seq // GRID_W
    row = np.repeat(np.arange(rows, dtype=np.float64), GRID_W)
    col = np.tile(np.arange(GRID_W, dtype=np.float64), rows)
    inv = ROPE_BASE ** (-np.arange(0, ROPE_AXIS, 2, dtype=np.float64) / ROPE_AXIS)
    ang = np.concatenate([row[:, None] * inv, col[:, None] * inv], axis=-1)
    cos, sin = np.cos(ang), np.sin(ang)
    half = MLA_ROPE // 2
    c = np.ones((n_ctx + seq, MLA_HEAD_PAD))
    s = np.zeros((n_ctx + seq, MLA_HEAD_PAD))
    c[n_ctx:, MLA_NOPE:MLA_NOPE + half] = cos
    c[n_ctx:, MLA_NOPE + half:MLA_QK] = cos
    s[n_ctx:, MLA_NOPE:MLA_NOPE + half] = -sin
    s[n_ctx:, MLA_NOPE + half:MLA_QK] = sin
    return c.astype(np.float32), s.astype(np.float32)


def _mla_prep_kernel(p_ref, qn_ref, kvn_ref, wq_ref, wk_ref, wv_ref, qg_ref, kg_ref, cos_ref, sin_ref,
                     qc_ref, ql_ref, k_ref, v_ref):
    p = p_ref[0].astype(_F32)
    cq = p[:, :MLA_Q_RANK]
    ckv = p[:, MLA_Q_RANK:MLA_Q_RANK + MLA_KV_RANK]
    kr = p[:, MLA_Q_RANK + MLA_KV_RANK:]
    cq = (cq * lax.rsqrt(jnp.mean(cq * cq, axis=-1, keepdims=True) + EPS) * qn_ref[...]).astype(_BF16)
    ckv = (ckv * lax.rsqrt(jnp.mean(ckv * ckv, axis=-1, keepdims=True) + EPS) * kvn_ref[...]).astype(_BF16)
    q = jnp.dot(cq, wq_ref[...], preferred_element_type=_F32)
    k = jnp.dot(jnp.concatenate([ckv, kr.astype(_BF16)], axis=1), wk_ref[...], preferred_element_type=_F32)
    lane = lax.broadcasted_iota(jnp.int32, (TOK, MLA_WIDE), 1) & (MLA_HEAD_PAD - 1)
    v = jnp.dot(ckv, wv_ref[...], preferred_element_type=_F32) + jnp.where(lane == MLA_V, 1.0, 0.0)
    cos = jnp.concatenate([cos_ref[...]] * MLA_HEADS, axis=1)
    sin = jnp.concatenate([sin_ref[...]] * MLA_HEADS, axis=1)
    half = MLA_ROPE // 2
    first = jnp.logical_and(lane >= MLA_NOPE, lane < MLA_NOPE + half)
    second = jnp.logical_and(lane >= MLA_NOPE + half, lane < MLA_QK)

    def head_norm_rope(x, g):
        parts = []
        for h in range(MLA_HEADS):
            xh = x[:, h * MLA_HEAD_PAD:(h + 1) * MLA_HEAD_PAD]
            ms = jnp.sum(xh * xh, axis=-1, keepdims=True) * (1.0 / MLA_QK)
            parts.append(xh * lax.rsqrt(ms + EPS))
        xn = jnp.concatenate(parts, axis=1) * g
        up = pltpu.roll(xn, half, 1)
        down = pltpu.roll(xn, MLA_WIDE - half, 1)
        swapped = jnp.where(first, down, jnp.where(second, up, 0.0))
        return xn * cos + swapped * sin

    qf = head_norm_rope(q, qg_ref[...]) * (MLA_QK ** -0.5 * math.log2(math.e))
    kf = head_norm_rope(k, kg_ref[...])
    is_ctx = pl.program_id(1) == 0
    for h in range(MLA_HEADS):
        sl = slice(h * MLA_HEAD_PAD, (h + 1) * MLA_HEAD_PAD)
        k_ref[0, h] = kf[:, sl].astype(_BF16)
        v_ref[0, h] = v[:, sl].astype(_BF16)

    @pl.when(is_ctx)
    def _():
        for h in range(MLA_HEADS):
            qc_ref[0, h] = qf[:, h * MLA_HEAD_PAD:(h + 1) * MLA_HEAD_PAD].astype(_BF16)

    @pl.when(jnp.logical_not(is_ctx))
    def _():
        for h in range(MLA_HEADS):
            ql_ref[0, h] = qf[:, h * MLA_HEAD_PAD:(h + 1) * MLA_HEAD_PAD].astype(_BF16)


def _mla_weights(w_uq, w_ukv, q_head_g, k_head_g):
    pad = MLA_HEAD_PAD
    wq = jnp.pad(w_uq.reshape(MLA_Q_RANK, MLA_HEADS, MLA_QK), ((0, 0), (0, 0), (0, pad - MLA_QK)))
    wkv = w_ukv.reshape(MLA_KV_RANK, MLA_HEADS, MLA_NOPE + MLA_V)
    wk = jnp.pad(wkv[:, :, :MLA_NOPE], ((0, 0), (0, 0), (0, pad - MLA_NOPE)))
    wv = jnp.pad(wkv[:, :, MLA_NOPE:], ((0, 0), (0, 0), (0, pad - MLA_V)))
    sel = np.zeros((MLA_PAD - MLA_Q_RANK - MLA_KV_RANK, MLA_WIDE), np.float32)
    for h in range(MLA_HEADS):
        for r in range(MLA_ROPE):
            sel[r, h * pad + MLA_NOPE + r] = 1.0
    tile_g = lambda g: jnp.tile(jnp.pad(g, (0, pad - MLA_QK)), MLA_HEADS)[None, :]
    flat = lambda w: w.reshape(w.shape[0], MLA_WIDE).astype(_BF16)
    wk_sel = jnp.concatenate([flat(wk), jnp.asarray(sel, _BF16)], axis=0)
    return flat(wq), wk_sel, flat(wv), tile_g(q_head_g), tile_g(k_head_g)


def _mla_prep(mla_in, q_norm_g, kv_norm_g, weights, seq):
    b, lt, _ = mla_in.shape
    nt = lt // TOK
    wq, wk, wv, qg, kg = weights
    cos, sin = (jnp.asarray(t) for t in _rope_tables(lt - seq, seq))
    whole = lambda a: pl.BlockSpec(a.shape, lambda i, j: (0,) * a.ndim)
    tab = pl.BlockSpec((TOK, MLA_HEAD_PAD), lambda i, j: (j, 0))
    hd = lambda f: pl.BlockSpec((1, MLA_HEADS, TOK, MLA_HEAD_PAD), f)
    qn, kvn = q_norm_g[None, :], kv_norm_g[None, :]
    shp = lambda t: jax.ShapeDtypeStruct((b, MLA_HEADS, t, MLA_HEAD_PAD), _BF16)
    return pl.pallas_call(
        _mla_prep_kernel,
        grid=(b, nt),
        in_specs=[pl.BlockSpec((1, TOK, MLA_PAD), lambda i, j: (i, j, 0)), whole(qn), whole(kvn),
                  whole(wq), whole(wk), whole(wv), whole(qg), whole(kg), tab, tab],
        out_specs=[hd(lambda i, j: (i, 0, 0, 0)), hd(lambda i, j: (i, 0, jnp.maximum(j, 1) - 1, 0)),
                   hd(lambda i, j: (i, 0, j, 0)), hd(lambda i, j: (i, 0, j, 0))],
        out_shape=[shp(TOK), shp(lt - TOK), shp(lt), shp(lt)],
        compiler_params=_cparams(2, ("parallel", "arbitrary")),
        name="mla_prep",
    )(mla_in, qn, kvn, wq, wk, wv, qg, kg, cos, sin)


def _attn_kernel(q_ref, k_ref, v_ref, o_ref):
    n_keys = k_ref.shape[2]
    starts = list(range(0, n_keys, ATTN_TK))
    heads = range(2)

    def scores(h, lo):
        hi = min(lo + ATTN_TK, n_keys)
        return lax.dot_general(q_ref[0, h], k_ref[0, h, lo:hi, :], _NT, preferred_element_type=_F32)

    nxt = [scores(h, starts[0]) for h in heads]
    m = [None, None]
    acc = [None, None]
    for n, lo in enumerate(starts):
        cur = nxt
        if n + 1 < len(starts):
            nxt = [scores(h, starts[n + 1]) for h in heads]
        hi = min(lo + ATTN_TK, n_keys)
        for h in heads:
            s = cur[h]
            m_blk = jnp.max(s, axis=-1, keepdims=True)
            m_new = m_blk if n == 0 else jnp.maximum(m[h], m_blk)
            p = jnp.exp2(s - m_new).astype(_BF16)
            pv = jnp.dot(p, v_ref[0, h, lo:hi, :], preferred_element_type=_F32)
            acc[h] = pv if n == 0 else acc[h] * jnp.exp2(m[h] - m_new) + pv
            m[h] = m_new
    o_ref[0] = jnp.concatenate([a[:, :MLA_V] / a[:, MLA_V:MLA_V + 1] for a in acc], axis=1)


def _attention(q, k, v, n_keys, tq):
    b, h, t, w = q.shape
    return pl.pallas_call(
        _attn_kernel,
        grid=(b, h // 2, t // tq),
        in_specs=[pl.BlockSpec((1, 2, tq, w), lambda i, j, l: (i, j, l, 0)),
                  pl.BlockSpec((1, 2, n_keys, w), lambda i, j, l: (i, j, 0, 0)),
                  pl.BlockSpec((1, 2, n_keys, w), lambda i, j, l: (i, j, 0, 0))],
        out_specs=pl.BlockSpec((1, tq, 2 * MLA_V), lambda i, j, l: (i, l, j)),
        out_shape=jax.ShapeDtypeStruct((b, t, h * MLA_V), _F32),
        compiler_params=_cparams(3),
        name="mla_attention",
    )(q, k, v)


HY_BLK = 256
HY_LO = 128
HY_CB = 16


def _hy_prep_kernel(xp_ref, x_ref, xn_ref, cw_ref, cb_ref, x0_ref, z_ref, zc_ref, zl_ref, xe_ref):
    j = pl.program_id(1)
    nt = pl.num_programs(1)
    pv = (j >= 2).astype(_F32)
    nv = jnp.logical_and(j >= 1, j < nt - 1).astype(_F32)
    xe_ref[0:HALO] = xp_ref[0].astype(_F32) * pv
    xe_ref[HALO:HALO + TOK] = x_ref[0].astype(_F32)
    xe_ref[HALO + TOK:] = xn_ref[0].astype(_F32) * nv
    cw = cw_ref[...]
    u = (xe_ref[HALO - 1:HALO - 1 + TOK] * cw[0:1] + xe_ref[HALO:HALO + TOK] * cw[1:2]
         + xe_ref[HALO + 1:HALO + 1 + TOK] * cw[2:3] + cb_ref[...])
    x0_ref[0] = u[:, :HY_CH]
    z = u[:, 2 * HY_CH:] * u[:, HY_CH:2 * HY_CH]
    z_ref[0] = z

    @pl.when(j == 0)
    def _():
        zc_ref[0] = z.astype(_BF16)

    @pl.when(j > 0)
    def _():
        zl_ref[0] = z.astype(_BF16)


def _hy_prep(hy_in, conv_w, conv_b):
    b, lt, _ = hy_in.shape
    nh = TOK // HALO
    last = lt // HALO - 1
    whole = lambda a: pl.BlockSpec(a.shape, lambda i, j: (0,) * a.ndim)
    row = lambda n: pl.BlockSpec((1, TOK, n), lambda i, j: (i, j, 0))
    cb = conv_b[None, :]
    return pl.pallas_call(
        _hy_prep_kernel,
        grid=(b, lt // TOK),
        in_specs=[pl.BlockSpec((1, HALO, HY_IN), lambda i, j: (i, jnp.maximum(j * nh - 1, 0), 0)),
                  row(HY_IN),
                  pl.BlockSpec((1, HALO, HY_IN), lambda i, j: (i, jnp.minimum((j + 1) * nh, last), 0)),
                  whole(conv_w), whole(cb)],
        out_specs=[row(HY_CH), row(HY_CH),
                   pl.BlockSpec((1, TOK, HY_CH), lambda i, j: (i, 0, 0)),
                   pl.BlockSpec((1, TOK, HY_CH), lambda i, j: (i, jnp.maximum(j, 1) - 1, 0))],
        out_shape=[jax.ShapeDtypeStruct((b, lt, HY_CH), _F32)] * 2
        + [jax.ShapeDtypeStruct((b, TOK, HY_CH), _BF16), jax.ShapeDtypeStruct((b, lt - TOK, HY_CH), _BF16)],
        scratch_shapes=[pltpu.VMEM((TOK + 2 * HALO, HY_IN), _F32)],
        compiler_params=_cparams(2, ("parallel", "arbitrary")),
        name="hyena_prep",
    )(hy_in, hy_in, hy_in, conv_w, cb)


def _hy_filter_consts(l):
    def emb(t):
        t = t.astype(np.float64)
        t_norm = t / max(l - 1, 1)
        bands = np.linspace(1e-4, HY_BANDS - 1, HY_BANDS)
        ang = 2.0 * math.pi * t[:, None] * bands[None, :] / l
        z = np.concatenate([t_norm[:, None], np.cos(ang), np.sin(ang)], axis=-1)
        return np.pad(z, ((0, 0), (0, LANES - HY_EMB))), t_norm[:, None]
    r = np.arange(l)
    e_rev, tn_rev = emb(l - 1 - r)
    e_sh, tn_sh = emb(r + 1)
    deltas = np.abs(np.linspace(HY_MIN_DECAY, HY_MAX_DECAY, HY_CH))[None, :]
    f = lambda a: np.asarray(a, np.float32)
    return f(e_rev), f(e_sh), f(tn_rev), f(tn_sh), f(deltas)


def _hy_filter_kernel(er_ref, es_ref, tr_ref, ts_ref, dl_ref, w1_ref, b1_ref, w2_ref, b2_ref, w3_ref, b3_ref, o_ref):
    l = er_ref.shape[0]

    def mlp(e, col):
        h = jnp.sin(jnp.dot(e, w1_ref[...], precision=_HI, preferred_element_type=_F32) + b1_ref[...])
        h = jnp.sin(jnp.dot(h, w2_ref[...], precision=_HI, preferred_element_type=_F32) + b2_ref[...])
        return (jnp.dot(h, w3_ref[:, col * HY_CH:(col + 1) * HY_CH], precision=_HI, preferred_element_type=_F32)
                + b3_ref[:, col * HY_CH:(col + 1) * HY_CH])

    hf = mlp(er_ref[...], 0) * jnp.exp(-tr_ref[...] * dl_ref[...])
    hb = mlp(es_ref[...], 1) * jnp.exp(-ts_ref[...] * dl_ref[...])
    row = lax.broadcasted_iota(jnp.int32, hb.shape, 0)
    hb = jnp.where(row < l - 1, hb, 0.0)
    norm = jnp.sum(jnp.abs(hf), axis=0, keepdims=True) + jnp.sum(jnp.abs(hb), axis=0, keepdims=True)
    o_ref[...] = jnp.transpose(jnp.concatenate([hf, hb], axis=0) / norm)


def _hy_filter(l, w1, b1, w2, b2, w3, b3):
    consts = [jnp.asarray(a) for a in _hy_filter_consts(l)]
    w1p = jnp.pad(w1, ((0, LANES - HY_EMB), (0, 0)))
    args = consts + [w1p, b1[None, :], w2, b2[None, :], w3, b3[None, :]]
    return pl.pallas_call(
        _hy_filter_kernel,
        out_shape=jax.ShapeDtypeStruct((HY_CH, 2 * l), _F32),
        compiler_params=pltpu.CompilerParams(vmem_limit_bytes=VMEM_LIMIT_BYTES),
        name="hyena_filter",
    )(*args)


def _hy_conv_kernel(f_ref, z_ref, y_ref, g_ref, *, nblk, nb):
    cols = z_ref.shape[2]
    lane = lax.broadcasted_iota(jnp.int32, (HY_BLK, cols), 1)
    lane_p = lax.broadcasted_iota(jnp.int32, (HY_BLK // 2, cols), 1)
    gs = 4 if nblk >= 4 else 1

    def build(ch, slot):
        rows = SUBLANES_BF16
        base = pltpu.roll(jnp.broadcast_to(f_ref[ch], (rows, f_ref.shape[2])), 1, 1, stride=1, stride_axis=0)
        base = base.astype(_BF16)
        g_ref[slot, 0:rows, :] = base
        packed = pltpu.bitcast(base, jnp.int32)
        for a in range(1, HY_LO // rows):
            g_ref[slot, a * rows:(a + 1) * rows, :] = pltpu.bitcast(pltpu.roll(packed, rows * a, 1), _BF16)

    def convolve(ch, slot):
        z = z_ref[ch]
        zp = pltpu.bitcast(z, jnp.int32)
        real = nb * nblk

        def shift(x, lanes, blocks, sign, fill):
            if blocks == 0:
                return x
            s = nb * blocks
            if sign > 0:
                return jnp.where(lanes >= s, pltpu.roll(x, s, 1), fill)
            return jnp.where(lanes < real - s, pltpu.roll(x, cols - s, 1), fill)

        acc = {}
        for sign in (1, -1):
            for a in range(-(-nblk // gs)):
                ds = [gs * a + r for r in range(gs) if gs * a + r < nblk and not (sign < 0 and gs * a + r == 0)]
                if not ds:
                    continue
                wins = []
                for d in ds:
                    o = HY_BLK * (nblk - sign * d)
                    wins += [g_ref[slot, :, o:o + HY_BLK], g_ref[slot, :, o - HY_LO:o - HY_LO + HY_BLK]]
                zs = pltpu.bitcast(shift(zp, lane_p, gs * a, sign, 0), _BF16)
                part = jnp.dot(jnp.concatenate(wins, axis=0), zs, preferred_element_type=_F32)
                for n, d in enumerate(ds):
                    key = (sign, d - gs * a)
                    blk = part[n * HY_BLK:(n + 1) * HY_BLK]
                    acc[key] = blk if key not in acc else acc[key] + blk
        y = None
        for (sign, r), v in acc.items():
            v = shift(v, lane, r, sign, 0.0)
            y = v if y is None else y + v
        y_ref[ch] = y

    n_ch = f_ref.shape[0]
    build(0, 0)

    def pair(it, carry):
        ch = 2 * it
        build(ch + 1, 1)
        convolve(ch, 0)
        build(jnp.minimum(ch + 2, n_ch - 1), 0)
        convolve(ch + 1, 1)
        return carry

    lax.fori_loop(0, n_ch // 2, pair, 0)


def _hy_conv(fline, z):
    b, l, c = z.shape
    nblk = l // HY_BLK
    cols = max(nblk * b, LANES)
    zall = z.reshape(b, nblk, HY_BLK, c).transpose(3, 2, 1, 0).reshape(c, HY_BLK, nblk * b)
    zall = jnp.pad(zall.astype(_BF16), ((0, 0), (0, 0), (0, cols - nblk * b)))
    y = pl.pallas_call(
        functools.partial(_hy_conv_kernel, nblk=nblk, nb=b),
        grid=(c // HY_CB,),
        in_specs=[pl.BlockSpec((HY_CB, 1, 2 * l), lambda i: (i, 0, 0)),
                  pl.BlockSpec((HY_CB, HY_BLK, cols), lambda i: (i, 0, 0))],
        out_specs=pl.BlockSpec((HY_CB, HY_BLK, cols), lambda i: (i, 0, 0)),
        out_shape=jax.ShapeDtypeStruct((c, HY_BLK, cols), _F32),
        scratch_shapes=[pltpu.VMEM((2, HY_LO, 2 * l), _BF16)],
        compiler_params=_cparams(1),
        name="hyena_conv",
    )(fline.reshape(c, 1, 2 * l), zall)
    return y[:, :, :nblk * b].reshape(c, HY_BLK, nblk, b).transpose(3, 2, 1, 0).reshape(b, l, c)


def _out_proj_kernel(x_ref, of_ref, ob_ref, z_ref, gn_ref, hones_ref, mla_ref, hx_ref, hz_ref, hy_ref, hd_ref,
                     ga_ref, w_ref, o_ref):
    o = of_ref[0] + ob_ref[0]
    ms = jnp.dot(jnp.concatenate(_split_bf16(o * o, 2), axis=1), hones_ref[...],
                 preferred_element_type=_F32) * (1.0 / GDN_DV)
    gdn = o * lax.rsqrt(ms + EPS) * gn_ref[...] * _silu(z_ref[0].astype(_F32))
    hy = hx_ref[0] * (hy_ref[0] + hz_ref[0] * hd_ref[...])
    mix = jnp.concatenate([gdn, mla_ref[0], hy], axis=-1).astype(_BF16)
    y = jnp.dot(mix, w_ref[...], preferred_element_type=_F32)
    o_ref[0] = x_ref[0] + ga_ref[0] * y


def _out_proj(x, o_f, o_b, z, gn_row, mla, hx0, hz, hy, hd_row, ga, w_out, toff):
    b, l, d = x.shape
    hones = _gdn_prep_consts()[2]
    row = lambda n: pl.BlockSpec((1, TOK, n), lambda i, j: (i, j, 0))
    rowc = lambda n: pl.BlockSpec((1, TOK, n), lambda i, j: (i, j + toff, 0))
    whole = lambda a: pl.BlockSpec(a.shape, lambda i, j: (0,) * a.ndim)
    return pl.pallas_call(
        _out_proj_kernel,
        grid=(b, l // TOK),
        in_specs=[row(d), row(GDN_WIDTH), row(GDN_WIDTH), rowc(GDN_WIDTH), whole(gn_row), whole(hones),
                  row(MLA_WIDTH), rowc(HY_WIDTH), rowc(HY_WIDTH), row(HY_WIDTH), whole(hd_row),
                  pl.BlockSpec((1, 1, d), lambda i, j: (i, 0, 0)),
                  pl.BlockSpec((d, d), lambda i, j: (0, 0))],
        out_specs=row(d),
        out_shape=jax.ShapeDtypeStruct((b, l, d), _F32),
        compiler_params=_cparams(2),
        name="out_proj",
    )(x, o_f, o_b, z, gn_row, hones, mla, hx0, hz, hy, hd_row, ga, w_out)


def _ffn_kernel(xp_ref, x_ref, xn_ref, g_ref, sf_ref, cf_ref, gf_ref, wup_ref, cw_ref, cb_ref, wdn_ref,
                o_ref, h_ref, up_ref, uv_ref, act_ref, *, tl):
    i = pl.program_id(1)
    nt = pl.num_programs(1)
    g, sf, cf = g_ref[...], sf_ref[0], cf_ref[0]
    pv = (i > 0).astype(_F32)
    nv = (i < nt - 1).astype(_F32)
    h_ref[0:HALO] = (_norm_mod(xp_ref[0], g, sf, cf) * pv).astype(_BF16)
    h_ref[HALO:HALO + tl] = _norm_mod(x_ref[0], g, sf, cf).astype(_BF16)
    h_ref[HALO + tl:] = (_norm_mod(xn_ref[0], g, sf, cf) * nv).astype(_BF16)

    def up_proj(c, slot):
        lo = pl.multiple_of(c * FFN_CHUNK, FFN_CHUNK)
        up_ref[slot] = jnp.dot(h_ref[...], wup_ref[:, pl.ds(lo, FFN_CHUNK)], preferred_element_type=_F32)
        uv_ref[slot] = jnp.dot(h_ref[HALO:HALO + tl], wup_ref[:, pl.ds(D_FF + lo, FFN_CHUNK)],
                               preferred_element_type=_F32)

    def gate(c, slot):
        lo = pl.multiple_of(c * FFN_CHUNK, FFN_CHUNK)
        cw = cw_ref[:, pl.ds(lo, FFN_CHUNK)]
        cb = cb_ref[:, pl.ds(lo, FFN_CHUNK)]
        gt = (up_ref[slot, HALO - 1:HALO - 1 + tl] * cw[0:1] + up_ref[slot, HALO:HALO + tl] * cw[1:2]
              + up_ref[slot, HALO + 1:HALO + 1 + tl] * cw[2:3] + cb)
        act_ref[:, pl.ds(lo, FFN_CHUNK)] = (_silu(gt) * uv_ref[slot]).astype(_BF16)

    n_chunks = D_FF // FFN_CHUNK
    up_proj(0, 0)

    def pair(it, carry):
        c = 2 * it
        up_proj(c + 1, 1)
        gate(c, 0)
        up_proj(c + 2, 0)
        gate(c + 1, 1)
        return carry

    lax.fori_loop(0, (n_chunks - 1) // 2, pair, 0)
    assert n_chunks % 2 == 1
    gate(n_chunks - 1, 0)
    y = jnp.dot(act_ref[...], wdn_ref[...], preferred_element_type=_F32)
    o_ref[0] = x_ref[0] + gf_ref[0] * y


def _ffn(x, g, sf, cf, gf, w_up, conv_w, conv_b, w_down, tl):
    b, l, d = x.shape
    nh = tl // HALO
    last = l // HALO - 1
    vec = pl.BlockSpec((1, 1, d), lambda i, j: (i, 0, 0))
    whole = lambda a: pl.BlockSpec(a.shape, lambda i, j: (0,) * a.ndim)
    once = lambda a: pl.BlockSpec(a.shape, lambda i, j: (0,) * a.ndim, pipeline_mode=pl.Buffered(1))
    return pl.pallas_call(
        functools.partial(_ffn_kernel, tl=tl),
        grid=(b, l // tl),
        in_specs=[
            pl.BlockSpec((1, HALO, d), lambda i, j: (i, jnp.maximum(j * nh - 1, 0), 0)),
            pl.BlockSpec((1, tl, d), lambda i, j: (i, j, 0)),
            pl.BlockSpec((1, HALO, d), lambda i, j: (i, jnp.minimum((j + 1) * nh, last), 0)),
            pl.BlockSpec((1, d), lambda i, j: (0, 0)),
            vec, vec, vec,
            once(w_up), whole(conv_w), whole(conv_b), once(w_down),
        ],
        out_specs=pl.BlockSpec((1, tl, d), lambda i, j: (i, j, 0)),
        out_shape=jax.ShapeDtypeStruct((b, l, d), _F32),
        scratch_shapes=[
            pltpu.VMEM((tl + 2 * HALO, d), _BF16),
            pltpu.VMEM((2, tl + 2 * HALO, FFN_CHUNK), _F32),
            pltpu.VMEM((2, tl, FFN_CHUNK), _F32),
            pltpu.VMEM((tl, D_FF), _BF16),
        ],
        compiler_params=_cparams(2),
        name="conv_ffn",
    )(x, x, x, g, sf, cf, gf, w_up, conv_w, conv_b, w_down)


def _pad_row(v, n=LANES):
    v = v.reshape(1, -1)
    return jnp.pad(v, ((0, 0), (0, n - v.shape[1])))


def kernel(x, c, ctx, c_ctx, ada_w, ada_b, mix_norm_g, w_in, gdn_conv_w, gdn_a_log, gdn_dt_bias, gdn_norm_g, mla_q_norm_g, mla_w_uq, mla_kv_norm_g, mla_w_ukv, mla_q_head_g, mla_k_head_g, hy_conv_w, hy_conv_b, hy_w1, hy_b1, hy_w2, hy_b2, hy_w3, hy_b3, hy_d, w_out, ffn_norm_g, ffn_w_up, ffn_conv_w, ffn_conv_b, ffn_w_down):
    bsz, seq, d = x.shape
    n_ctx = ctx.shape[1]
    assert n_ctx == TOK and seq % TOK == 0 and bsz < ADA_ROWS
    cond = jnp.concatenate([c, c_ctx[None, :], jnp.zeros((ADA_ROWS - bsz - 1, d), c.dtype)], axis=0)
    for i in range(DEPTH):
        last = i == DEPTH - 1
        mod = _ada_mod(cond, ada_w[i], ada_b[i])
        mod_lat = mod[:bsz, None, :]
        mod_ctx = mod[bsz][None, None, :]
        sa_l, ca_l, ga_l, sf_l, cf_l, gf_l = jnp.split(mod_lat, 6, axis=-1)
        sa_c, ca_c, ga_c, sf_c, cf_c, gf_c = (jnp.broadcast_to(t, (bsz, 1, d)) for t in jnp.split(mod_ctx, 6, axis=-1))

        w_in_p = _pad_w_in(w_in[i])
        g_mix = mix_norm_g[i][None, :]
        qkv, z, ab, mla_in, hy_in = _in_proj(x, ctx, g_mix, sa_l, ca_l, sa_c[:1], ca_c[:1], w_in_p)

        q, k, v, gf, gb, bf, bb = _gdn_prep(qkv, ab, gdn_conv_w[i], _pad_row(gdn_a_log[i]), _pad_row(gdn_dt_bias[i]))
        oc_f, oc_b, ol_f, ol_b = _gdn_scan(q, k, v, gf, gb, bf, bb)

        mla_w = _mla_weights(mla_w_uq[i], mla_w_ukv[i], mla_q_head_g[i], mla_k_head_g[i])
        q_ctx, q_lat, k_all, v_all = _mla_prep(mla_in, mla_q_norm_g[i], mla_kv_norm_g[i], mla_w, seq)
        mla_l = _attention(q_lat, k_all, v_all, n_ctx + seq, ATTN_TQ)
        hy_x0, hy_z, hy_zc, hy_zl = _hy_prep(hy_in, hy_conv_w[i], hy_conv_b[i])
        hy_mlp = (hy_w1[i], hy_b1[i], hy_w2[i], hy_b2[i], hy_w3[i], hy_b3[i])
        hy_l = _hy_conv(_hy_filter(seq, *hy_mlp), hy_zl)
        hd_row = hy_d[i][None, :]

        w_out_b = w_out[i].astype(_BF16)
        w_up_b = ffn_w_up[i].astype(_BF16)
        w_dn_b = ffn_w_down[i].astype(_BF16)
        g_ffn = ffn_norm_g[i][None, :]
        cb = ffn_conv_b[i][None, :]
        gn_row = jnp.tile(gdn_norm_g[i], GDN_HEADS)[None, :]

        x = _out_proj(x, ol_f, ol_b, z, gn_row, mla_l, hy_x0, hy_z, hy_l, hd_row, ga_l, w_out_b, 1)
        x = _ffn(x, g_ffn, sf_l, cf_l, gf_l, w_up_b, ffn_conv_w[i], cb, w_dn_b, FFN_TL)

        if not last:
            mla_c = _attention(q_ctx, k_all, v_all, n_ctx, TOK)
            hy_c = _hy_conv(_hy_filter(n_ctx, *hy_mlp), hy_zc)
            ctx = _out_proj(ctx, oc_f, oc_b, z, gn_row, mla_c, hy_x0, hy_z, hy_c, hd_row, ga_c, w_out_b, 0)
            ctx = _ffn(ctx, g_ffn, sf_c, cf_c, gf_c, w_up_b, ffn_conv_w[i], cb, w_dn_b, TOK)
    return x
```

```python
import functools
import math

import jax
import jax.numpy as jnp
import numpy as np
from jax import lax
from jax.experimental import pallas as pl
from jax.experimental.pallas import tpu as pltpu

D_MODEL = 1024
DEPTH = 2
GRID_W = 64
EPS = 1e-6

GDN_HEADS = 6
GDN_DK = 64
GDN_DV = 64
GDN_CHUNK = 64

MLA_HEADS = 6
MLA_Q_RANK = 256
MLA_KV_RANK = 128
MLA_NOPE = 64
MLA_ROPE = 32
MLA_V = 64
MLA_QK = MLA_NOPE + MLA_ROPE
ROPE_BASE = 10000.0
ROPE_AXIS = MLA_ROPE // 2

HY_CH = 256
HY_BANDS = 16
HY_EMB = 1 + 2 * HY_BANDS
HY_HIDDEN = 64
HY_TARGET = 1e-2
HY_FAST_DECAY_PCT = 0.3
HY_SLOW_DECAY_PCT = 1.5
HY_MAX_DECAY = math.log(HY_TARGET) / HY_FAST_DECAY_PCT
HY_MIN_DECAY = math.log(HY_TARGET) / HY_SLOW_DECAY_PCT

D_FF = 2816

GDN_WIDTH = GDN_HEADS * GDN_DV
MLA_WIDTH = MLA_HEADS * MLA_V
HY_WIDTH = HY_CH
GDN_QKV = GDN_HEADS * (2 * GDN_DK + GDN_DV)
GDN_IN = GDN_QKV + GDN_WIDTH + 4 * GDN_HEADS
MLA_IN = MLA_Q_RANK + MLA_KV_RANK + MLA_ROPE
HY_IN = 3 * HY_CH

LANES = 128
SUBLANES_BF16 = 16
VMEM_LIMIT_BYTES = 56 * 1024 * 1024

TOK = 256
GDN_PAIRS = GDN_HEADS // 2
GDN_BB = 2
N_GATE = 4 * GDN_HEADS

AB_PAD = LANES
MLA_PAD = 512
IN_GROUPS = (GDN_QKV, GDN_WIDTH, AB_PAD, MLA_PAD, HY_IN)
IN_TOTAL = sum(IN_GROUPS)
IN_DTYPES = (jnp.bfloat16, jnp.bfloat16, jnp.float32, jnp.bfloat16, jnp.bfloat16)

FFN_CHUNK = 256
FFN_TL = 1024
HALO = SUBLANES_BF16

_BF16 = jnp.bfloat16
_F32 = jnp.float32
_HI = lax.Precision.HIGHEST
_NT = (((1,), (1,)), ((), ()))
_TN = (((0,), (0,)), ((), ()))


def _cparams(n_axes, sem=None):
    return pltpu.CompilerParams(
        dimension_semantics=sem or ("parallel",) * n_axes, vmem_limit_bytes=VMEM_LIMIT_BYTES)


def _norm_mod(x, g, shift, scale):
    ms = jnp.mean(x * x, axis=-1, keepdims=True)
    y = x * lax.rsqrt(ms + EPS) * g
    return y * (1.0 + scale) + shift


def _silu(x):
    return x * jax.nn.sigmoid(x)


ADA_ROWS = 16
ADA_TN = 1024


def _ada_kernel(c_ref, w_ref, b_ref, o_ref):
    o_ref[...] = jnp.dot(_silu(c_ref[...]), w_ref[...], precision=_HI, preferred_element_type=_F32) + b_ref[...]


def _ada_mod(cond, w, b):
    d, n = w.shape
    return pl.pallas_call(
        _ada_kernel,
        grid=(n // ADA_TN,),
        in_specs=[pl.BlockSpec((ADA_ROWS, d), lambda j: (0, 0)),
                  pl.BlockSpec((d, ADA_TN), lambda j: (0, j)),
                  pl.BlockSpec((1, ADA_TN), lambda j: (0, j))],
        out_specs=pl.BlockSpec((ADA_ROWS, ADA_TN), lambda j: (0, j)),
        out_shape=jax.ShapeDtypeStruct((ADA_ROWS, n), _F32),
        compiler_params=_cparams(1),
        name="ada_mod",
    )(cond, w, b[None, :])


def _in_proj_kernel(x_ref, c_ref, g_ref, sl_ref, cl_ref, sc_ref, cc_ref, w_ref, *out_refs):
    is_ctx = pl.program_id(1) == 0
    x = jnp.where(is_ctx, c_ref[0], x_ref[0])
    shift = jnp.where(is_ctx, sc_ref[0], sl_ref[0])
    scale = jnp.where(is_ctx, cc_ref[0], cl_ref[0])
    h = _norm_mod(x, g_ref[...], shift, scale)
    p = jnp.dot(h.astype(_BF16), w_ref[...], preferred_element_type=_F32)
    off = 0
    for o_ref, n in zip(out_refs, IN_GROUPS):
        o_ref[0] = p[:, off:off + n].astype(o_ref.dtype)
        off += n


def _in_proj(x, ctx, g, shift_l, scale_l, shift_c, scale_c, w_pad):
    b, l, d = x.shape
    nt = 1 + l // TOK
    vec_l = pl.BlockSpec((1, 1, d), lambda i, j: (i, 0, 0))
    vec_c = pl.BlockSpec((1, 1, d), lambda i, j: (0, 0, 0))
    return pl.pallas_call(
        _in_proj_kernel,
        grid=(b, nt),
        in_specs=[
            pl.BlockSpec((1, TOK, d), lambda i, j: (i, jnp.maximum(j - 1, 0), 0)),
            pl.BlockSpec((1, TOK, d), lambda i, j: (i, 0, 0)),
            pl.BlockSpec((1, d), lambda i, j: (0, 0)),
            vec_l, vec_l, vec_c, vec_c,
            pl.BlockSpec((d, IN_TOTAL), lambda i, j: (0, 0)),
        ],
        out_specs=[pl.BlockSpec((1, TOK, n), lambda i, j: (i, j, 0)) for n in IN_GROUPS],
        out_shape=[jax.ShapeDtypeStruct((b, nt * TOK, n), dt) for n, dt in zip(IN_GROUPS, IN_DTYPES)],
        compiler_params=_cparams(2),
        name="in_proj",
    )(x, ctx, g, shift_l, scale_l, shift_c, scale_c, w_pad)


def _pad_w_in(w_in):
    s1 = GDN_QKV + GDN_WIDTH
    s2 = GDN_IN
    s3 = GDN_IN + MLA_IN
    d = w_in.shape[0]
    z = lambda n: jnp.zeros((d, n), w_in.dtype)
    parts = [w_in[:, :s1], w_in[:, s1:s2], z(AB_PAD - N_GATE),
             w_in[:, s2:s3], z(MLA_PAD - MLA_IN), w_in[:, s3:]]
    return jnp.concatenate(parts, axis=1).astype(_BF16)


def _gdn_consts():
    r = np.arange(TOK)
    same = (r[:, None] // GDN_CHUNK) == (r[None, :] // GDN_CHUNK)
    tril = (same & (r[None, :] <= r[:, None])).astype(np.float32)
    triu = (same & (r[None, :] >= r[:, None])).astype(np.float32)
    c = np.arange(GDN_WIDTH)
    head_ones = (c[:, None] // GDN_DK == c[None, :] // GDN_DK).astype(np.float32)
    expand = np.zeros((LANES, 4 * GDN_WIDTH), np.float32)
    for k in range(4):
        for h in range(GDN_HEADS):
            expand[k * GDN_HEADS + h, k * GDN_WIDTH + h * GDN_DK:k * GDN_WIDTH + (h + 1) * GDN_DK] = 1.0
    return tril, triu, head_ones, expand


def _gdn_prep_consts():
    tril, triu, head_ones, expand = _gdn_consts()
    b = lambda a: jnp.asarray(a, _BF16)
    return b(tril), b(triu), b(np.concatenate([head_ones] * 2, axis=0)), b(np.concatenate([expand] * 3, axis=0))


def _split_bf16(x, n):
    terms = []
    for _ in range(n):
        t = x.astype(_BF16)
        terms.append(t)
        x = x - t.astype(_F32)
    return terms


def _sum_terms(y, n):
    w = y.shape[1] // n
    out = y[:, :w]
    for t in range(1, n):
        out = out + y[:, t * w:(t + 1) * w]
    return out


def _gdn_prep_kernel(xp_ref, x_ref, xn_ref, ab_ref, cw_ref, alog_ref, dt_ref, tril_ref, triu_ref, hones_ref,
                     exp_ref, q_ref, k_ref, v_ref, gf_ref, gb_ref, bf_ref, bb_ref, xe_ref):
    j = pl.program_id(1)
    nt = pl.num_programs(1)
    pv = (j >= 2).astype(_F32)
    nv = jnp.logical_and(j >= 1, j < nt - 1).astype(_F32)
    xe_ref[0:HALO] = xp_ref[0].astype(_F32) * pv
    xe_ref[HALO:HALO + TOK] = x_ref[0].astype(_F32)
    xe_ref[HALO + TOK:] = xn_ref[0].astype(_F32) * nv
    cw = cw_ref[...]
    y = (xe_ref[HALO - 1:HALO - 1 + TOK] * cw[0:1] + xe_ref[HALO:HALO + TOK] * cw[1:2]
         + xe_ref[HALO + 1:HALO + 1 + TOK] * cw[2:3])
    y = _silu(y)
    hk = GDN_HEADS * GDN_DK
    q, k, v = y[:, :hk], y[:, hk:2 * hk], y[:, 2 * hk:]
    sq = jnp.concatenate([q * q, k * k], axis=0)
    ss = jnp.dot(jnp.concatenate(_split_bf16(sq, 2), axis=1), hones_ref[...], preferred_element_type=_F32)
    q_ref[0] = q * lax.rsqrt(ss[:TOK] + EPS) * (GDN_DK ** -0.5)
    k_ref[0] = k * lax.rsqrt(ss[TOK:] + EPS)
    v_ref[0] = v

    ab = ab_ref[0]
    lane = lax.broadcasted_iota(jnp.int32, ab.shape, 1)
    a_in = ab + dt_ref[...]
    softplus = jnp.maximum(a_in, 0.0) + jnp.log(1.0 + jnp.exp(-jnp.abs(a_in)))
    g = jnp.where(lane < 2 * GDN_HEADS, -jnp.exp(alog_ref[...]) * softplus, 0.0)
    g3 = jnp.concatenate(_split_bf16(g, 3), axis=1)
    gc_f = _sum_terms(jnp.dot(tril_ref[...], g3, preferred_element_type=_F32), 3)
    gc_b = _sum_terms(jnp.dot(triu_ref[...], g3, preferred_element_type=_F32), 3)
    cols = jnp.where(lane < GDN_HEADS, gc_f, jnp.where(lane < 2 * GDN_HEADS, gc_b, jax.nn.sigmoid(ab)))
    wide = jnp.dot(jnp.concatenate(_split_bf16(cols, 3), axis=1), exp_ref[...],
                   preferred_element_type=_F32)
    gf_ref[0] = wide[:, 0:GDN_WIDTH]
    gb_ref[0] = wide[:, GDN_WIDTH:2 * GDN_WIDTH]
    bf_ref[0] = wide[:, 2 * GDN_WIDTH:3 * GDN_WIDTH]
    bb_ref[0] = wide[:, 3 * GDN_WIDTH:]


def _gdn_prep(qkv, ab, conv_w, a_log_row, dt_row):
    b, lt, _ = qkv.shape
    nt = lt // TOK
    nh = TOK // HALO
    last = lt // HALO - 1
    consts = list(_gdn_prep_consts())
    whole = lambda a: pl.BlockSpec(a.shape, lambda i, j: (0,) * a.ndim)
    row = lambda n: pl.BlockSpec((1, TOK, n), lambda i, j: (i, j, 0))
    return pl.pallas_call(
        _gdn_prep_kernel,
        grid=(b, nt),
        in_specs=[
            pl.BlockSpec((1, HALO, GDN_QKV), lambda i, j: (i, jnp.maximum(j * nh - 1, 0), 0)),
            row(GDN_QKV),
            pl.BlockSpec((1, HALO, GDN_QKV), lambda i, j: (i, jnp.minimum((j + 1) * nh, last), 0)),
            row(AB_PAD), whole(conv_w), whole(a_log_row), whole(dt_row),
        ] + [whole(a) for a in consts],
        out_specs=[row(GDN_WIDTH)] * 7,
        out_shape=[jax.ShapeDtypeStruct((b, lt, GDN_WIDTH), _F32)] * 7,
        scratch_shapes=[pltpu.VMEM((TOK + 2 * HALO, GDN_QKV), _F32)],
        compiler_params=_cparams(2),
        name="gdn_prep",
    )(qkv, qkv, qkv, ab, conv_w, a_log_row, dt_row, *consts)


def _block_diag(z, left):
    return jnp.concatenate([jnp.where(left, z, 0.0), jnp.where(left, 0.0, z)], axis=0).astype(_BF16)


def _mm(a, b):
    return jnp.dot(a.astype(_BF16), b, preferred_element_type=_F32)


def _gdn_intra(probs, masks):
    left, eye2, ones64, tri, same_blk = masks
    c = GDN_CHUNK
    qs, ks, vs, gxs, bxs, bws = zip(*probs)
    qk_kk = [lax.dot_general(jnp.concatenate([q, k], axis=0).astype(_BF16), _block_diag(k, left), _NT,
                             preferred_element_type=_F32) for q, k in zip(qs, ks)]
    rs = []
    for gx in gxs:
        d0 = jnp.where(eye2, gx, 0.0)
        t_hi = d0.astype(_BF16)
        r1 = d0 - t_hi.astype(_F32)
        t_mid = r1.astype(_BF16)
        t_lo = (r1 - t_mid.astype(_F32)).astype(_BF16)
        r3 = jnp.dot(ones64, jnp.concatenate([t_hi, t_mid, t_lo], axis=1), preferred_element_type=_F32)
        rs.append(r3[:, :LANES] + r3[:, LANES:2 * LANES] + r3[:, 2 * LANES:])
    a_s, qkm, egs = [], [], []
    for x, gx, bx, r, bw in zip(qk_kk, gxs, bxs, rs, bws):
        incl, strict = tri[bw]
        dec = jnp.where(incl, jnp.exp(jnp.where(incl, gx - r, 0.0)), 0.0)
        a_s.append(bx * x[c:] * jnp.where(strict, dec, 0.0))
        qkm.append(x[:c] * dec)
        egs.append(jnp.exp(gx))
    eye_f = jnp.where(eye2, 1.0, 0.0)
    base = same_blk[8]
    d1 = [jnp.where(base, a, 0.0) for a in a_s]
    ps = [eye_f - d for d in d1]
    d2 = [_mm(d, _block_diag(d, left)) for d in d1]
    d2_bd = [_block_diag(d, left) for d in d2]
    ps = [p + _mm(p, bd) for p, bd in zip(ps, d2_bd)]
    d4 = [_mm(d, bd) for d, bd in zip(d2, d2_bd)]
    ps = [p + _mm(p, _block_diag(d, left)) for p, d in zip(ps, d4)]
    for blk in (8, 16, 32):
        off = jnp.logical_and(same_blk[2 * blk], jnp.logical_not(same_blk[blk]))
        t1 = [_mm(p, _block_diag(jnp.where(off, a, 0.0), left)) for p, a in zip(ps, a_s)]
        ps = [p - _mm(t, _block_diag(p, left)) for p, t in zip(ps, t1)]
    out = []
    for p, q, k, v, gx, bx, eg, qk, bw in zip(ps, qs, ks, vs, gxs, bxs, egs, qkm, bws):
        tot = gx[0:1] if bw else gx[c - 1:c]
        lhs = jnp.concatenate([k * eg, q * eg], axis=0).astype(_BF16)
        out.append((p, lhs, bx, bx * v, qk, k * jnp.exp(tot - gx), tot))
    return out


def _gdn_state_step(chains, left, diag_blocks):
    c = GDN_CHUNK
    ys = [jnp.dot(x[1], s.astype(_BF16), preferred_element_type=_F32) for s, x in chains]
    resid = [x[3] - x[2] * y[:c] for (s, x), y in zip(chains, ys)]
    deltas = [_mm(x[0], _block_diag(r, left)) for (s, x), r in zip(chains, resid)]
    os_ = [y[c:] + _mm(x[4], _block_diag(d, left)) for (s, x), y, d in zip(chains, ys, deltas)]
    upds = [lax.dot_general(x[5].astype(_BF16), d.astype(_BF16), _TN, preferred_element_type=_F32)
            for (s, x), d in zip(chains, deltas)]
    new_s = [s * jnp.exp(x[6]) + jnp.where(diag_blocks, u, 0.0) for (s, x), u in zip(chains, upds)]
    return list(zip(os_, new_s))


def _gdn_scan_kernel(cq, ck, cv, cgf, cbf, cgb, cbb, fq, fk, fv, fg, fb, rq, rk, rv, rg, rb,
                        ocf_ref, ocb_ref, of_ref, ob_ref, sf_ref, sb_ref):
    step = pl.program_id(1)
    is_ctx = step == 0

    @pl.when(is_ctx)
    def _():
        sf_ref[...] = jnp.zeros_like(sf_ref)
        sb_ref[...] = jnp.zeros_like(sb_ref)

    c = GDN_CHUNK
    nc = TOK // c
    li = lax.broadcasted_iota(jnp.int32, (c, LANES), 1)
    ri = lax.broadcasted_iota(jnp.int32, (c, LANES), 0)
    lj = li & (c - 1)
    left = li < c
    eye2 = lj == ri
    ones64 = jnp.ones((c, c), _BF16)
    r2 = lax.broadcasted_iota(jnp.int32, (LANES, LANES), 0)
    c2 = lax.broadcasted_iota(jnp.int32, (LANES, LANES), 1)
    diag_blocks = (r2 // c) == (c2 // c)
    tri = {False: (ri >= lj, ri > lj), True: (ri <= lj, ri < lj)}
    same_blk = {b: (ri // b) == (lj // b) for b in (8, 16, 32, 64)}
    masks = (left, eye2, ones64, tri, same_blk)

    nbb = fq.shape[0]
    ctx_refs = {False: (cq, ck, cv, cgf, cbf), True: (cq, ck, cv, cgb, cbb)}
    lat_refs = {False: (fq, fk, fv, fg, fb), True: (rq, rk, rv, rg, rb)}
    vals = {(bw, e): [jnp.where(is_ctx, cr[e], r[e]) for cr, r in zip(ctx_refs[bw], lat_refs[bw])]
            for bw in (False, True) for e in range(nbb)}
    chain_keys = [(bw, e, p) for bw in (False, True) for e in range(nbb) for p in range(GDN_PAIRS)]
    prob_keys = [(bw, e, p, n) for bw, e, p in chain_keys for n in range(nc)]
    probs = [tuple(a[n * c:(n + 1) * c, p * LANES:(p + 1) * LANES] for a in vals[(bw, e)]) + (bw,)
             for bw, e, p, n in prob_keys]
    intra = dict(zip(prob_keys, _gdn_intra(probs, masks)))

    state_ref = lambda bw: sb_ref if bw else sf_ref
    states = {(bw, e, p): state_ref(bw)[e, p] for bw, e, p in chain_keys}
    outs = {}
    for t in range(nc):
        ns = {kk: (nc - 1 - t if kk[0] else t) for kk in chain_keys}
        res = _gdn_state_step([(states[kk], intra[kk + (ns[kk],)]) for kk in chain_keys], left, diag_blocks)
        for kk, (o, st) in zip(chain_keys, res):
            outs[kk + (ns[kk],)] = o
            states[kk] = st
    for bw, e, p in chain_keys:
        state_ref(bw)[e, p] = states[(bw, e, p)]

    for bw, o_ref, oc_ref in ((False, of_ref, ocf_ref), (True, ob_ref, ocb_ref)):
        for e in range(nbb):
            o_all = jnp.concatenate(
                [jnp.concatenate([outs[(bw, e, p, n)] for p in range(GDN_PAIRS)], axis=1) for n in range(nc)], axis=0)

            @pl.when(is_ctx)
            def _(o_all=o_all, oc_ref=oc_ref, e=e):
                oc_ref[e] = o_all

            @pl.when(jnp.logical_not(is_ctx))
            def _(o_all=o_all, o_ref=o_ref, e=e):
                o_ref[e] = o_all


def _gdn_scan(q, k, v, gf, gb, bf, bb):
    b, lt, w = q.shape
    nl = lt // TOK - 1
    nbb = GDN_BB
    ctx = pl.BlockSpec((nbb, TOK, w), lambda i, s: (i, 0, 0))
    fwd = pl.BlockSpec((nbb, TOK, w), lambda i, s: (i, jnp.maximum(s, 1), 0))
    bwd = pl.BlockSpec((nbb, TOK, w), lambda i, s: (i, nl + 1 - jnp.maximum(s, 1), 0))
    fwd_o = pl.BlockSpec((nbb, TOK, w), lambda i, s: (i, jnp.maximum(s, 1) - 1, 0))
    bwd_o = pl.BlockSpec((nbb, TOK, w), lambda i, s: (i, nl - jnp.maximum(s, 1), 0))
    return pl.pallas_call(
        _gdn_scan_kernel,
        grid=(b // nbb, nl + 1),
        in_specs=[ctx] * 7 + [fwd] * 5 + [bwd] * 5,
        out_specs=[ctx, ctx, fwd_o, bwd_o],
        out_shape=[jax.ShapeDtypeStruct((b, TOK, w), _F32)] * 2 + [jax.ShapeDtypeStruct((b, nl * TOK, w), _F32)] * 2,
        scratch_shapes=[pltpu.VMEM((nbb, GDN_PAIRS, LANES, LANES), _F32)] * 2,
        compiler_params=_cparams(2, ("parallel", "arbitrary")),
        name="gdn_scan",
    )(q, k, v, gf, bf, gb, bb, q, k, v, gf, bf, q, k, v, gb, bb)


MLA_HEAD_PAD = LANES
MLA_WIDE = MLA_HEADS * MLA_HEAD_PAD
ATTN_TQ = 512
ATTN_TK = 512
ATTN_HEADS = 2


def _rope_tables(n_ctx, seq):
    rows = seq // GRID_W
    row = np.repeat(np.arange(rows, dtype=np.float64), GRID_W)
    col = np.tile(np.arange(GRID_W, dtype=np.float64), rows)
    inv = ROPE_BASE ** (-np.arange(0, ROPE_AXIS, 2, dtype=np.float64) / ROPE_AXIS)
    ang = np.concatenate([row[:, None] * inv, col[:, None] * inv], axis=-1)
    cos, sin = np.cos(ang), np.sin(ang)
    half = MLA_ROPE // 2
    c = np.ones((n_ctx + seq, MLA_HEAD_PAD))
    s = np.zeros((n_ctx + seq, MLA_HEAD_PAD))
    c[n_ctx:, MLA_NOPE:MLA_NOPE + half] = cos
    c[n_ctx:, MLA_NOPE + half:MLA_QK] = cos
    s[n_ctx:, MLA_NOPE:MLA_NOPE + half] = -sin
    s[n_ctx:, MLA_NOPE + half:MLA_QK] = sin
    return c.astype(np.float32), s.astype(np.float32)


def _mla_prep_kernel(p_ref, qn_ref, kvn_ref, wq_ref, wk_ref, wv_ref, qg_ref, kg_ref, cos_ref, sin_ref,
                     qc_ref, ql_ref, k_ref, v_ref):
    p = p_ref[0].astype(_F32)
    cq = p[:, :MLA_Q_RANK]
    ckv = p[:, MLA_Q_RANK:MLA_Q_RANK + MLA_KV_RANK]
    kr = p[:, MLA_Q_RANK + MLA_KV_RANK:]
    cq = (cq * lax.rsqrt(jnp.mean(cq * cq, axis=-1, keepdims=True) + EPS) * qn_ref[...]).astype(_BF16)
    ckv = (ckv * lax.rsqrt(jnp.mean(ckv * ckv, axis=-1, keepdims=True) + EPS) * kvn_ref[...]).astype(_BF16)
    q = jnp.dot(cq, wq_ref[...], preferred_element_type=_F32)
    k = jnp.dot(jnp.concatenate([ckv, kr.astype(_BF16)], axis=1), wk_ref[...], preferred_element_type=_F32)
    lane = lax.broadcasted_iota(jnp.int32, (TOK, MLA_WIDE), 1) & (MLA_HEAD_PAD - 1)
    v = jnp.dot(ckv, wv_ref[...], preferred_element_type=_F32) + jnp.where(lane == MLA_V, 1.0, 0.0)
    cos = jnp.concatenate([cos_ref[...]] * MLA_HEADS, axis=1)
    sin = jnp.concatenate([sin_ref[...]] * MLA_HEADS, axis=1)
    half = MLA_ROPE // 2
    first = jnp.logical_and(lane >= MLA_NOPE, lane < MLA_NOPE + half)
    second = jnp.logical_and(lane >= MLA_NOPE + half, lane < MLA_QK)

    def head_norm_rope(x, g):
        parts = []
        for h in range(MLA_HEADS):
            xh = x[:, h * MLA_HEAD_PAD:(h + 1) * MLA_HEAD_PAD]
            ms = jnp.sum(xh * xh, axis=-1, keepdims=True) * (1.0 / MLA_QK)
            parts.append(xh * lax.rsqrt(ms + EPS))
        xn = jnp.concatenate(parts, axis=1) * g
        up = pltpu.roll(xn, half, 1)
        down = pltpu.roll(xn, MLA_WIDE - half, 1)
        swapped = jnp.where(first, down, jnp.where(second, up, 0.0))
        return xn * cos + swapped * sin

    qf = head_norm_rope(q, qg_ref[...]) * (MLA_QK ** -0.5 * math.log2(math.e))
    kf = head_norm_rope(k, kg_ref[...])
    is_ctx = pl.program_id(1) == 0
    for h in range(MLA_HEADS):
        sl = slice(h * MLA_HEAD_PAD, (h + 1) * MLA_HEAD_PAD)
        k_ref[0, h] = kf[:, sl].astype(_BF16)
        v_ref[0, h] = v[:, sl].astype(_BF16)

    @pl.when(is_ctx)
    def _():
        for h in range(MLA_HEADS):
            qc_ref[0, h] = qf[:, h * MLA_HEAD_PAD:(h + 1) * MLA_HEAD_PAD].astype(_BF16)

    @pl.when(jnp.logical_not(is_ctx))
    def _():
        for h in range(MLA_HEADS):
            ql_ref[0, h] = qf[:, h * MLA_HEAD_PAD:(h + 1) * MLA_HEAD_PAD].astype(_BF16)


def _mla_weights(w_uq, w_ukv, q_head_g, k_head_g):
    pad = MLA_HEAD_PAD
    wq = jnp.pad(w_uq.reshape(MLA_Q_RANK, MLA_HEADS, MLA_QK), ((0, 0), (0, 0), (0, pad - MLA_QK)))
    wkv = w_ukv.reshape(MLA_KV_RANK, MLA_HEADS, MLA_NOPE + MLA_V)
    wk = jnp.pad(wkv[:, :, :MLA_NOPE], ((0, 0), (0, 0), (0, pad - MLA_NOPE)))
    wv = jnp.pad(wkv[:, :, MLA_NOPE:], ((0, 0), (0, 0), (0, pad - MLA_V)))
    sel = np.zeros((MLA_PAD - MLA_Q_RANK - MLA_KV_RANK, MLA_WIDE), np.float32)
    for h in range(MLA_HEADS):
        for r in range(MLA_ROPE):
            sel[r, h * pad + MLA_NOPE + r] = 1.0
    tile_g = lambda g: jnp.tile(jnp.pad(g, (0, pad - MLA_QK)), MLA_HEADS)[None, :]
    flat = lambda w: w.reshape(w.shape[0], MLA_WIDE).astype(_BF16)
    wk_sel = jnp.concatenate([flat(wk), jnp.asarray(sel, _BF16)], axis=0)
    return flat(wq), wk_sel, flat(wv), tile_g(q_head_g), tile_g(k_head_g)


def _mla_prep(mla_in, q_norm_g, kv_norm_g, weights, seq):
    b, lt, _ = mla_in.shape
    nt = lt // TOK
    wq, wk, wv, qg, kg = weights
    cos, sin = (jnp.asarray(t) for t in _rope_tables(lt - seq, seq))
    whole = lambda a: pl.BlockSpec(a.shape, lambda i, j: (0,) * a.ndim)
    tab = pl.BlockSpec((TOK, MLA_HEAD_PAD), lambda i, j: (j, 0))
    hd = lambda f: pl.BlockSpec((1, MLA_HEADS, TOK, MLA_HEAD_PAD), f)
    qn, kvn = q_norm_g[None, :], kv_norm_g[None, :]
    shp = lambda t: jax.ShapeDtypeStruct((b, MLA_HEADS, t, MLA_HEAD_PAD), _BF16)
    return pl.pallas_call(
        _mla_prep_kernel,
        grid=(b, nt),
        in_specs=[pl.BlockSpec((1, TOK, MLA_PAD), lambda i, j: (i, j, 0)), whole(qn), whole(kvn),
                  whole(wq), whole(wk), whole(wv), whole(qg), whole(kg), tab, tab],
        out_specs=[hd(lambda i, j: (i, 0, 0, 0)), hd(lambda i, j: (i, 0, jnp.maximum(j, 1) - 1, 0)),
                   hd(lambda i, j: (i, 0, j, 0)), hd(lambda i, j: (i, 0, j, 0))],
        out_shape=[shp(TOK), shp(lt - TOK), shp(lt), shp(lt)],
        compiler_params=_cparams(2, ("parallel", "arbitrary")),
        name="mla_prep",
    )(mla_in, qn, kvn, wq, wk, wv, qg, kg, cos, sin)


def _attn_kernel(q_ref, k_ref, v_ref, o_ref):
    n_keys = k_ref.shape[2]
    starts = list(range(0, n_keys, ATTN_TK))
    heads = range(q_ref.shape[1])

    def scores(h, lo):
        hi = min(lo + ATTN_TK, n_keys)
        return lax.dot_general(q_ref[0, h], k_ref[0, h, lo:hi, :], _NT, preferred_element_type=_F32)

    nxt = [scores(h, starts[0]) for h in heads]
    m = [None] * len(heads)
    acc = [None] * len(heads)
    for n, lo in enumerate(starts):
        cur = nxt
        if n + 1 < len(starts):
            nxt = [scores(h, starts[n + 1]) for h in heads]
        hi = min(lo + ATTN_TK, n_keys)
        for h in heads:
            s = cur[h]
            m_blk = jnp.max(s, axis=-1, keepdims=True)
            m_new = m_blk if n == 0 else jnp.maximum(m[h], m_blk)
            p = jnp.exp2(s - m_new).astype(_BF16)
            pv = jnp.dot(p, v_ref[0, h, lo:hi, :], preferred_element_type=_F32)
            acc[h] = pv if n == 0 else acc[h] * jnp.exp2(m[h] - m_new) + pv
            m[h] = m_new
    o_ref[0] = jnp.concatenate([a[:, :MLA_V] / a[:, MLA_V:MLA_V + 1] for a in acc], axis=1)


def _attention(q, k, v, n_keys, tq):
    b, h, t, w = q.shape
    return pl.pallas_call(
        _attn_kernel,
        grid=(b, h // ATTN_HEADS, t // tq),
        in_specs=[pl.BlockSpec((1, ATTN_HEADS, tq, w), lambda i, j, l: (i, j, l, 0)),
                  pl.BlockSpec((1, ATTN_HEADS, n_keys, w), lambda i, j, l: (i, j, 0, 0)),
                  pl.BlockSpec((1, ATTN_HEADS, n_keys, w), lambda i, j, l: (i, j, 0, 0))],
        out_specs=pl.BlockSpec((1, tq, ATTN_HEADS * MLA_V), lambda i, j, l: (i, l, j)),
        out_shape=jax.ShapeDtypeStruct((b, t, h * MLA_V), _F32),
        compiler_params=_cparams(3),
        name="mla_attention",
    )(q, k, v)


HY_BLK = 256
HY_LO = 128
HY_CB = 16


def _hy_prep_kernel(xp_ref, x_ref, xn_ref, cw_ref, cb_ref, x0_ref, z_ref, zc_ref, zl_ref, xe_ref):
    j = pl.program_id(1)
    nt = pl.num_programs(1)
    pv = (j >= 2).astype(_F32)
    nv = jnp.logical_and(j >= 1, j < nt - 1).astype(_F32)
    xe_ref[0:HALO] = xp_ref[0].astype(_F32) * pv
    xe_ref[HALO:HALO + TOK] = x_ref[0].astype(_F32)
    xe_ref[HALO + TOK:] = xn_ref[0].astype(_F32) * nv
    cw = cw_ref[...]
    u = (xe_ref[HALO - 1:HALO - 1 + TOK] * cw[0:1] + xe_ref[HALO:HALO + TOK] * cw[1:2]
         + xe_ref[HALO + 1:HALO + 1 + TOK] * cw[2:3] + cb_ref[...])
    x0_ref[0] = u[:, :HY_CH]
    z = u[:, 2 * HY_CH:] * u[:, HY_CH:2 * HY_CH]
    z_ref[0] = z

    @pl.when(j == 0)
    def _():
        zc_ref[0] = z.astype(_BF16)

    @pl.when(j > 0)
    def _():
        zl_ref[0] = z.astype(_BF16)


def _hy_prep(hy_in, conv_w, conv_b):
    b, lt, _ = hy_in.shape
    nh = TOK // HALO
    last = lt // HALO - 1
    whole = lambda a: pl.BlockSpec(a.shape, lambda i, j: (0,) * a.ndim)
    row = lambda n: pl.BlockSpec((1, TOK, n), lambda i, j: (i, j, 0))
    cb = conv_b[None, :]
    return pl.pallas_call(
        _hy_prep_kernel,
        grid=(b, lt // TOK),
        in_specs=[pl.BlockSpec((1, HALO, HY_IN), lambda i, j: (i, jnp.maximum(j * nh - 1, 0), 0)),
                  row(HY_IN),
                  pl.BlockSpec((1, HALO, HY_IN), lambda i, j: (i, jnp.minimum((j + 1) * nh, last), 0)),
                  whole(conv_w), whole(cb)],
        out_specs=[row(HY_CH), row(HY_CH),
                   pl.BlockSpec((1, TOK, HY_CH), lambda i, j: (i, 0, 0)),
                   pl.BlockSpec((1, TOK, HY_CH), lambda i, j: (i, jnp.maximum(j, 1) - 1, 0))],
        out_shape=[jax.ShapeDtypeStruct((b, lt, HY_CH), _F32)] * 2
        + [jax.ShapeDtypeStruct((b, TOK, HY_CH), _BF16), jax.ShapeDtypeStruct((b, lt - TOK, HY_CH), _BF16)],
        scratch_shapes=[pltpu.VMEM((TOK + 2 * HALO, HY_IN), _F32)],
        compiler_params=_cparams(2, ("parallel", "arbitrary")),
        name="hyena_prep",
    )(hy_in, hy_in, hy_in, conv_w, cb)


def _hy_filter_consts(l):
    def emb(t):
        t = t.astype(np.float64)
        t_norm = t / max(l - 1, 1)
        bands = np.linspace(1e-4, HY_BANDS - 1, HY_BANDS)
        ang = 2.0 * math.pi * t[:, None] * bands[None, :] / l
        z = np.concatenate([t_norm[:, None], np.cos(ang), np.sin(ang)], axis=-1)
        return np.pad(z, ((0, 0), (0, LANES - HY_EMB))), t_norm[:, None]
    r = np.arange(l)
    e_rev, tn_rev = emb(l - 1 - r)
    e_sh, tn_sh = emb(r + 1)
    deltas = np.abs(np.linspace(HY_MIN_DECAY, HY_MAX_DECAY, HY_CH))[None, :]
    f = lambda a: np.asarray(a, np.float32)
    return f(e_rev), f(e_sh), f(tn_rev), f(tn_sh), f(deltas)


def _hy_filter_kernel(er_ref, es_ref, tr_ref, ts_ref, dl_ref, w1_ref, b1_ref, w2_ref, b2_ref, w3_ref, b3_ref, o_ref):
    l = er_ref.shape[0]

    def mlp(e, col):
        h = jnp.sin(jnp.dot(e, w1_ref[...], precision=_HI, preferred_element_type=_F32) + b1_ref[...])
        h = jnp.sin(jnp.dot(h, w2_ref[...], precision=_HI, preferred_element_type=_F32) + b2_ref[...])
        return (jnp.dot(h, w3_ref[:, col * HY_CH:(col + 1) * HY_CH], precision=_HI, preferred_element_type=_F32)
                + b3_ref[:, col * HY_CH:(col + 1) * HY_CH])

    hf = mlp(er_ref[...], 0) * jnp.exp(-tr_ref[...] * dl_ref[...])
    hb = mlp(es_ref[...], 1) * jnp.exp(-ts_ref[...] * dl_ref[...])
    row = lax.broadcasted_iota(jnp.int32, hb.shape, 0)
    hb = jnp.where(row < l - 1, hb, 0.0)
    norm = jnp.sum(jnp.abs(hf), axis=0, keepdims=True) + jnp.sum(jnp.abs(hb), axis=0, keepdims=True)
    o_ref[...] = jnp.transpose(jnp.concatenate([hf, hb], axis=0) / norm)


def _hy_filter(l, w1, b1, w2, b2, w3, b3):
    consts = [jnp.asarray(a) for a in _hy_filter_consts(l)]
    w1p = jnp.pad(w1, ((0, LANES - HY_EMB), (0, 0)))
    args = consts + [w1p, b1[None, :], w2, b2[None, :], w3, b3[None, :]]
    return pl.pallas_call(
        _hy_filter_kernel,
        out_shape=jax.ShapeDtypeStruct((HY_CH, 2 * l), _F32),
        compiler_params=pltpu.CompilerParams(vmem_limit_bytes=VMEM_LIMIT_BYTES),
        name="hyena_filter",
    )(*args)


def _hy_conv_kernel(f_ref, z_ref, y_ref, g_ref, *, nblk, nb):
    cols = z_ref.shape[2]
    lane = lax.broadcasted_iota(jnp.int32, (HY_BLK, cols), 1)
    lane_p = lax.broadcasted_iota(jnp.int32, (HY_BLK // 2, cols), 1)
    gs = 4 if nblk >= 4 else 1

    def build(ch, slot):
        rows = SUBLANES_BF16
        base = pltpu.roll(jnp.broadcast_to(f_ref[ch], (rows, f_ref.shape[2])), 1, 1, stride=1, stride_axis=0)
        base = base.astype(_BF16)
        g_ref[slot, 0:rows, :] = base
        packed = pltpu.bitcast(base, jnp.int32)
        for a in range(1, HY_LO // rows):
            g_ref[slot, a * rows:(a + 1) * rows, :] = pltpu.bitcast(pltpu.roll(packed, rows * a, 1), _BF16)

    def convolve(ch, slot):
        z = z_ref[ch]
        zp = pltpu.bitcast(z, jnp.int32)
        real = nb * nblk

        def shift(x, lanes, blocks, sign, fill):
            if blocks == 0:
                return x
            s = nb * blocks
            if sign > 0:
                return jnp.where(lanes >= s, pltpu.roll(x, s, 1), fill)
            return jnp.where(lanes < real - s, pltpu.roll(x, cols - s, 1), fill)

        acc = {}
        for sign in (1, -1):
            for a in range(-(-nblk // gs)):
                ds = [gs * a + r for r in range(gs) if gs * a + r < nblk and not (sign < 0 and gs * a + r == 0)]
                if not ds:
                    continue
                wins = []
                for d in ds:
                    o = HY_BLK * (nblk - sign * d)
                    wins += [g_ref[slot, :, o:o + HY_BLK], g_ref[slot, :, o - HY_LO:o - HY_LO + HY_BLK]]
                zs = pltpu.bitcast(shift(zp, lane_p, gs * a, sign, 0), _BF16)
                part = jnp.dot(jnp.concatenate(wins, axis=0), zs, preferred_element_type=_F32)
                for n, d in enumerate(ds):
                    key = (sign, d - gs * a)
                    blk = part[n * HY_BLK:(n + 1) * HY_BLK]
                    acc[key] = blk if key not in acc else acc[key] + blk
        y = None
        for (sign, r), v in acc.items():
            v = shift(v, lane, r, sign, 0.0)
            y = v if y is None else y + v
        y_ref[ch] = y

    n_ch = f_ref.shape[0]
    build(0, 0)

    def pair(it, carry):
        ch = 2 * it
        build(ch + 1, 1)
        convolve(ch, 0)
        build(jnp.minimum(ch + 2, n_ch - 1), 0)
        convolve(ch + 1, 1)
        return carry

    lax.fori_loop(0, n_ch // 2, pair, 0)


def _hy_conv(fline, z):
    b, l, c = z.shape
    nblk = l // HY_BLK
    cols = max(nblk * b, LANES)
    zall = z.reshape(b, nblk, HY_BLK, c).transpose(3, 2, 1, 0).reshape(c, HY_BLK, nblk * b)
    zall = jnp.pad(zall.astype(_BF16), ((0, 0), (0, 0), (0, cols - nblk * b)))
    y = pl.pallas_call(
        functools.partial(_hy_conv_kernel, nblk=nblk, nb=b),
        grid=(c // HY_CB,),
        in_specs=[pl.BlockSpec((HY_CB, 1, 2 * l), lambda i: (i, 0, 0)),
                  pl.BlockSpec((HY_CB, HY_BLK, cols), lambda i: (i, 0, 0))],
        out_specs=pl.BlockSpec((HY_CB, HY_BLK, cols), lambda i: (i, 0, 0)),
        out_shape=jax.ShapeDtypeStruct((c, HY_BLK, cols), _F32),
        scratch_shapes=[pltpu.VMEM((2, HY_LO, 2 * l), _BF16)],
        compiler_params=_cparams(1),
        name="hyena_conv",
    )(fline.reshape(c, 1, 2 * l), zall)
    return y[:, :, :nblk * b].reshape(c, HY_BLK, nblk, b).transpose(3, 2, 1, 0).reshape(b, l, c)


def _out_proj_kernel(x_ref, of_ref, ob_ref, z_ref, gn_ref, hones_ref, mla_ref, hx_ref, hz_ref, hy_ref, hd_ref,
                     ga_ref, w_ref, o_ref):
    o = of_ref[0] + ob_ref[0]
    ms = jnp.dot(jnp.concatenate(_split_bf16(o * o, 2), axis=1), hones_ref[...],
                 preferred_element_type=_F32) * (1.0 / GDN_DV)
    gdn = o * lax.rsqrt(ms + EPS) * gn_ref[...] * _silu(z_ref[0].astype(_F32))
    hy = hx_ref[0] * (hy_ref[0] + hz_ref[0] * hd_ref[...])
    mix = jnp.concatenate([gdn, mla_ref[0], hy], axis=-1).astype(_BF16)
    y = jnp.dot(mix, w_ref[...], preferred_element_type=_F32)
    o_ref[0] = x_ref[0] + ga_ref[0] * y


def _out_proj(x, o_f, o_b, z, gn_row, mla, hx0, hz, hy, hd_row, ga, w_out, toff):
    b, l, d = x.shape
    hones = _gdn_prep_consts()[2]
    row = lambda n: pl.BlockSpec((1, TOK, n), lambda i, j: (i, j, 0))
    rowc = lambda n: pl.BlockSpec((1, TOK, n), lambda i, j: (i, j + toff, 0))
    whole = lambda a: pl.BlockSpec(a.shape, lambda i, j: (0,) * a.ndim)
    return pl.pallas_call(
        _out_proj_kernel,
        grid=(b, l // TOK),
        in_specs=[row(d), row(GDN_WIDTH), row(GDN_WIDTH), rowc(GDN_WIDTH), whole(gn_row), whole(hones),
                  row(MLA_WIDTH), rowc(HY_WIDTH), rowc(HY_WIDTH), row(HY_WIDTH), whole(hd_row),
                  pl.BlockSpec((1, 1, d), lambda i, j: (i, 0, 0)),
                  pl.BlockSpec((d, d), lambda i, j: (0, 0))],
        out_specs=row(d),
        out_shape=jax.ShapeDtypeStruct((b, l, d), _F32),
        compiler_params=_cparams(2),
        name="out_proj",
    )(x, o_f, o_b, z, gn_row, hones, mla, hx0, hz, hy, hd_row, ga, w_out)


def _ffn_kernel(xp_ref, x_ref, xn_ref, g_ref, sf_ref, cf_ref, gf_ref, wup_ref, cw_ref, cb_ref, wdn_ref,
                o_ref, h_ref, up_ref, uv_ref, act_ref, *, tl):
    i = pl.program_id(1)
    nt = pl.num_programs(1)
    g, sf, cf = g_ref[...], sf_ref[0], cf_ref[0]
    pv = (i > 0).astype(_F32)
    nv = (i < nt - 1).astype(_F32)
    h_ref[0:HALO] = (_norm_mod(xp_ref[0], g, sf, cf) * pv).astype(_BF16)
    h_ref[HALO:HALO + tl] = _norm_mod(x_ref[0], g, sf, cf).astype(_BF16)
    h_ref[HALO + tl:] = (_norm_mod(xn_ref[0], g, sf, cf) * nv).astype(_BF16)

    def up_proj(c, slot):
        lo = pl.multiple_of(c * FFN_CHUNK, FFN_CHUNK)
        up_ref[slot] = jnp.dot(h_ref[...], wup_ref[:, pl.ds(lo, FFN_CHUNK)], preferred_element_type=_F32)
        uv_ref[slot] = jnp.dot(h_ref[HALO:HALO + tl], wup_ref[:, pl.ds(D_FF + lo, FFN_CHUNK)],
                               preferred_element_type=_F32)

    def gate(c, slot):
        lo = pl.multiple_of(c * FFN_CHUNK, FFN_CHUNK)
        cw = cw_ref[:, pl.ds(lo, FFN_CHUNK)]
        cb = cb_ref[:, pl.ds(lo, FFN_CHUNK)]
        gt = (up_ref[slot, HALO - 1:HALO - 1 + tl] * cw[0:1] + up_ref[slot, HALO:HALO + tl] * cw[1:2]
              + up_ref[slot, HALO + 1:HALO + 1 + tl] * cw[2:3] + cb)
        act_ref[:, pl.ds(lo, FFN_CHUNK)] = (_silu(gt) * uv_ref[slot]).astype(_BF16)

    n_chunks = D_FF // FFN_CHUNK
    up_proj(0, 0)

    def pair(it, carry):
        c = 2 * it
        up_proj(c + 1, 1)
        gate(c, 0)
        up_proj(c + 2, 0)
        gate(c + 1, 1)
        return carry

    lax.fori_loop(0, (n_chunks - 1) // 2, pair, 0)
    assert n_chunks % 2 == 1
    gate(n_chunks - 1, 0)
    y = jnp.dot(act_ref[...], wdn_ref[...], preferred_element_type=_F32)
    o_ref[0] = x_ref[0] + gf_ref[0] * y


def _ffn(x, g, sf, cf, gf, w_up, conv_w, conv_b, w_down, tl):
    b, l, d = x.shape
    nh = tl // HALO
    last = l // HALO - 1
    vec = pl.BlockSpec((1, 1, d), lambda i, j: (i, 0, 0))
    whole = lambda a: pl.BlockSpec(a.shape, lambda i, j: (0,) * a.ndim)
    once = lambda a: pl.BlockSpec(a.shape, lambda i, j: (0,) * a.ndim, pipeline_mode=pl.Buffered(1))
    return pl.pallas_call(
        functools.partial(_ffn_kernel, tl=tl),
        grid=(b, l // tl),
        in_specs=[
            pl.BlockSpec((1, HALO, d), lambda i, j: (i, jnp.maximum(j * nh - 1, 0), 0)),
            pl.BlockSpec((1, tl, d), lambda i, j: (i, j, 0)),
            pl.BlockSpec((1, HALO, d), lambda i, j: (i, jnp.minimum((j + 1) * nh, last), 0)),
            pl.BlockSpec((1, d), lambda i, j: (0, 0)),
            vec, vec, vec,
            once(w_up), whole(conv_w), whole(conv_b), once(w_down),
        ],
        out_specs=pl.BlockSpec((1, tl, d), lambda i, j: (i, j, 0)),
        out_shape=jax.ShapeDtypeStruct((b, l, d), _F32),
        scratch_shapes=[
            pltpu.VMEM((tl + 2 * HALO, d), _BF16),
            pltpu.VMEM((2, tl + 2 * HALO, FFN_CHUNK), _F32),
            pltpu.VMEM((2, tl, FFN_CHUNK), _F32),
            pltpu.VMEM((tl, D_FF), _BF16),
        ],
        compiler_params=_cparams(2),
        name="conv_ffn",
    )(x, x, x, g, sf, cf, gf, w_up, conv_w, conv_b, w_down)


def _pad_row(v, n=LANES):
    v = v.reshape(1, -1)
    return jnp.pad(v, ((0, 0), (0, n - v.shape[1])))


def kernel(x, c, ctx, c_ctx, ada_w, ada_b, mix_norm_g, w_in, gdn_conv_w, gdn_a_log, gdn_dt_bias, gdn_norm_g, mla_q_norm_g, mla_w_uq, mla_kv_norm_g, mla_w_ukv, mla_q_head_g, mla_k_head_g, hy_conv_w, hy_conv_b, hy_w1, hy_b1, hy_w2, hy_b2, hy_w3, hy_b3, hy_d, w_out, ffn_norm_g, ffn_w_up, ffn_conv_w, ffn_conv_b, ffn_w_down):
    bsz, seq, d = x.shape
    n_ctx = ctx.shape[1]
    assert n_ctx == TOK and seq % TOK == 0 and bsz < ADA_ROWS
    cond = jnp.concatenate([c, c_ctx[None, :], jnp.zeros((ADA_ROWS - bsz - 1, d), c.dtype)], axis=0)
    for i in range(DEPTH):
        last = i == DEPTH - 1
        mod = _ada_mod(cond, ada_w[i], ada_b[i])
        mod_lat = mod[:bsz, None, :]
        mod_ctx = mod[bsz][None, None, :]
        sa_l, ca_l, ga_l, sf_l, cf_l, gf_l = jnp.split(mod_lat, 6, axis=-1)
        sa_c, ca_c, ga_c, sf_c, cf_c, gf_c = (jnp.broadcast_to(t, (bsz, 1, d)) for t in jnp.split(mod_ctx, 6, axis=-1))

        w_in_p = _pad_w_in(w_in[i])
        g_mix = mix_norm_g[i][None, :]
        qkv, z, ab, mla_in, hy_in = _in_proj(x, ctx, g_mix, sa_l, ca_l, sa_c[:1], ca_c[:1], w_in_p)

        q, k, v, gf, gb, bf, bb = _gdn_prep(qkv, ab, gdn_conv_w[i], _pad_row(gdn_a_log[i]), _pad_row(gdn_dt_bias[i]))
        oc_f, oc_b, ol_f, ol_b = _gdn_scan(q, k, v, gf, gb, bf, bb)

        mla_w = _mla_weights(mla_w_uq[i], mla_w_ukv[i], mla_q_head_g[i], mla_k_head_g[i])
        q_ctx, q_lat, k_all, v_all = _mla_prep(mla_in, mla_q_norm_g[i], mla_kv_norm_g[i], mla_w, seq)
        mla_l = _attention(q_lat, k_all, v_all, n_ctx + seq, ATTN_TQ)
        hy_x0, hy_z, hy_zc, hy_zl = _hy_prep(hy_in, hy_conv_w[i], hy_conv_b[i])
        hy_mlp = (hy_w1[i], hy_b1[i], hy_w2[i], hy_b2[i], hy_w3[i], hy_b3[i])
        hy_l = _hy_conv(_hy_filter(seq, *hy_mlp), hy_zl)
        hd_row = hy_d[i][None, :]

        w_out_b = w_out[i].astype(_BF16)
        w_up_b = ffn_w_up[i].astype(_BF16)
        w_dn_b = ffn_w_down[i].astype(_BF16)
        g_ffn = ffn_norm_g[i][None, :]
        cb = ffn_conv_b[i][None, :]
        gn_row = jnp.tile(gdn_norm_g[i], GDN_HEADS)[None, :]

        x = _out_proj(x, ol_f, ol_b, z, gn_row, mla_l, hy_x0, hy_z, hy_l, hd_row, ga_l, w_out_b, 1)
        x = _ffn(x, g_ffn, sf_l, cf_l, gf_l, w_up_b, ffn_conv_w[i], cb, w_dn_b, FFN_TL)

        if not last:
            mla_c = _attention(q_ctx, k_all, v_all, n_ctx, TOK)
            hy_c = _hy_conv(_hy_filter(n_ctx, *hy_mlp), hy_zc)
            ctx = _out_proj(ctx, oc_f, oc_b, z, gn_row, mla_c, hy_x0, hy_z, hy_c, hd_row, ga_c, w_out_b, 0)
            ctx = _ffn(ctx, g_ffn, sf_c, cf_c, gf_c, w_up_b, ffn_conv_w[i], cb, w_dn_b, TOK)
    return x
```

```python
import functools
import math

import jax
import jax.numpy as jnp
import numpy as np
from jax import lax
from jax.experimental import pallas as pl
from jax.experimental.pallas import tpu as pltpu

D_MODEL = 1024
DEPTH = 2
GRID_W = 64
EPS = 1e-6

GDN_HEADS = 6
GDN_DK = 64
GDN_DV = 64
GDN_CHUNK = 64

MLA_HEADS = 6
MLA_Q_RANK = 256
MLA_KV_RANK = 128
MLA_NOPE = 64
MLA_ROPE = 32
MLA_V = 64
MLA_QK = MLA_NOPE + MLA_ROPE
ROPE_BASE = 10000.0
ROPE_AXIS = MLA_ROPE // 2

HY_CH = 256
HY_BANDS = 16
HY_EMB = 1 + 2 * HY_BANDS
HY_HIDDEN = 64
HY_TARGET = 1e-2
HY_FAST_DECAY_PCT = 0.3
HY_SLOW_DECAY_PCT = 1.5
HY_MAX_DECAY = math.log(HY_TARGET) / HY_FAST_DECAY_PCT
HY_MIN_DECAY = math.log(HY_TARGET) / HY_SLOW_DECAY_PCT

D_FF = 2816

GDN_WIDTH = GDN_HEADS * GDN_DV
MLA_WIDTH = MLA_HEADS * MLA_V
HY_WIDTH = HY_CH
GDN_QKV = GDN_HEADS * (2 * GDN_DK + GDN_DV)
GDN_IN = GDN_QKV + GDN_WIDTH + 4 * GDN_HEADS
MLA_IN = MLA_Q_RANK + MLA_KV_RANK + MLA_ROPE
HY_IN = 3 * HY_CH

LANES = 128
SUBLANES_BF16 = 16
VMEM_LIMIT_BYTES = 56 * 1024 * 1024

TOK = 256
GDN_PAIRS = GDN_HEADS // 2
GDN_BB = 2
N_GATE = 4 * GDN_HEADS

AB_PAD = LANES
MLA_PAD = 512
IN_GROUPS = (GDN_QKV, GDN_WIDTH, AB_PAD, MLA_PAD, HY_IN)
IN_TOTAL = sum(IN_GROUPS)
IN_DTYPES = (jnp.bfloat16, jnp.bfloat16, jnp.float32, jnp.bfloat16, jnp.bfloat16)

FFN_CHUNK = 256
FFN_TL = 1024
HALO = SUBLANES_BF16

_BF16 = jnp.bfloat16
_F32 = jnp.float32
_HI = lax.Precision.HIGHEST
_NT = (((1,), (1,)), ((), ()))
_TN = (((0,), (0,)), ((), ()))


def _cparams(n_axes, sem=None):
    return pltpu.CompilerParams(
        dimension_semantics=sem or ("parallel",) * n_axes, vmem_limit_bytes=VMEM_LIMIT_BYTES)


def _norm_mod(x, g, shift, scale):
    ms = jnp.mean(x * x, axis=-1, keepdims=True)
    y = x * lax.rsqrt(ms + EPS) * g
    return y * (1.0 + scale) + shift


def _silu(x):
    return x * jax.nn.sigmoid(x)


ADA_ROWS = 16
ADA_TN = 1024


def _ada_kernel(c_ref, w_ref, b_ref, o_ref):
    o_ref[...] = jnp.dot(_silu(c_ref[...]), w_ref[...], precision=_HI, preferred_element_type=_F32) + b_ref[...]


def _ada_mod(cond, w, b):
    d, n = w.shape
    return pl.pallas_call(
        _ada_kernel,
        grid=(n // ADA_TN,),
        in_specs=[pl.BlockSpec((ADA_ROWS, d), lambda j: (0, 0)),
                  pl.BlockSpec((d, ADA_TN), lambda j: (0, j)),
                  pl.BlockSpec((1, ADA_TN), lambda j: (0, j))],
        out_specs=pl.BlockSpec((ADA_ROWS, ADA_TN), lambda j: (0, j)),
        out_shape=jax.ShapeDtypeStruct((ADA_ROWS, n), _F32),
        compiler_params=_cparams(1),
        name="ada_mod",
    )(cond, w, b[None, :])


def _in_proj_kernel(x_ref, c_ref, g_ref, sl_ref, cl_ref, sc_ref, cc_ref, w_ref, *out_refs):
    is_ctx = pl.program_id(1) == 0
    x = jnp.where(is_ctx, c_ref[0], x_ref[0])
    shift = jnp.where(is_ctx, sc_ref[0], sl_ref[0])
    scale = jnp.where(is_ctx, cc_ref[0], cl_ref[0])
    h = _norm_mod(x, g_ref[...], shift, scale)
    p = jnp.dot(h.astype(_BF16), w_ref[...], preferred_element_type=_F32)
    off = 0
    for o_ref, n in zip(out_refs, IN_GROUPS):
        o_ref[0] = p[:, off:off + n].astype(o_ref.dtype)
        off += n


def _in_proj(x, ctx, g, shift_l, scale_l, shift_c, scale_c, w_pad):
    b, l, d = x.shape
    nt = 1 + l // TOK
    vec_l = pl.BlockSpec((1, 1, d), lambda i, j: (i, 0, 0))
    vec_c = pl.BlockSpec((1, 1, d), lambda i, j: (0, 0, 0))
    return pl.pallas_call(
        _in_proj_kernel,
        grid=(b, nt),
        in_specs=[
            pl.BlockSpec((1, TOK, d), lambda i, j: (i, jnp.maximum(j - 1, 0), 0)),
            pl.BlockSpec((1, TOK, d), lambda i, j: (i, 0, 0)),
            pl.BlockSpec((1, d), lambda i, j: (0, 0)),
            vec_l, vec_l, vec_c, vec_c,
            pl.BlockSpec((d, IN_TOTAL), lambda i, j: (0, 0)),
        ],
        out_specs=[pl.BlockSpec((1, TOK, n), lambda i, j: (i, j, 0)) for n in IN_GROUPS],
        out_shape=[jax.ShapeDtypeStruct((b, nt * TOK, n), dt) for n, dt in zip(IN_GROUPS, IN_DTYPES)],
        compiler_params=_cparams(2),
        name="in_proj",
    )(x, ctx, g, shift_l, scale_l, shift_c, scale_c, w_pad)


def _pad_w_in(w_in):
    s1 = GDN_QKV + GDN_WIDTH
    s2 = GDN_IN
    s3 = GDN_IN + MLA_IN
    d = w_in.shape[0]
    z = lambda n: jnp.zeros((d, n), w_in.dtype)
    parts = [w_in[:, :s1], w_in[:, s1:s2], z(AB_PAD - N_GATE),
             w_in[:, s2:s3], z(MLA_PAD - MLA_IN), w_in[:, s3:]]
    return jnp.concatenate(parts, axis=1).astype(_BF16)


def _gdn_consts():
    r = np.arange(TOK)
    same = (r[:, None] // GDN_CHUNK) == (r[None, :] // GDN_CHUNK)
    tril = (same & (r[None, :] <= r[:, None])).astype(np.float32)
    triu = (same & (r[None, :] >= r[:, None])).astype(np.float32)
    c = np.arange(GDN_WIDTH)
    head_ones = (c[:, None] // GDN_DK == c[None, :] // GDN_DK).astype(np.float32)
    expand = np.zeros((LANES, 4 * GDN_WIDTH), np.float32)
    for k in range(4):
        for h in range(GDN_HEADS):
            expand[k * GDN_HEADS + h, k * GDN_WIDTH + h * GDN_DK:k * GDN_WIDTH + (h + 1) * GDN_DK] = 1.0
    return tril, triu, head_ones, expand


def _gdn_prep_consts():
    tril, triu, head_ones, expand = _gdn_consts()
    b = lambda a: jnp.asarray(a, _BF16)
    return b(tril), b(triu), b(np.concatenate([head_ones] * 2, axis=0)), b(np.concatenate([expand] * 3, axis=0))


def _split_bf16(x, n):
    terms = []
    for _ in range(n):
        t = x.astype(_BF16)
        terms.append(t)
        x = x - t.astype(_F32)
    return terms


def _sum_terms(y, n):
    w = y.shape[1] // n
    out = y[:, :w]
    for t in range(1, n):
        out = out + y[:, t * w:(t + 1) * w]
    return out


def _gdn_prep_kernel(xp_ref, x_ref, xn_ref, ab_ref, cw_ref, alog_ref, dt_ref, tril_ref, triu_ref, hones_ref,
                     exp_ref, q_ref, k_ref, v_ref, gf_ref, gb_ref, bf_ref, bb_ref, xe_ref):
    j = pl.program_id(1)
    nt = pl.num_programs(1)
    pv = (j >= 2).astype(_F32)
    nv = jnp.logical_and(j >= 1, j < nt - 1).astype(_F32)
    xe_ref[0:HALO] = xp_ref[0].astype(_F32) * pv
    xe_ref[HALO:HALO + TOK] = x_ref[0].astype(_F32)
    xe_ref[HALO + TOK:] = xn_ref[0].astype(_F32) * nv
    cw = cw_ref[...]
    y = (xe_ref[HALO - 1:HALO - 1 + TOK] * cw[0:1] + xe_ref[HALO:HALO + TOK] * cw[1:2]
         + xe_ref[HALO + 1:HALO + 1 + TOK] * cw[2:3])
    y = _silu(y)
    hk = GDN_HEADS * GDN_DK
    q, k, v = y[:, :hk], y[:, hk:2 * hk], y[:, 2 * hk:]
    sq = jnp.concatenate([q * q, k * k], axis=0)
    ss = jnp.dot(jnp.concatenate(_split_bf16(sq, 2), axis=1), hones_ref[...], preferred_element_type=_F32)
    q_ref[0] = q * lax.rsqrt(ss[:TOK] + EPS) * (GDN_DK ** -0.5)
    k_ref[0] = k * lax.rsqrt(ss[TOK:] + EPS)
    v_ref[0] = v

    ab = ab_ref[0]
    lane = lax.broadcasted_iota(jnp.int32, ab.shape, 1)
    a_in = ab + dt_ref[...]
    softplus = jnp.maximum(a_in, 0.0) + jnp.log(1.0 + jnp.exp(-jnp.abs(a_in)))
    g = jnp.where(lane < 2 * GDN_HEADS, -jnp.exp(alog_ref[...]) * softplus, 0.0)
    g3 = jnp.concatenate(_split_bf16(g, 3), axis=1)
    gc_f = _sum_terms(jnp.dot(tril_ref[...], g3, preferred_element_type=_F32), 3)
    gc_b = _sum_terms(jnp.dot(triu_ref[...], g3, preferred_element_type=_F32), 3)
    cols = jnp.where(lane < GDN_HEADS, gc_f, jnp.where(lane < 2 * GDN_HEADS, gc_b, jax.nn.sigmoid(ab)))
    wide = jnp.dot(jnp.concatenate(_split_bf16(cols, 3), axis=1), exp_ref[...],
                   preferred_element_type=_F32)
    gf_ref[0] = wide[:, 0:GDN_WIDTH]
    gb_ref[0] = wide[:, GDN_WIDTH:2 * GDN_WIDTH]
    bf_ref[0] = wide[:, 2 * GDN_WIDTH:3 * GDN_WIDTH]
    bb_ref[0] = wide[:, 3 * GDN_WIDTH:]


def _gdn_prep(qkv, ab, conv_w, a_log_row, dt_row):
    b, lt, _ = qkv.shape
    nt = lt // TOK
    nh = TOK // HALO
    last = lt // HALO - 1
    consts = list(_gdn_prep_consts())
    whole = lambda a: pl.BlockSpec(a.shape, lambda i, j: (0,) * a.ndim)
    row = lambda n: pl.BlockSpec((1, TOK, n), lambda i, j: (i, j, 0))
    return pl.pallas_call(
        _gdn_prep_kernel,
        grid=(b, nt),
        in_specs=[
            pl.BlockSpec((1, HALO, GDN_QKV), lambda i, j: (i, jnp.maximum(j * nh - 1, 0), 0)),
            row(GDN_QKV),
            pl.BlockSpec((1, HALO, GDN_QKV), lambda i, j: (i, jnp.minimum((j + 1) * nh, last), 0)),
            row(AB_PAD), whole(conv_w), whole(a_log_row), whole(dt_row),
        ] + [whole(a) for a in consts],
        out_specs=[row(GDN_WIDTH)] * 7,
        out_shape=[jax.ShapeDtypeStruct((b, lt, GDN_WIDTH), _F32)] * 7,
        scratch_shapes=[pltpu.VMEM((TOK + 2 * HALO, GDN_QKV), _F32)],
        compiler_params=_cparams(2),
        name="gdn_prep",
    )(qkv, qkv, qkv, ab, conv_w, a_log_row, dt_row, *consts)


def _block_diag(z, left):
    return jnp.concatenate([jnp.where(left, z, 0.0), jnp.where(left, 0.0, z)], axis=0).astype(_BF16)


def _mm(a, b):
    return jnp.dot(a.astype(_BF16), b, preferred_element_type=_F32)


def _gdn_intra(probs, masks):
    left, eye2, ones64, tri, same_blk = masks
    c = GDN_CHUNK
    qs, ks, vs, gxs, bxs, bws = zip(*probs)
    qk_kk = [lax.dot_general(jnp.concatenate([q, k], axis=0).astype(_BF16), _block_diag(k, left), _NT,
                             preferred_element_type=_F32) for q, k in zip(qs, ks)]
    rs = []
    for gx in gxs:
        d0 = jnp.where(eye2, gx, 0.0)
        t_hi = d0.astype(_BF16)
        r1 = d0 - t_hi.astype(_F32)
        t_mid = r1.astype(_BF16)
        t_lo = (r1 - t_mid.astype(_F32)).astype(_BF16)
        r3 = jnp.dot(ones64, jnp.concatenate([t_hi, t_mid, t_lo], axis=1), preferred_element_type=_F32)
        rs.append(r3[:, :LANES] + r3[:, LANES:2 * LANES] + r3[:, 2 * LANES:])
    a_s, qkm, egs = [], [], []
    for x, gx, bx, r, bw in zip(qk_kk, gxs, bxs, rs, bws):
        incl, strict = tri[bw]
        dec = jnp.where(incl, jnp.exp(jnp.where(incl, gx - r, 0.0)), 0.0)
        a_s.append(bx * x[c:] * jnp.where(strict, dec, 0.0))
        qkm.append(x[:c] * dec)
        egs.append(jnp.exp(gx))
    eye_f = jnp.where(eye2, 1.0, 0.0)
    base = same_blk[8]
    d1 = [jnp.where(base, a, 0.0) for a in a_s]
    ps = [eye_f - d for d in d1]
    d2 = [_mm(d, _block_diag(d, left)) for d in d1]
    d2_bd = [_block_diag(d, left) for d in d2]
    ps = [p + _mm(p, bd) for p, bd in zip(ps, d2_bd)]
    d4 = [_mm(d, bd) for d, bd in zip(d2, d2_bd)]
    ps = [p + _mm(p, _block_diag(d, left)) for p, d in zip(ps, d4)]
    for blk in (8, 16, 32):
        off = jnp.logical_and(same_blk[2 * blk], jnp.logical_not(same_blk[blk]))
        t1 = [_mm(p, _block_diag(jnp.where(off, a, 0.0), left)) for p, a in zip(ps, a_s)]
        ps = [p - _mm(t, _block_diag(p, left)) for p, t in zip(ps, t1)]
    out = []
    for p, q, k, v, gx, bx, eg, qk, bw in zip(ps, qs, ks, vs, gxs, bxs, egs, qkm, bws):
        tot = gx[0:1] if bw else gx[c - 1:c]
        lhs = jnp.concatenate([k * eg, q * eg], axis=0).astype(_BF16)
        out.append((p, lhs, bx, bx * v, qk, k * jnp.exp(tot - gx), tot))
    return out


def _gdn_state_step(chains, left, diag_blocks):
    c = GDN_CHUNK
    ys = [jnp.dot(x[1], s.astype(_BF16), preferred_element_type=_F32) for s, x in chains]
    resid = [x[3] - x[2] * y[:c] for (s, x), y in zip(chains, ys)]
    deltas = [_mm(x[0], _block_diag(r, left)) for (s, x), r in zip(chains, resid)]
    os_ = [y[c:] + _mm(x[4], _block_diag(d, left)) for (s, x), y, d in zip(chains, ys, deltas)]
    upds = [lax.dot_general(x[5].astype(_BF16), d.astype(_BF16), _TN, preferred_element_type=_F32)
            for (s, x), d in zip(chains, deltas)]
    new_s = [s * jnp.exp(x[6]) + jnp.where(diag_blocks, u, 0.0) for (s, x), u in zip(chains, upds)]
    return list(zip(os_, new_s))


def _gdn_scan_kernel(cq, ck, cv, cgf, cbf, cgb, cbb, fq, fk, fv, fg, fb, rq, rk, rv, rg, rb,
                        ocf_ref, ocb_ref, of_ref, ob_ref, sf_ref, sb_ref):
    step = pl.program_id(1)
    is_ctx = step == 0

    @pl.when(is_ctx)
    def _():
        sf_ref[...] = jnp.zeros_like(sf_ref)
        sb_ref[...] = jnp.zeros_like(sb_ref)

    c = GDN_CHUNK
    nc = TOK // c
    li = lax.broadcasted_iota(jnp.int32, (c, LANES), 1)
    ri = lax.broadcasted_iota(jnp.int32, (c, LANES), 0)
    lj = li & (c - 1)
    left = li < c
    eye2 = lj == ri
    ones64 = jnp.ones((c, c), _BF16)
    r2 = lax.broadcasted_iota(jnp.int32, (LANES, LANES), 0)
    c2 = lax.broadcasted_iota(jnp.int32, (LANES, LANES), 1)
    diag_blocks = (r2 // c) == (c2 // c)
    tri = {False: (ri >= lj, ri > lj), True: (ri <= lj, ri < lj)}
    same_blk = {b: (ri // b) == (lj // b) for b in (8, 16, 32, 64)}
    masks = (left, eye2, ones64, tri, same_blk)

    nbb = fq.shape[0]
    ctx_refs = {False: (cq, ck, cv, cgf, cbf), True: (cq, ck, cv, cgb, cbb)}
    lat_refs = {False: (fq, fk, fv, fg, fb), True: (rq, rk, rv, rg, rb)}
    vals = {(bw, e): [jnp.where(is_ctx, cr[e], r[e]) for cr, r in zip(ctx_refs[bw], lat_refs[bw])]
            for bw in (False, True) for e in range(nbb)}
    chain_keys = [(bw, e, p) for bw in (False, True) for e in range(nbb) for p in range(GDN_PAIRS)]
    prob_keys = [(bw, e, p, n) for bw, e, p in chain_keys for n in range(nc)]
    probs = [tuple(a[n * c:(n + 1) * c, p * LANES:(p + 1) * LANES] for a in vals[(bw, e)]) + (bw,)
             for bw, e, p, n in prob_keys]
    intra = dict(zip(prob_keys, _gdn_intra(probs, masks)))

    state_ref = lambda bw: sb_ref if bw else sf_ref
    states = {(bw, e, p): state_ref(bw)[e, p] for bw, e, p in chain_keys}
    outs = {}
    for t in range(nc):
        ns = {kk: (nc - 1 - t if kk[0] else t) for kk in chain_keys}
        res = _gdn_state_step([(states[kk], intra[kk + (ns[kk],)]) for kk in chain_keys], left, diag_blocks)
        for kk, (o, st) in zip(chain_keys, res):
            outs[kk + (ns[kk],)] = o
            states[kk] = st
    for bw, e, p in chain_keys:
        state_ref(bw)[e, p] = states[(bw, e, p)]

    for bw, o_ref, oc_ref in ((False, of_ref, ocf_ref), (True, ob_ref, ocb_ref)):
        for e in range(nbb):
            o_all = jnp.concatenate(
                [jnp.concatenate([outs[(bw, e, p, n)] for p in range(GDN_PAIRS)], axis=1) for n in range(nc)], axis=0)

            @pl.when(is_ctx)
            def _(o_all=o_all, oc_ref=oc_ref, e=e):
                oc_ref[e] = o_all

            o_ref[e] = o_all


def _gdn_scan(q, k, v, gf, gb, bf, bb):
    b, lt, w = q.shape
    nl = lt // TOK - 1
    nbb = GDN_BB
    ctx = pl.BlockSpec((nbb, TOK, w), lambda i, s: (i, 0, 0))
    fwd = pl.BlockSpec((nbb, TOK, w), lambda i, s: (i, jnp.maximum(s, 1), 0))
    bwd = pl.BlockSpec((nbb, TOK, w), lambda i, s: (i, nl + 1 - jnp.maximum(s, 1), 0))
    fwd_o = pl.BlockSpec((nbb, TOK, w), lambda i, s: (i, jnp.maximum(s, 1) - 1, 0))
    bwd_o = pl.BlockSpec((nbb, TOK, w), lambda i, s: (i, nl - jnp.maximum(s, 1), 0))
    return pl.pallas_call(
        _gdn_scan_kernel,
        grid=(b // nbb, nl + 1),
        in_specs=[ctx] * 7 + [fwd] * 5 + [bwd] * 5,
        out_specs=[ctx, ctx, fwd_o, bwd_o],
        out_shape=[jax.ShapeDtypeStruct((b, TOK, w), _F32)] * 2 + [jax.ShapeDtypeStruct((b, nl * TOK, w), _F32)] * 2,
        scratch_shapes=[pltpu.VMEM((nbb, GDN_PAIRS, LANES, LANES), _F32)] * 2,
        compiler_params=_cparams(2, ("parallel", "arbitrary")),
        name="gdn_scan",
    )(q, k, v, gf, bf, gb, bb, q, k, v, gf, bf, q, k, v, gb, bb)


MLA_HEAD_PAD = LANES
MLA_WIDE = MLA_HEADS * MLA_HEAD_PAD
ATTN_TQ = 512
ATTN_TK = 512
ATTN_HEADS = 2


def _rope_tables(n_ctx, seq):
    rows = seq // GRID_W
    row = np.repeat(np.arange(rows, dtype=np.float64), GRID_W)
    col = np.tile(np.arange(GRID_W, dtype=np.float64), rows)
    inv = ROPE_BASE ** (-np.arange(0, ROPE_AXIS, 2, dtype=np.float64) / ROPE_AXIS)
    ang = np.concatenate([row[:, None] * inv, col[:, None] * inv], axis=-1)
    cos, sin = np.cos(ang), np.sin(ang)
    half = MLA_ROPE // 2
    c = np.ones((n_ctx + seq, MLA_HEAD_PAD))
    s = np.zeros((n_ctx + seq, MLA_HEAD_PAD))
    c[n_ctx:, MLA_NOPE:MLA_NOPE + half] = cos
    c[n_ctx:, MLA_NOPE + half:MLA_QK] = cos
    s[n_ctx:, MLA_NOPE:MLA_NOPE + half] = -sin
    s[n_ctx:, MLA_NOPE + half:MLA_QK] = sin
    return c.astype(np.float32), s.astype(np.float32)


def _mla_prep_kernel(p_ref, qn_ref, kvn_ref, wq_ref, wk_ref, wv_ref, qg_ref, kg_ref, cos_ref, sin_ref,
                     qc_ref, ql_ref, k_ref, v_ref):
    p = p_ref[0].astype(_F32)
    cq = p[:, :MLA_Q_RANK]
    ckv = p[:, MLA_Q_RANK:MLA_Q_RANK + MLA_KV_RANK]
    kr = p[:, MLA_Q_RANK + MLA_KV_RANK:]
    cq = (cq * lax.rsqrt(jnp.mean(cq * cq, axis=-1, keepdims=True) + EPS) * qn_ref[...]).astype(_BF16)
    ckv = (ckv * lax.rsqrt(jnp.mean(ckv * ckv, axis=-1, keepdims=True) + EPS) * kvn_ref[...]).astype(_BF16)
    q = jnp.dot(cq, wq_ref[...], preferred_element_type=_F32)
    k = jnp.dot(jnp.concatenate([ckv, kr.astype(_BF16)], axis=1), wk_ref[...], preferred_element_type=_F32)
    lane = lax.broadcasted_iota(jnp.int32, (TOK, MLA_WIDE), 1) & (MLA_HEAD_PAD - 1)
    v = jnp.dot(ckv, wv_ref[...], preferred_element_type=_F32) + jnp.where(lane == MLA_V, 1.0, 0.0)
    cos = jnp.concatenate([cos_ref[...]] * MLA_HEADS, axis=1)
    sin = jnp.concatenate([sin_ref[...]] * MLA_HEADS, axis=1)
    half = MLA_ROPE // 2
    first = jnp.logical_and(lane >= MLA_NOPE, lane < MLA_NOPE + half)
    second = jnp.logical_and(lane >= MLA_NOPE + half, lane < MLA_QK)

    def head_norm_rope(x, g):
        parts = []
        for h in range(MLA_HEADS):
            xh = x[:, h * MLA_HEAD_PAD:(h + 1) * MLA_HEAD_PAD]
            ms = jnp.sum(xh * xh, axis=-1, keepdims=True) * (1.0 / MLA_QK)
            parts.append(xh * lax.rsqrt(ms + EPS))
        xn = jnp.concatenate(parts, axis=1) * g
        up = pltpu.roll(xn, half, 1)
        down = pltpu.roll(xn, MLA_WIDE - half, 1)
        swapped = jnp.where(first, down, jnp.where(second, up, 0.0))
        return xn * cos + swapped * sin

    qf = head_norm_rope(q, qg_ref[...]) * (MLA_QK ** -0.5 * math.log2(math.e))
    kf = head_norm_rope(k, kg_ref[...])
    is_ctx = pl.program_id(1) == 0
    for h in range(MLA_HEADS):
        sl = slice(h * MLA_HEAD_PAD, (h + 1) * MLA_HEAD_PAD)
        k_ref[0, h] = kf[:, sl].astype(_BF16)
        v_ref[0, h] = v[:, sl].astype(_BF16)

    @pl.when(is_ctx)
    def _():
        for h in range(MLA_HEADS):
            qc_ref[0, h] = qf[:, h * MLA_HEAD_PAD:(h + 1) * MLA_HEAD_PAD].astype(_BF16)

    for h in range(MLA_HEADS):
        ql_ref[0, h] = qf[:, h * MLA_HEAD_PAD:(h + 1) * MLA_HEAD_PAD].astype(_BF16)


def _mla_weights(w_uq, w_ukv, q_head_g, k_head_g):
    pad = MLA_HEAD_PAD
    wq = jnp.pad(w_uq.reshape(MLA_Q_RANK, MLA_HEADS, MLA_QK), ((0, 0), (0, 0), (0, pad - MLA_QK)))
    wkv = w_ukv.reshape(MLA_KV_RANK, MLA_HEADS, MLA_NOPE + MLA_V)
    wk = jnp.pad(wkv[:, :, :MLA_NOPE], ((0, 0), (0, 0), (0, pad - MLA_NOPE)))
    wv = jnp.pad(wkv[:, :, MLA_NOPE:], ((0, 0), (0, 0), (0, pad - MLA_V)))
    sel = np.zeros((MLA_PAD - MLA_Q_RANK - MLA_KV_RANK, MLA_WIDE), np.float32)
    for h in range(MLA_HEADS):
        for r in range(MLA_ROPE):
            sel[r, h * pad + MLA_NOPE + r] = 1.0
    tile_g = lambda g: jnp.tile(jnp.pad(g, (0, pad - MLA_QK)), MLA_HEADS)[None, :]
    flat = lambda w: w.reshape(w.shape[0], MLA_WIDE).astype(_BF16)
    wk_sel = jnp.concatenate([flat(wk), jnp.asarray(sel, _BF16)], axis=0)
    return flat(wq), wk_sel, flat(wv), tile_g(q_head_g), tile_g(k_head_g)


def _mla_prep(mla_in, q_norm_g, kv_norm_g, weights, seq):
    b, lt, _ = mla_in.shape
    nt = lt // TOK
    wq, wk, wv, qg, kg = weights
    cos, sin = (jnp.asarray(t) for t in _rope_tables(lt - seq, seq))
    whole = lambda a: pl.BlockSpec(a.shape, lambda i, j: (0,) * a.ndim)
    tab = pl.BlockSpec((TOK, MLA_HEAD_PAD), lambda i, j: (j, 0))
    hd = lambda f: pl.BlockSpec((1, MLA_HEADS, TOK, MLA_HEAD_PAD), f)
    qn, kvn = q_norm_g[None, :], kv_norm_g[None, :]
    shp = lambda t: jax.ShapeDtypeStruct((b, MLA_HEADS, t, MLA_HEAD_PAD), _BF16)
    return pl.pallas_call(
        _mla_prep_kernel,
        grid=(b, nt),
        in_specs=[pl.BlockSpec((1, TOK, MLA_PAD), lambda i, j: (i, j, 0)), whole(qn), whole(kvn),
                  whole(wq), whole(wk), whole(wv), whole(qg), whole(kg), tab, tab],
        out_specs=[hd(lambda i, j: (i, 0, 0, 0)), hd(lambda i, j: (i, 0, jnp.maximum(j, 1) - 1, 0)),
                   hd(lambda i, j: (i, 0, j, 0)), hd(lambda i, j: (i, 0, j, 0))],
        out_shape=[shp(TOK), shp(lt - TOK), shp(lt), shp(lt)],
        compiler_params=_cparams(2, ("parallel", "arbitrary")),
        name="mla_prep",
    )(mla_in, qn, kvn, wq, wk, wv, qg, kg, cos, sin)


def _attn_kernel(q_ref, k_ref, v_ref, o_ref):
    n_keys = k_ref.shape[2]
    starts = list(range(0, n_keys, ATTN_TK))
    heads = range(q_ref.shape[1])

    def scores(h, lo):
        hi = min(lo + ATTN_TK, n_keys)
        return lax.dot_general(q_ref[0, h], k_ref[0, h, lo:hi, :], _NT, preferred_element_type=_F32)

    nxt = [scores(h, starts[0]) for h in heads]
    m = [None] * len(heads)
    acc = [None] * len(heads)
    for n, lo in enumerate(starts):
        cur = nxt
        if n + 1 < len(starts):
            nxt = [scores(h, starts[n + 1]) for h in heads]
        hi = min(lo + ATTN_TK, n_keys)
        for h in heads:
            s = cur[h]
            m_blk = jnp.max(s, axis=-1, keepdims=True)
            m_new = m_blk if n == 0 else jnp.maximum(m[h], m_blk)
            p = jnp.exp2(s - m_new).astype(_BF16)
            pv = jnp.dot(p, v_ref[0, h, lo:hi, :], preferred_element_type=_F32)
            acc[h] = pv if n == 0 else acc[h] * jnp.exp2(m[h] - m_new) + pv
            m[h] = m_new
    o_ref[0] = jnp.concatenate([a[:, :MLA_V] / a[:, MLA_V:MLA_V + 1] for a in acc], axis=1)


def _attention(q, k, v, n_keys, tq):
    b, h, t, w = q.shape
    return pl.pallas_call(
        _attn_kernel,
        grid=(b, h // ATTN_HEADS, t // tq),
        in_specs=[pl.BlockSpec((1, ATTN_HEADS, tq, w), lambda i, j, l: (i, j, l, 0)),
                  pl.BlockSpec((1, ATTN_HEADS, n_keys, w), lambda i, j, l: (i, j, 0, 0)),
                  pl.BlockSpec((1, ATTN_HEADS, n_keys, w), lambda i, j, l: (i, j, 0, 0))],
        out_specs=pl.BlockSpec((1, tq, ATTN_HEADS * MLA_V), lambda i, j, l: (i, l, j)),
        out_shape=jax.ShapeDtypeStruct((b, t, h * MLA_V), _F32),
        compiler_params=_cparams(3),
        name="mla_attention",
    )(q, k, v)


HY_BLK = 256
HY_LO = 128
HY_CB = 16


def _hy_prep_kernel(xp_ref, x_ref, xn_ref, cw_ref, cb_ref, x0_ref, z_ref, zc_ref, zl_ref, xe_ref):
    j = pl.program_id(1)
    nt = pl.num_programs(1)
    pv = (j >= 2).astype(_F32)
    nv = jnp.logical_and(j >= 1, j < nt - 1).astype(_F32)
    xe_ref[0:HALO] = xp_ref[0].astype(_F32) * pv
    xe_ref[HALO:HALO + TOK] = x_ref[0].astype(_F32)
    xe_ref[HALO + TOK:] = xn_ref[0].astype(_F32) * nv
    cw = cw_ref[...]
    u = (xe_ref[HALO - 1:HALO - 1 + TOK] * cw[0:1] + xe_ref[HALO:HALO + TOK] * cw[1:2]
         + xe_ref[HALO + 1:HALO + 1 + TOK] * cw[2:3] + cb_ref[...])
    x0_ref[0] = u[:, :HY_CH]
    z = u[:, 2 * HY_CH:] * u[:, HY_CH:2 * HY_CH]
    z_ref[0] = z

    @pl.when(j == 0)
    def _():
        zc_ref[0] = z.astype(_BF16)

    zl_ref[0] = z.astype(_BF16)


def _hy_prep(hy_in, conv_w, conv_b):
    b, lt, _ = hy_in.shape
    nh = TOK // HALO
    last = lt // HALO - 1
    whole = lambda a: pl.BlockSpec(a.shape, lambda i, j: (0,) * a.ndim)
    row = lambda n: pl.BlockSpec((1, TOK, n), lambda i, j: (i, j, 0))
    cb = conv_b[None, :]
    return pl.pallas_call(
        _hy_prep_kernel,
        grid=(b, lt // TOK),
        in_specs=[pl.BlockSpec((1, HALO, HY_IN), lambda i, j: (i, jnp.maximum(j * nh - 1, 0), 0)),
                  row(HY_IN),
                  pl.BlockSpec((1, HALO, HY_IN), lambda i, j: (i, jnp.minimum((j + 1) * nh, last), 0)),
                  whole(conv_w), whole(cb)],
        out_specs=[row(HY_CH), row(HY_CH),
                   pl.BlockSpec((1, TOK, HY_CH), lambda i, j: (i, 0, 0)),
                   pl.BlockSpec((1, TOK, HY_CH), lambda i, j: (i, jnp.maximum(j, 1) - 1, 0))],
        out_shape=[jax.ShapeDtypeStruct((b, lt, HY_CH), _F32)] * 2
        + [jax.ShapeDtypeStruct((b, TOK, HY_CH), _BF16), jax.ShapeDtypeStruct((b, lt - TOK, HY_CH), _BF16)],
        scratch_shapes=[pltpu.VMEM((TOK + 2 * HALO, HY_IN), _F32)],
        compiler_params=_cparams(2, ("parallel", "arbitrary")),
        name="hyena_prep",
    )(hy_in, hy_in, hy_in, conv_w, cb)


def _hy_filter_consts(l):
    def emb(t):
        t = t.astype(np.float64)
        t_norm = t / max(l - 1, 1)
        bands = np.linspace(1e-4, HY_BANDS - 1, HY_BANDS)
        ang = 2.0 * math.pi * t[:, None] * bands[None, :] / l
        z = np.concatenate([t_norm[:, None], np.cos(ang), np.sin(ang)], axis=-1)
        return np.pad(z, ((0, 0), (0, LANES - HY_EMB))), t_norm[:, None]
    r = np.arange(l)
    e_rev, tn_rev = emb(l - 1 - r)
    e_sh, tn_sh = emb(r + 1)
    deltas = np.abs(np.linspace(HY_MIN_DECAY, HY_MAX_DECAY, HY_CH))[None, :]
    f = lambda a: np.asarray(a, np.float32)
    return f(e_rev), f(e_sh), f(tn_rev), f(tn_sh), f(deltas)


def _hy_filter_kernel(er_ref, es_ref, tr_ref, ts_ref, dl_ref, w1_ref, b1_ref, w2_ref, b2_ref, w3_ref, b3_ref, o_ref):
    l = er_ref.shape[0]

    def mlp(e, col):
        h = jnp.sin(jnp.dot(e, w1_ref[...], precision=_HI, preferred_element_type=_F32) + b1_ref[...])
        h = jnp.sin(jnp.dot(h, w2_ref[...], precision=_HI, preferred_element_type=_F32) + b2_ref[...])
        return (jnp.dot(h, w3_ref[:, col * HY_CH:(col + 1) * HY_CH], precision=_HI, preferred_element_type=_F32)
                + b3_ref[:, col * HY_CH:(col + 1) * HY_CH])

    hf = mlp(er_ref[...], 0) * jnp.exp(-tr_ref[...] * dl_ref[...])
    hb = mlp(es_ref[...], 1) * jnp.exp(-ts_ref[...] * dl_ref[...])
    row = lax.broadcasted_iota(jnp.int32, hb.shape, 0)
    hb = jnp.where(row < l - 1, hb, 0.0)
    norm = jnp.sum(jnp.abs(hf), axis=0, keepdims=True) + jnp.sum(jnp.abs(hb), axis=0, keepdims=True)
    o_ref[...] = jnp.transpose(jnp.concatenate([hf, hb], axis=0) / norm)


def _hy_filter(l, w1, b1, w2, b2, w3, b3):
    consts = [jnp.asarray(a) for a in _hy_filter_consts(l)]
    w1p = jnp.pad(w1, ((0, LANES - HY_EMB), (0, 0)))
    args = consts + [w1p, b1[None, :], w2, b2[None, :], w3, b3[None, :]]
    return pl.pallas_call(
        _hy_filter_kernel,
        out_shape=jax.ShapeDtypeStruct((HY_CH, 2 * l), _F32),
        compiler_params=pltpu.CompilerParams(vmem_limit_bytes=VMEM_LIMIT_BYTES),
        name="hyena_filter",
    )(*args)


def _hy_conv_kernel(f_ref, z_ref, y_ref, g_ref, *, nblk, nb):
    cols = z_ref.shape[2]
    lane = lax.broadcasted_iota(jnp.int32, (HY_BLK, cols), 1)
    lane_p = lax.broadcasted_iota(jnp.int32, (HY_BLK // 2, cols), 1)
    gs = 4 if nblk >= 4 else 1

    def build(ch, slot):
        rows = SUBLANES_BF16
        base = pltpu.roll(jnp.broadcast_to(f_ref[ch], (rows, f_ref.shape[2])), 1, 1, stride=1, stride_axis=0)
        base = base.astype(_BF16)
        g_ref[slot, 0:rows, :] = base
        packed = pltpu.bitcast(base, jnp.int32)
        for a in range(1, HY_LO // rows):
            g_ref[slot, a * rows:(a + 1) * rows, :] = pltpu.bitcast(pltpu.roll(packed, rows * a, 1), _BF16)

    def convolve(ch, slot):
        z = z_ref[ch]
        zp = pltpu.bitcast(z, jnp.int32)
        real = nb * nblk

        def shift(x, lanes, blocks, sign, fill):
            if blocks == 0:
                return x
            s = nb * blocks
            if sign > 0:
                return jnp.where(lanes >= s, pltpu.roll(x, s, 1), fill)
            return jnp.where(lanes < real - s, pltpu.roll(x, cols - s, 1), fill)

        acc = {}
        for sign in (1, -1):
            for a in range(-(-nblk // gs)):
                ds = [gs * a + r for r in range(gs) if gs * a + r < nblk and not (sign < 0 and gs * a + r == 0)]
                if not ds:
                    continue
                wins = []
                for d in ds:
                    o = HY_BLK * (nblk - sign * d)
                    wins += [g_ref[slot, :, o:o + HY_BLK], g_ref[slot, :, o - HY_LO:o - HY_LO + HY_BLK]]
                zs = pltpu.bitcast(shift(zp, lane_p, gs * a, sign, 0), _BF16)
                part = jnp.dot(jnp.concatenate(wins, axis=0), zs, preferred_element_type=_F32)
                for n, d in enumerate(ds):
                    key = (sign, d - gs * a)
                    blk = part[n * HY_BLK:(n + 1) * HY_BLK]
                    acc[key] = blk if key not in acc else acc[key] + blk
        y = None
        for (sign, r), v in acc.items():
            v = shift(v, lane, r, sign, 0.0)
            y = v if y is None else y + v
        y_ref[ch] = y

    n_ch = f_ref.shape[0]
    build(0, 0)

    def pair(it, carry):
        ch = 2 * it
        build(ch + 1, 1)
        convolve(ch, 0)
        build(jnp.minimum(ch + 2, n_ch - 1), 0)
        convolve(ch + 1, 1)
        return carry

    lax.fori_loop(0, n_ch // 2, pair, 0)


def _hy_conv(fline, z):
    b, l, c = z.shape
    nblk = l // HY_BLK
    cols = max(nblk * b, LANES)
    zall = z.reshape(b, nblk, HY_BLK, c).transpose(3, 2, 1, 0).reshape(c, HY_BLK, nblk * b)
    zall = jnp.pad(zall.astype(_BF16), ((0, 0), (0, 0), (0, cols - nblk * b)))
    y = pl.pallas_call(
        functools.partial(_hy_conv_kernel, nblk=nblk, nb=b),
        grid=(c // HY_CB,),
        in_specs=[pl.BlockSpec((HY_CB, 1, 2 * l), lambda i: (i, 0, 0)),
                  pl.BlockSpec((HY_CB, HY_BLK, cols), lambda i: (i, 0, 0))],
        out_specs=pl.BlockSpec((HY_CB, HY_BLK, cols), lambda i: (i, 0, 0)),
        out_shape=jax.ShapeDtypeStruct((c, HY_BLK, cols), _F32),
        scratch_shapes=[pltpu.VMEM((2, HY_LO, 2 * l), _BF16)],
        compiler_params=_cparams(1),
        name="hyena_conv",
    )(fline.reshape(c, 1, 2 * l), zall)
    return y[:, :, :nblk * b].reshape(c, HY_BLK, nblk, b).transpose(3, 2, 1, 0).reshape(b, l, c)


def _out_proj_kernel(x_ref, of_ref, ob_ref, z_ref, gn_ref, hones_ref, mla_ref, hx_ref, hz_ref, hy_ref, hd_ref,
                     ga_ref, w_ref, o_ref):
    o = of_ref[0] + ob_ref[0]
    ms = jnp.dot(jnp.concatenate(_split_bf16(o * o, 2), axis=1), hones_ref[...],
                 preferred_element_type=_F32) * (1.0 / GDN_DV)
    gdn = o * lax.rsqrt(ms + EPS) * gn_ref[...] * _silu(z_ref[0].astype(_F32))
    hy = hx_ref[0] * (hy_ref[0] + hz_ref[0] * hd_ref[...])
    mix = jnp.concatenate([gdn, mla_ref[0], hy], axis=-1).astype(_BF16)
    y = jnp.dot(mix, w_ref[...], preferred_element_type=_F32)
    o_ref[0] = x_ref[0] + ga_ref[0] * y


def _out_proj(x, o_f, o_b, z, gn_row, mla, hx0, hz, hy, hd_row, ga, w_out, toff):
    b, l, d = x.shape
    hones = _gdn_prep_consts()[2]
    row = lambda n: pl.BlockSpec((1, TOK, n), lambda i, j: (i, j, 0))
    rowc = lambda n: pl.BlockSpec((1, TOK, n), lambda i, j: (i, j + toff, 0))
    whole = lambda a: pl.BlockSpec(a.shape, lambda i, j: (0,) * a.ndim)
    return pl.pallas_call(
        _out_proj_kernel,
        grid=(b, l // TOK),
        in_specs=[row(d), row(GDN_WIDTH), row(GDN_WIDTH), rowc(GDN_WIDTH), whole(gn_row), whole(hones),
                  row(MLA_WIDTH), rowc(HY_WIDTH), rowc(HY_WIDTH), row(HY_WIDTH), whole(hd_row),
                  pl.BlockSpec((1, 1, d), lambda i, j: (i, 0, 0)),
                  pl.BlockSpec((d, d), lambda i, j: (0, 0))],
        out_specs=row(d),
        out_shape=jax.ShapeDtypeStruct((b, l, d), _F32),
        compiler_params=_cparams(2),
        name="out_proj",
    )(x, o_f, o_b, z, gn_row, hones, mla, hx0, hz, hy, hd_row, ga, w_out)


def _ffn_kernel(xp_ref, x_ref, xn_ref, g_ref, sf_ref, cf_ref, gf_ref, wup_ref, cw_ref, cb_ref, wdn_ref,
                o_ref, h_ref, up_ref, uv_ref, act_ref, *, tl):
    i = pl.program_id(1)
    nt = pl.num_programs(1)
    g, sf, cf = g_ref[...], sf_ref[0], cf_ref[0]
    pv = (i > 0).astype(_F32)
    nv = (i < nt - 1).astype(_F32)
    h_ref[0:HALO] = (_norm_mod(xp_ref[0], g, sf, cf) * pv).astype(_BF16)
    h_ref[HALO:HALO + tl] = _norm_mod(x_ref[0], g, sf, cf).astype(_BF16)
    h_ref[HALO + tl:] = (_norm_mod(xn_ref[0], g, sf, cf) * nv).astype(_BF16)

    def up_proj(c, slot):
        lo = pl.multiple_of(c * FFN_CHUNK, FFN_CHUNK)
        up_ref[slot] = jnp.dot(h_ref[...], wup_ref[:, pl.ds(lo, FFN_CHUNK)], preferred_element_type=_F32)
        uv_ref[slot] = jnp.dot(h_ref[HALO:HALO + tl], wup_ref[:, pl.ds(D_FF + lo, FFN_CHUNK)],
                               preferred_element_type=_F32)

    def gate(c, slot):
        lo = pl.multiple_of(c * FFN_CHUNK, FFN_CHUNK)
        cw = cw_ref[:, pl.ds(lo, FFN_CHUNK)]
        cb = cb_ref[:, pl.ds(lo, FFN_CHUNK)]
        gt = (up_ref[slot, HALO - 1:HALO - 1 + tl] * cw[0:1] + up_ref[slot, HALO:HALO + tl] * cw[1:2]
              + up_ref[slot, HALO + 1:HALO + 1 + tl] * cw[2:3] + cb)
        act_ref[:, pl.ds(lo, FFN_CHUNK)] = (_silu(gt) * uv_ref[slot]).astype(_BF16)

    n_chunks = D_FF // FFN_CHUNK
    up_proj(0, 0)

    def pair(it, carry):
        c = 2 * it
        up_proj(c + 1, 1)
        gate(c, 0)
        up_proj(c + 2, 0)
        gate(c + 1, 1)
        return carry

    lax.fori_loop(0, (n_chunks - 1) // 2, pair, 0)
    assert n_chunks % 2 == 1
    gate(n_chunks - 1, 0)
    y = jnp.dot(act_ref[...], wdn_ref[...], preferred_element_type=_F32)
    o_ref[0] = x_ref[0] + gf_ref[0] * y


def _ffn(x, g, sf, cf, gf, w_up, conv_w, conv_b, w_down, tl):
    b, l, d = x.shape
    nh = tl // HALO
    last = l // HALO - 1
    vec = pl.BlockSpec((1, 1, d), lambda i, j: (i, 0, 0))
    whole = lambda a: pl.BlockSpec(a.shape, lambda i, j: (0,) * a.ndim)
    once = lambda a: pl.BlockSpec(a.shape, lambda i, j: (0,) * a.ndim, pipeline_mode=pl.Buffered(1))
    return pl.pallas_call(
        functools.partial(_ffn_kernel, tl=tl),
        grid=(b, l // tl),
        in_specs=[
            pl.BlockSpec((1, HALO, d), lambda i, j: (i, jnp.maximum(j * nh - 1, 0), 0)),
            pl.BlockSpec((1, tl, d), lambda i, j: (i, j, 0)),
            pl.BlockSpec((1, HALO, d), lambda i, j: (i, jnp.minimum((j + 1) * nh, last), 0)),
            pl.BlockSpec((1, d), lambda i, j: (0, 0)),
            vec, vec, vec,
            once(w_up), whole(conv_w), whole(conv_b), once(w_down),
        ],
        out_specs=pl.BlockSpec((1, tl, d), lambda i, j: (i, j, 0)),
        out_shape=jax.ShapeDtypeStruct((b, l, d), _F32),
        scratch_shapes=[
            pltpu.VMEM((tl + 2 * HALO, d), _BF16),
            pltpu.VMEM((2, tl + 2 * HALO, FFN_CHUNK), _F32),
            pltpu.VMEM((2, tl, FFN_CHUNK), _F32),
            pltpu.VMEM((tl, D_FF), _BF16),
        ],
        compiler_params=_cparams(2),
        name="conv_ffn",
    )(x, x, x, g, sf, cf, gf, w_up, conv_w, conv_b, w_down)


def _pad_row(v, n=LANES):
    v = v.reshape(1, -1)
    return jnp.pad(v, ((0, 0), (0, n - v.shape[1])))


def kernel(x, c, ctx, c_ctx, ada_w, ada_b, mix_norm_g, w_in, gdn_conv_w, gdn_a_log, gdn_dt_bias, gdn_norm_g, mla_q_norm_g, mla_w_uq, mla_kv_norm_g, mla_w_ukv, mla_q_head_g, mla_k_head_g, hy_conv_w, hy_conv_b, hy_w1, hy_b1, hy_w2, hy_b2, hy_w3, hy_b3, hy_d, w_out, ffn_norm_g, ffn_w_up, ffn_conv_w, ffn_conv_b, ffn_w_down):
    bsz, seq, d = x.shape
    n_ctx = ctx.shape[1]
    assert n_ctx == TOK and seq % TOK == 0 and bsz < ADA_ROWS
    cond = jnp.concatenate([c, c_ctx[None, :], jnp.zeros((ADA_ROWS - bsz - 1, d), c.dtype)], axis=0)
    for i in range(DEPTH):
        last = i == DEPTH - 1
        mod = _ada_mod(cond, ada_w[i], ada_b[i])
        mod_lat = mod[:bsz, None, :]
        mod_ctx = mod[bsz][None, None, :]
        sa_l, ca_l, ga_l, sf_l, cf_l, gf_l = jnp.split(mod_lat, 6, axis=-1)
        sa_c, ca_c, ga_c, sf_c, cf_c, gf_c = (jnp.broadcast_to(t, (bsz, 1, d)) for t in jnp.split(mod_ctx, 6, axis=-1))

        w_in_p = _pad_w_in(w_in[i])
        g_mix = mix_norm_g[i][None, :]
        qkv, z, ab, mla_in, hy_in = _in_proj(x, ctx, g_mix, sa_l, ca_l, sa_c[:1], ca_c[:1], w_in_p)

        q, k, v, gf, gb, bf, bb = _gdn_prep(qkv, ab, gdn_conv_w[i], _pad_row(gdn_a_log[i]), _pad_row(gdn_dt_bias[i]))
        oc_f, oc_b, ol_f, ol_b = _gdn_scan(q, k, v, gf, gb, bf, bb)

        mla_w = _mla_weights(mla_w_uq[i], mla_w_ukv[i], mla_q_head_g[i], mla_k_head_g[i])
        q_ctx, q_lat, k_all, v_all = _mla_prep(mla_in, mla_q_norm_g[i], mla_kv_norm_g[i], mla_w, seq)
        mla_l = _attention(q_lat, k_all, v_all, n_ctx + seq, ATTN_TQ)
        hy_x0, hy_z, hy_zc, hy_zl = _hy_prep(hy_in, hy_conv_w[i], hy_conv_b[i])
        hy_mlp = (hy_w1[i], hy_b1[i], hy_w2[i], hy_b2[i], hy_w3[i], hy_b3[i])
        hy_l = _hy_conv(_hy_filter(seq, *hy_mlp), hy_zl)
        hd_row = hy_d[i][None, :]

        w_out_b = w_out[i].astype(_BF16)
        w_up_b = ffn_w_up[i].astype(_BF16)
        w_dn_b = ffn_w_down[i].astype(_BF16)
        g_ffn = ffn_norm_g[i][None, :]
        cb = ffn_conv_b[i][None, :]
        gn_row = jnp.tile(gdn_norm_g[i], GDN_HEADS)[None, :]

        x = _out_proj(x, ol_f, ol_b, z, gn_row, mla_l, hy_x0, hy_z, hy_l, hd_row, ga_l, w_out_b, 1)
        x = _ffn(x, g_ffn, sf_l, cf_l, gf_l, w_up_b, ffn_conv_w[i], cb, w_dn_b, FFN_TL)

        if not last:
            mla_c = _attention(q_ctx, k_all, v_all, n_ctx, TOK)
            hy_c = _hy_conv(_hy_filter(n_ctx, *hy_mlp), hy_zc)
            ctx = _out_proj(ctx, oc_f, oc_b, z, gn_row, mla_c, hy_x0, hy_z, hy_c, hd_row, ga_c, w_out_b, 0)
            ctx = _ffn(ctx, g_ffn, sf_c, cf_c, gf_c, w_up_b, ffn_conv_w[i], cb, w_dn_b, TOK)
    return x
```
